```python
import math
import jax, jax.numpy as jnp
from jax import lax
import numpy as np

D_MODEL = 2048
BATCH = 8
SEQ = 2048
DEPTH = 1
DEC_BATCH = 16
DEC_SEQ = 64
PAST_LEN = 2048

CHUNK = 64
HEAD_DIM = 128
N_HEADS = D_MODEL // 256
N_KV_HEADS = 2
GROUP = N_HEADS // N_KV_HEADS
ATT_WIDTH = N_HEADS * HEAD_DIM
IDX_HEADS = 16
IDX_DIM = 64
TOPK_MAX = 256
S5_CH = 16
S5_WIDTH = D_MODEL // 2
S5_GROUPS = S5_WIDTH // S5_CH
S5_STATE = 64
D_FF = -(-8 * D_MODEL // (3 * 256)) * 256
ROPE_THETA = 10000.0
EPS = 1e-6
IN_SIZES = (ATT_WIDTH, N_KV_HEADS * HEAD_DIM, N_KV_HEADS * HEAD_DIM, IDX_HEADS * IDX_DIM,
            IDX_DIM, IDX_HEADS, S5_WIDTH, 2 * D_MODEL)
IN_WIDTH = sum(IN_SIZES)
IN_SPLITS = tuple(int(s) for s in np.cumsum(IN_SIZES)[:-1])

kernel_name = 'hybrid_dsa_s5_streaming_step'


def rms_norm(x, g):
    xf = x.astype(jnp.float32)
    y = xf * lax.rsqrt(jnp.mean(xf * xf, axis=-1, keepdims=True) + EPS)
    return (y * g.astype(jnp.float32)).astype(x.dtype)


def rope(x, pos):
    dim = x.shape[-1]
    half = dim // 2
    inv = 1.0 / (ROPE_THETA ** (jnp.arange(half, dtype=jnp.float32) * (2.0 / dim)))
    ang = pos[:, None] * inv[None, :]
    cos = jnp.cos(ang)[:, None, :]
    sin = jnp.sin(ang)[:, None, :]
    xf = x.astype(jnp.float32)
    x1, x2 = xf[..., :half], xf[..., half:]
    return jnp.concatenate([x1 * cos - x2 * sin, x1 * sin + x2 * cos], axis=-1).astype(x.dtype)


def dsa_attend(q, iq, iw, k, v, ik, mask, topk):
    n, nq = q.shape[0], q.shape[1]
    idx_logits = jnp.einsum('nqhd,nsd->nqhs', iq, ik) * (IDX_DIM ** -0.5)
    score = jnp.einsum('nqhs,nqh->nqs', jax.nn.relu(idx_logits), iw).astype(jnp.float32)
    score = jnp.where(mask[None], score, -jnp.inf)
    top_val, sel = lax.top_k(score, topk)
    ok = top_val > -jnp.inf
    take = jax.vmap(lambda a, i: a[i])
    kg = take(k, sel)
    vg = take(v, sel)
    qg = q.reshape(n, nq, N_KV_HEADS, GROUP, HEAD_DIM)
    logits = jnp.einsum('nqcgd,nqkcd->nqcgk', qg, kg).astype(jnp.float32) * (HEAD_DIM ** -0.5)
    logits = jnp.where(ok[:, :, None, None, :], logits, -jnp.inf)
    p = jax.nn.softmax(logits, axis=-1).astype(v.dtype)
    o = jnp.einsum('nqcgk,nqkcd->nqcgd', p, vg)
    return o.reshape(n, nq, ATT_WIDTH)


def dsa_prompt(q, iq, iw, k, v, ik):
    n, t = q.shape[0], q.shape[1]
    nblk = t // CHUNK
    topk = min(TOPK_MAX, t // 4)
    key_pos = jnp.arange(t)

    def to_blocks(a):
        return jnp.moveaxis(a.reshape(n, nblk, CHUNK, *a.shape[2:]), 1, 0)

    def one_block(args):
        j, qb, iqb, iwb = args
        mask = jnp.broadcast_to(key_pos[None, :] < (j + 1) * CHUNK, (CHUNK, t))
        return dsa_attend(qb, iqb, iwb, k, v, ik, mask, topk)

    out = lax.map(one_block, (jnp.arange(nblk), to_blocks(q), to_blocks(iq), to_blocks(iw)))
    return jnp.moveaxis(out, 0, 1).reshape(n, t, ATT_WIDTH)


def s5_scan(u, x0_re, x0_im, a_re, a_im, log_dt, b_re, b_im, c_re, c_im, d_skip):
    n, t, _ = u.shape
    ug = u.reshape(n, t, S5_GROUPS, S5_CH)
    dt = jnp.exp(log_dt)[:, None]
    mag = jnp.exp(dt * a_re)
    ab_re = mag * jnp.cos(dt * a_im)
    ab_im = mag * jnp.sin(dt * a_im)
    den = a_re * a_re + a_im * a_im
    f_re = ((ab_re - 1.0) * a_re + ab_im * a_im) / den
    f_im = (ab_im * a_re - (ab_re - 1.0) * a_im) / den
    bb_re = f_re[..., None] * b_re - f_im[..., None] * b_im
    bb_im = f_re[..., None] * b_im + f_im[..., None] * b_re
    bu_re = jnp.einsum('ntgc,gpc->ntgp', ug, bb_re)
    bu_im = jnp.einsum('ntgc,gpc->ntgp', ug, bb_im)
    bu_re = bu_re.at[:, 0].add(ab_re * x0_re - ab_im * x0_im)
    bu_im = bu_im.at[:, 0].add(ab_re * x0_im + ab_im * x0_re)
    a_re_t = jnp.broadcast_to(ab_re, bu_re.shape)
    a_im_t = jnp.broadcast_to(ab_im, bu_im.shape)

    def combine(e1, e2):
        a1r, a1i, b1r, b1i = e1
        a2r, a2i, b2r, b2i = e2
        return (a2r * a1r - a2i * a1i,
                a2r * a1i + a2i * a1r,
                a2r * b1r - a2i * b1i + b2r,
                a2r * b1i + a2i * b1r + b2i)

    _, _, x_re, x_im = lax.associative_scan(combine, (a_re_t, a_im_t, bu_re, bu_im), axis=1)
    y = (jnp.einsum('ntgp,gcp->ntgc', x_re, c_re) - jnp.einsum('ntgp,gcp->ntgc', x_im, c_im)
         + d_skip.reshape(S5_GROUPS, S5_CH) * ug)
    return y.reshape(n, t, S5_WIDTH), x_re[:, -1], x_im[:, -1]


def trunk_layer(x, pos, past, ssm0, lw):
    n, t, _ = x.shape
    h = rms_norm(x, lw['g_mix'])
    z = h @ lw['w_in']
    q, k, v, iq, ik, iw, u, gl = jnp.split(z, IN_SPLITS, axis=-1)
    q = rope(q.reshape(n, t, N_HEADS, HEAD_DIM), pos)
    k = rope(k.reshape(n, t, N_KV_HEADS, HEAD_DIM), pos)
    v = v.reshape(n, t, N_KV_HEADS, HEAD_DIM)
    iq = rope(iq.reshape(n, t, IDX_HEADS, IDX_DIM), pos)
    ik = rope(ik[:, :, None, :], pos)[:, :, 0, :]
    iw = iw * (IDX_HEADS ** -0.5)
    if past is None:
        o_a = dsa_prompt(q, iq, iw, k, v, ik)
        x0_re = jnp.zeros((n, S5_GROUPS, S5_STATE), x.dtype)
        x0_im = jnp.zeros((n, S5_GROUPS, S5_STATE), x.dtype)
    else:
        pk, pv, pik = past
        k_all = jnp.concatenate([pk, k], axis=1)
        v_all = jnp.concatenate([pv, v], axis=1)
        ik_all = jnp.concatenate([pik, ik], axis=1)
        l_keys = k_all.shape[1]
        mask = jnp.ones((t, l_keys), dtype=bool)
        o_a = dsa_attend(q, iq, iw, k_all, v_all, ik_all, mask, min(TOPK_MAX, l_keys // 4))
        x0_re, x0_im = ssm0
    y_b, s_re, s_im = s5_scan(u, x0_re, x0_im, lw['a_re'], lw['a_im'], lw['log_dt'],
                              lw['b_re'], lw['b_im'], lw['c_re'], lw['c_im'], lw['d_skip'])
    y_b = jax.nn.gelu(y_b)
    o_b = y_b * jax.nn.sigmoid(y_b @ lw['w_glu'] + lw['b_glu'])
    g_a, g_b = jnp.split(jax.nn.sigmoid(gl), 2, axis=-1)
    merged = g_a * (o_a @ lw['w_proj_a']) + g_b * (o_b @ lw['w_proj_b'])
    x = x + merged @ lw['w_out']
    hf = rms_norm(x, lw['g_ffn'])
    x = x + (jax.nn.silu(hf @ lw['w_gate']) * (hf @ lw['w_up'])) @ lw['w_down']
    return x, (k, v, ik, s_re, s_im)


def setup_inputs(seed: int = 0) -> dict:
    key = jax.random.key(seed)
    ks = jax.random.split(key, 32)
    nrm = jax.random.normal
    f32 = jnp.float32
    a_im_base = math.pi * jnp.arange(S5_STATE, dtype=f32)
    return {
        'x_prompt': nrm(ks[0], (BATCH, SEQ, D_MODEL), f32),
        'x_sample': nrm(ks[1], (DEC_BATCH, DEC_SEQ, D_MODEL), f32),
        'cache_k': nrm(ks[2], (DEPTH, DEC_BATCH, PAST_LEN, N_KV_HEADS, HEAD_DIM), f32),
        'cache_v': nrm(ks[3], (DEPTH, DEC_BATCH, PAST_LEN, N_KV_HEADS, HEAD_DIM), f32),
        'cache_idx_k': nrm(ks[4], (DEPTH, DEC_BATCH, PAST_LEN, IDX_DIM), f32),
        'state_ssm_re': 0.3 * nrm(ks[5], (DEPTH, DEC_BATCH, S5_GROUPS, S5_STATE), f32),
        'state_ssm_im': 0.3 * nrm(ks[6], (DEPTH, DEC_BATCH, S5_GROUPS, S5_STATE), f32),
        'g_mix': 1.0 + 0.01 * nrm(ks[7], (DEPTH, D_MODEL), f32),
        'w_in': nrm(ks[8], (DEPTH, D_MODEL, IN_WIDTH), f32) * D_MODEL ** -0.5,
        'a_re': -0.5 + 0.01 * nrm(ks[9], (DEPTH, S5_GROUPS, S5_STATE), f32),
        'a_im': a_im_base + 0.01 * nrm(ks[10], (DEPTH, S5_GROUPS, S5_STATE), f32),
        'log_dt': jax.random.uniform(ks[11], (DEPTH, S5_GROUPS), f32,
                                     minval=math.log(1e-3), maxval=math.log(1e-1)),
        'b_re': nrm(ks[12], (DEPTH, S5_GROUPS, S5_STATE, S5_CH), f32) * (2 * S5_CH) ** -0.5,
        'b_im': nrm(ks[13], (DEPTH, S5_GROUPS, S5_STATE, S5_CH), f32) * (2 * S5_CH) ** -0.5,
        'c_re': nrm(ks[14], (DEPTH, S5_GROUPS, S5_CH, S5_STATE), f32) * S5_STATE ** -0.5,
        'c_im': nrm(ks[15], (DEPTH, S5_GROUPS, S5_CH, S5_STATE), f32) * S5_STATE ** -0.5,
        'd_skip': nrm(ks[16], (DEPTH, S5_WIDTH), f32),
        'w_glu': nrm(ks[17], (DEPTH, S5_WIDTH, S5_WIDTH), f32) * S5_WIDTH ** -0.5,
        'b_glu': 0.01 * nrm(ks[18], (DEPTH, S5_WIDTH), f32),
        'w_proj_a': nrm(ks[19], (DEPTH, ATT_WIDTH, D_MODEL), f32) * ATT_WIDTH ** -0.5,
        'w_proj_b': nrm(ks[20], (DEPTH, S5_WIDTH, D_MODEL), f32) * S5_WIDTH ** -0.5,
        'w_out': nrm(ks[21], (DEPTH, D_MODEL, D_MODEL), f32) * D_MODEL ** -0.5,
        'g_ffn': 1.0 + 0.01 * nrm(ks[22], (DEPTH, D_MODEL), f32),
        'w_gate': nrm(ks[23], (DEPTH, D_MODEL, D_FF), f32) * D_MODEL ** -0.5,
        'w_up': nrm(ks[24], (DEPTH, D_MODEL, D_FF), f32) * D_MODEL ** -0.5,
        'w_down': nrm(ks[25], (DEPTH, D_FF, D_MODEL), f32) * D_FF ** -0.5,
        'g_final': 1.0 + 0.01 * nrm(ks[26], (D_MODEL,), f32),
    }


def reference(x_prompt, x_sample, cache_k, cache_v, cache_idx_k, state_ssm_re, state_ssm_im,
              g_mix, w_in, a_re, a_im, log_dt, b_re, b_im, c_re, c_im, d_skip, w_glu, b_glu,
              w_proj_a, w_proj_b, w_out, g_ffn, w_gate, w_up, w_down, g_final):
    t_p = x_prompt.shape[1]
    t_s = x_sample.shape[1]
    past_len = cache_k.shape[2]
    pos_p = jnp.arange(t_p, dtype=jnp.float32)
    pos_s = past_len + jnp.arange(t_s, dtype=jnp.float32)
    hp, hs = x_prompt, x_sample
    kp_l, vp_l, ikp_l, srp_l, sip_l = [], [], [], [], []
    ks_l, vs_l, iks_l, srs_l, sis_l = [], [], [], [], []
    for l in range(DEPTH):
        lw = {'g_mix': g_mix[l], 'w_in': w_in[l], 'a_re': a_re[l], 'a_im': a_im[l],
              'log_dt': log_dt[l], 'b_re': b_re[l], 'b_im': b_im[l], 'c_re': c_re[l],
              'c_im': c_im[l], 'd_skip': d_skip[l], 'w_glu': w_glu[l], 'b_glu': b_glu[l],
              'w_proj_a': w_proj_a[l], 'w_proj_b': w_proj_b[l], 'w_out': w_out[l],
              'g_ffn': g_ffn[l], 'w_gate': w_gate[l], 'w_up': w_up[l], 'w_down': w_down[l]}
        hp, (kp, vp, ikp, srp, sip) = trunk_layer(hp, pos_p, None, None, lw)
        hs, (kn, vn, ikn, srn, sin_) = trunk_layer(
            hs, pos_s, (cache_k[l], cache_v[l], cache_idx_k[l]),
            (state_ssm_re[l], state_ssm_im[l]), lw)
        kp_l.append(kp); vp_l.append(vp); ikp_l.append(ikp); srp_l.append(srp); sip_l.append(sip)
        ks_l.append(kn); vs_l.append(vn); iks_l.append(ikn); srs_l.append(srn); sis_l.append(sin_)
    y_prompt = rms_norm(hp, g_final)
    y_sample = rms_norm(hs, g_final)
    return (y_prompt, y_sample,
            jnp.stack(kp_l), jnp.stack(vp_l), jnp.stack(ikp_l), jnp.stack(srp_l), jnp.stack(sip_l),
            jnp.stack(ks_l), jnp.stack(vs_l), jnp.stack(iks_l), jnp.stack(srs_l), jnp.stack(sis_l))
```

```python
import functools
import math

import jax
import jax.numpy as jnp
import numpy as np
from jax import lax
from jax.experimental import pallas as pl
from jax.experimental.pallas import tpu as pltpu

F32 = jnp.float32
BF16 = jnp.bfloat16

D_MODEL = 2048
CHUNK = 64
HEAD_DIM = 128
N_HEADS = 8
N_KV_HEADS = 2
GROUP = N_HEADS // N_KV_HEADS
ATT_WIDTH = N_HEADS * HEAD_DIM
KV_WIDTH = N_KV_HEADS * HEAD_DIM
IDX_HEADS = 16
IDX_DIM = 64
IDX_WIDTH = IDX_HEADS * IDX_DIM
TOPK_MAX = 256
S5_CH = 16
S5_WIDTH = D_MODEL // 2
S5_GROUPS = S5_WIDTH // S5_CH
S5_STATE = 64
S5_LANES = S5_GROUPS * S5_STATE
D_FF = 5632
ROPE_THETA = 10000.0
EPS = 1e-6
IN_SIZES = (ATT_WIDTH, KV_WIDTH, KV_WIDTH, IDX_WIDTH, IDX_DIM, IDX_HEADS, S5_WIDTH, 2 * D_MODEL)
IN_OFFS = tuple(int(s) for s in np.cumsum((0,) + IN_SIZES))

LANES = 128
S5_SLAB_GROUPS = LANES // S5_CH
S5_SLABS = S5_GROUPS // S5_SLAB_GROUPS
S5_SLAB_STATE = S5_SLAB_GROUPS * S5_STATE
INT_MIN = -2 ** 31

VMEM_LIMIT = 56 * 2 ** 20


def _cparams(*sem):
    return pltpu.CompilerParams(dimension_semantics=sem, vmem_limit_bytes=VMEM_LIMIT)


def _const_spec(shape):
    nd = len(shape)
    return pl.BlockSpec(shape, lambda *_: (0,) * nd, pipeline_mode=pl.Buffered(1))


_PQ = 0
_PK = _PQ + ATT_WIDTH
_PV = _PK + KV_WIDTH
_PIQ = _PV + KV_WIDTH
_PIK = _PIQ + IDX_WIDTH
_PIW = _PIK + LANES
_PU = _PIW + LANES
_PEND = _PU + S5_WIDTH
IW_SCALE = (IDX_DIM ** -0.5) * (IDX_HEADS ** -0.5)


def _rope128(z, cos, sin):
    return z * cos + pltpu.roll(z, HEAD_DIM // 2, 1) * sin


def _rope64(z, cos, sin, low_half):
    partner = jnp.where(low_half, pltpu.roll(z, LANES - IDX_DIM // 2, 1), pltpu.roll(z, IDX_DIM // 2, 1))
    return z * cos + partner * sin


def _proj_body(x_ref, g_ref, w_ref, cq_ref, sq_ref, ci_ref, si_ref,
               h_ref, q_ref, k_ref, v_ref, kb_ref, vb_ref, iq_ref, ik_ref, iw_ref, u_ref):
    x = x_ref[...]
    h = (x * lax.rsqrt(jnp.mean(x * x, axis=-1, keepdims=True) + EPS)) * g_ref[...]
    hb = h.astype(BF16)
    h_ref[...] = hb

    def proj(lo, hi):
        return jnp.dot(hb, w_ref[:, lo:hi], preferred_element_type=F32)

    cq, sq, ci, si = cq_ref[...], sq_ref[...], ci_ref[...], si_ref[...]
    low_half = (lax.broadcasted_iota(jnp.int32, cq.shape, 1) & (IDX_DIM - 1)) < (IDX_DIM // 2)

    zq = proj(_PQ, _PK)
    for hd in range(N_HEADS):
        sl = slice(hd * HEAD_DIM, (hd + 1) * HEAD_DIM)
        q_ref[:, sl] = _rope128(zq[:, sl], cq, sq).astype(BF16)
    zk = proj(_PK, _PV)
    for hd in range(N_KV_HEADS):
        sl = slice(hd * HEAD_DIM, (hd + 1) * HEAD_DIM)
        r = _rope128(zk[:, sl], cq, sq)
        k_ref[:, sl] = r
        kb_ref[:, sl] = r.astype(BF16)
    zv = proj(_PV, _PIQ)
    v_ref[...] = zv
    vb_ref[...] = zv.astype(BF16)
    ziq = proj(_PIQ, _PIK)
    for p in range(IDX_WIDTH // LANES):
        sl = slice(p * LANES, (p + 1) * LANES)
        iq_ref[:, sl] = _rope64(ziq[:, sl], ci, si, low_half).astype(BF16)
    z2 = proj(_PIK, _PU)
    ik_ref[...] = _rope64(z2[:, :LANES], ci, si, low_half)[:, :IDX_DIM]
    iw_ref[...] = z2[:, LANES:] * IW_SCALE
    u_ref[...] = proj(_PU, _PEND)


def _proj_call(x2d, g_mix, w_pack, tabs, tm, tiles_per_seq, nb, u_time_major):
    m = x2d.shape[0]
    n_tiles = m // tm
    tab_tiles = tabs[0].shape[0] // tm
    row = lambda w: pl.BlockSpec((tm, w), lambda i: (i, 0))
    tab_spec = pl.BlockSpec((tm, LANES), lambda i: (i % tab_tiles, 0))
    if u_time_major:
        t_len = tiles_per_seq * tm
        u_shape = jax.ShapeDtypeStruct((t_len, nb * S5_WIDTH), F32)
        u_spec = pl.BlockSpec((tm, S5_WIDTH), lambda i: (i % tiles_per_seq, i // tiles_per_seq))
    else:
        u_shape = jax.ShapeDtypeStruct((m, S5_WIDTH), F32)
        u_spec = row(S5_WIDTH)
    out_shape = (
        jax.ShapeDtypeStruct((m, D_MODEL), BF16),
        jax.ShapeDtypeStruct((m, ATT_WIDTH), BF16),
        jax.ShapeDtypeStruct((m, KV_WIDTH), F32),
        jax.ShapeDtypeStruct((m, KV_WIDTH), F32),
        jax.ShapeDtypeStruct((m, KV_WIDTH), BF16),
        jax.ShapeDtypeStruct((m, KV_WIDTH), BF16),
        jax.ShapeDtypeStruct((m, IDX_WIDTH), BF16),
        jax.ShapeDtypeStruct((m, IDX_DIM), F32),
        jax.ShapeDtypeStruct((m, LANES), F32),
        u_shape,
    )
    out_specs = (row(D_MODEL), row(ATT_WIDTH), row(KV_WIDTH), row(KV_WIDTH), row(KV_WIDTH), row(KV_WIDTH),
                 row(IDX_WIDTH), row(IDX_DIM), row(LANES), u_spec)
    return pl.pallas_call(
        _proj_body,
        grid=(n_tiles,),
        in_specs=[row(D_MODEL), _const_spec((1, D_MODEL)), _const_spec(w_pack.shape),
                  tab_spec, tab_spec, tab_spec, tab_spec],
        out_specs=out_specs,
        out_shape=out_shape,
        compiler_params=_cparams("parallel"),
        name="proj",
    )(x2d, g_mix, w_pack, *tabs)


def _gates_body(h_ref, w_ref, o_ref):
    o_ref[...] = jax.nn.sigmoid(jnp.dot(h_ref[...], w_ref[...], preferred_element_type=F32))


def _gates_call(h, w_gl, tm, tn):
    m, n = h.shape[0], w_gl.shape[1]
    return pl.pallas_call(
        _gates_body,
        grid=(m // tm, n // tn),
        in_specs=[pl.BlockSpec((tm, D_MODEL), lambda i, j: (i, 0)),
                  pl.BlockSpec((D_MODEL, tn), lambda i, j: (0, j))],
        out_specs=pl.BlockSpec((tm, tn), lambda i, j: (i, j)),
        out_shape=jax.ShapeDtypeStruct((m, n), F32),
        compiler_params=_cparams("parallel", "arbitrary"),
        name="gates",
    )(h, w_gl)


def _dsa_body(q_ref, iq_ref, iw_ref, kb_ref, vb_ref, ikbd_ref, o_ref, score_ref, *,
              s_keys, s_chunk, blk0, fixed_valid, topk):
    if fixed_valid is None:
        valid = (blk0 + pl.program_id(1) + 1) * CHUNK
    else:
        valid = fixed_valid

    iq = iq_ref[...]
    pairs = IDX_WIDTH // LANES
    lhs = jnp.concatenate([iq[:, p * LANES:(p + 1) * LANES] for p in range(pairs)], axis=0)
    iw = iw_ref[...]
    for c in range(s_keys // s_chunk):
        logits = lax.dot_general(lhs, ikbd_ref[c], (((1,), (1,)), ((), ())), preferred_element_type=F32)
        acc = jnp.zeros((CHUNK, s_chunk), F32)
        for p in range(pairs):
            lp = logits[p * CHUNK:(p + 1) * CHUNK]
            acc = acc + jnp.maximum(lp[:, :s_chunk], 0.0) * iw[:, 2 * p:2 * p + 1]
            acc = acc + jnp.maximum(lp[:, s_chunk:], 0.0) * iw[:, 2 * p + 1:2 * p + 2]
        col = c * s_chunk + lax.broadcasted_iota(jnp.int32, acc.shape, 1)
        score_ref[:, c * s_chunk:(c + 1) * s_chunk] = jnp.where(col < valid, acc, -jnp.inf)

    score = score_ref[...]
    admissible = score > -jnp.inf
    bits = lax.bitcast_convert_type(score, jnp.int32)
    key = bits ^ ((bits >> 31) & jnp.int32(0x7FFFFFFF))

    def count(mask):
        return jnp.sum(jnp.where(mask, 1.0, 0.0), axis=1, keepdims=True)

    kf = float(topk)
    thr = jnp.full((CHUNK, 1), INT_MIN, jnp.int32)
    for b in range(31, -1, -1):
        cand = thr + jnp.int32(INT_MIN if b == 31 else 1 << b)
        thr = jnp.where(count(key >= cand) >= kf, cand, thr)
    above = key > thr
    tie = key == thr
    need = kf - count(above)
    col = lax.broadcasted_iota(jnp.int32, key.shape, 1)
    last_tie = jnp.zeros((CHUNK, 1), jnp.int32)
    for b in range(int(s_keys).bit_length() - 1, -1, -1):
        cand = last_tie + jnp.int32(1 << b)
        last_tie = jnp.where(count(tie & (col < cand)) < need, cand, last_tie)
    sel = (above | (tie & (col <= last_tie))) & admissible

    q = q_ref[...]
    scale = HEAD_DIM ** -0.5
    for c in range(N_KV_HEADS):
        kc = kb_ref[:, c * HEAD_DIM:(c + 1) * HEAD_DIM]
        vc = vb_ref[:, c * HEAD_DIM:(c + 1) * HEAD_DIM]
        qc = jnp.concatenate(
            [q[:, (c * GROUP + g) * HEAD_DIM:(c * GROUP + g + 1) * HEAD_DIM] for g in range(GROUP)], axis=0)
        logits = lax.dot_general(qc, kc, (((1,), (1,)), ((), ())), preferred_element_type=F32) * scale
        ps = []
        for g in range(GROUP):
            lg = jnp.where(sel, logits[g * CHUNK:(g + 1) * CHUNK], -jnp.inf)
            e = jnp.exp(lg - jnp.max(lg, axis=1, keepdims=True))
            ps.append((e / jnp.sum(e, axis=1, keepdims=True)).astype(BF16))
        oc = jnp.dot(jnp.concatenate(ps, axis=0), vc, preferred_element_type=F32)
        for g in range(GROUP):
            hd = c * GROUP + g
            o_ref[:, hd * HEAD_DIM:(hd + 1) * HEAD_DIM] = oc[g * CHUNK:(g + 1) * CHUNK].astype(BF16)


def _dsa_call(q, iq, iw, kb, vb, ikbd, nb, blk0, n_blk, blks_per_seq, s_keys, s_chunk, fixed_valid, topk):
    qrow = lambda w: pl.BlockSpec((CHUNK, w), lambda n, j: (n * blks_per_seq + blk0 + j, 0))
    n_chunks = s_keys // s_chunk
    body = functools.partial(_dsa_body, s_keys=s_keys, s_chunk=s_chunk, blk0=blk0,
                             fixed_valid=fixed_valid, topk=topk)
    return pl.pallas_call(
        body,
        grid=(nb, n_blk),
        in_specs=[qrow(ATT_WIDTH), qrow(IDX_WIDTH), qrow(LANES),
                  pl.BlockSpec((None, s_keys, KV_WIDTH), lambda n, j: (n, 0, 0)),
                  pl.BlockSpec((None, s_keys, KV_WIDTH), lambda n, j: (n, 0, 0)),
                  pl.BlockSpec((None, n_chunks, 2 * s_chunk, LANES), lambda n, j: (n, 0, 0, 0))],
        out_specs=pl.BlockSpec((CHUNK, ATT_WIDTH), lambda n, j: (n * n_blk + j, 0)),
        out_shape=jax.ShapeDtypeStruct((nb * n_blk * CHUNK, ATT_WIDTH), BF16),
        scratch_shapes=[pltpu.VMEM((CHUNK, s_keys), F32)],
        compiler_params=_cparams("parallel", "arbitrary"),
        name="dsa",
    )(q, iq, iw, kb, vb, ikbd)


def _indexer_key_blocks(ik, s_chunk):
    nb, s, _ = ik.shape
    ikb = ik.astype(BF16).reshape(nb, s // s_chunk, s_chunk, IDX_DIM)
    z = jnp.zeros_like(ikb)
    return jnp.concatenate([jnp.concatenate([ikb, z], axis=-1), jnp.concatenate([z, ikb], axis=-1)], axis=-2)


def _s5prep_body(are_ref, aim_ref, ldt_ref, bre_ref, bim_ref, abre_ref, abim_ref, bbre_ref, bbim_ref):
    a_re, a_im = are_ref[...], aim_ref[...]
    dt = jnp.exp(ldt_ref[...])
    mag = jnp.exp(dt * a_re)
    ab_re = mag * jnp.cos(dt * a_im)
    ab_im = mag * jnp.sin(dt * a_im)
    den = a_re * a_re + a_im * a_im
    f_re = ((ab_re - 1.0) * a_re + ab_im * a_im) / den
    f_im = (ab_im * a_re - (ab_re - 1.0) * a_im) / den
    abre_ref[...] = ab_re
    abim_ref[...] = ab_im
    for c in range(S5_CH):
        b_re, b_im = bre_ref[c], bim_ref[c]
        bbre_ref[c] = f_re * b_re - f_im * b_im
        bbim_ref[c] = f_re * b_im + f_im * b_re


def _s5prep_call(a_re, a_im, log_dt, b_re, b_im):
    rows = S5_LANES // LANES
    flat = lambda a: a.reshape(rows, LANES)
    ldt = jnp.broadcast_to(log_dt[:, None], (S5_GROUPS, S5_STATE))
    chan_major = lambda b: jnp.transpose(b, (2, 0, 1)).reshape(S5_CH, rows, LANES)
    small = jax.ShapeDtypeStruct((rows, LANES), F32)
    big = jax.ShapeDtypeStruct((S5_CH, rows, LANES), F32)
    return pl.pallas_call(
        _s5prep_body,
        out_shape=(small, small, big, big),
        name="s5prep",
    )(flat(a_re), flat(a_im), flat(ldt), chan_major(b_re), chan_major(b_im))


def _gelu_tanh(x):
    return 0.5 * x * (1.0 + jnp.tanh(math.sqrt(2.0 / math.pi) * (x + 0.044715 * (x * x * x))))


def _s5_body(u_ref, x0re_ref, x0im_ref, abre_ref, abim_ref, bre_ref, bim_ref, cre_ref, cim_ref,
             dskip_ref, wglu_ref, bglu_ref, ob_ref, sre_ref, sim_ref, xre, xim, st_re, st_im, *, tc, nb, lane_w):
    step = pl.program_id(0)

    @pl.when(step == 0)
    def _():
        st_re[...] = x0re_ref[...]
        st_im[...] = x0im_ref[...]

    u = u_ref[...]
    ub = u.astype(BF16)
    for k in range(S5_SLABS):
        us = ub[:, k * LANES:(k + 1) * LANES]
        sl = slice(k * S5_SLAB_STATE, (k + 1) * S5_SLAB_STATE)
        xre[:, :, sl] = jnp.dot(us, bre_ref[k], preferred_element_type=F32).reshape(tc, nb, S5_SLAB_STATE)
        xim[:, :, sl] = jnp.dot(us, bim_ref[k], preferred_element_type=F32).reshape(tc, nb, S5_SLAB_STATE)

    for lc in range(S5_LANES // lane_w):
        sl = slice(lc * lane_w, (lc + 1) * lane_w)
        a_r = jnp.broadcast_to(abre_ref[:, sl], (nb, lane_w))
        a_i = jnp.broadcast_to(abim_ref[:, sl], (nb, lane_w))

        def scan_step(t, carry, sl=sl, a_r=a_r, a_i=a_i):
            s_r, s_i = carry
            n_r = a_r * s_r - a_i * s_i + xre[t, :, sl]
            n_i = a_r * s_i + a_i * s_r + xim[t, :, sl]
            xre[t, :, sl] = n_r
            xim[t, :, sl] = n_i
            return n_r, n_i

        s_r, s_i = lax.fori_loop(0, tc, scan_step, (st_re[:, sl], st_im[:, sl]), unroll=8)
        st_re[:, sl] = s_r
        st_im[:, sl] = s_i

    ys = []
    for k in range(S5_SLABS):
        sl = slice(k * S5_SLAB_STATE, (k + 1) * S5_SLAB_STATE)
        xr = xre[:, :, sl].reshape(tc * nb, S5_SLAB_STATE).astype(BF16)
        xi = xim[:, :, sl].reshape(tc * nb, S5_SLAB_STATE).astype(BF16)
        ys.append(jnp.dot(xr, cre_ref[k], preferred_element_type=F32)
                  - jnp.dot(xi, cim_ref[k], preferred_element_type=F32))
    y = jnp.concatenate(ys, axis=1) + dskip_ref[...] * u
    yb = _gelu_tanh(y)
    gate = jax.nn.sigmoid(jnp.dot(yb.astype(BF16), wglu_ref[...], preferred_element_type=F32) + bglu_ref[...])
    ob_ref[...] = (yb * gate).astype(BF16)

    @pl.when(step == pl.num_programs(0) - 1)
    def _():
        sre_ref[...] = st_re[...]
        sim_ref[...] = st_im[...]


def _s5_call(u_tm, x0_re, x0_im, ab_re, ab_im, b_re_bd, b_im_bd, c_re_bd, c_im_bd, d_skip, w_glu, b_glu, nb, tc):
    rows = u_tm.shape[0]
    t_len = rows // nb
    lane_w = 1024 if nb <= 8 else 512
    body = functools.partial(_s5_body, tc=tc, nb=nb, lane_w=lane_w)
    state = jax.ShapeDtypeStruct((nb, S5_LANES), F32)
    return pl.pallas_call(
        body,
        grid=(t_len // tc,),
        in_specs=[pl.BlockSpec((tc * nb, S5_WIDTH), lambda i: (i, 0)),
                  _const_spec((nb, S5_LANES)), _const_spec((nb, S5_LANES)),
                  _const_spec((1, S5_LANES)), _const_spec((1, S5_LANES)),
                  _const_spec(b_re_bd.shape), _const_spec(b_im_bd.shape),
                  _const_spec(c_re_bd.shape), _const_spec(c_im_bd.shape),
                  _const_spec((1, S5_WIDTH)), _const_spec((S5_WIDTH, S5_WIDTH)), _const_spec((1, S5_WIDTH))],
        out_specs=(pl.BlockSpec((tc * nb, S5_WIDTH), lambda i: (i, 0)),
                   _const_spec((nb, S5_LANES)), _const_spec((nb, S5_LANES))),
        out_shape=(jax.ShapeDtypeStruct((rows, S5_WIDTH), BF16), state, state),
        scratch_shapes=[pltpu.VMEM((tc, nb, S5_LANES), F32), pltpu.VMEM((tc, nb, S5_LANES), F32),
                        pltpu.VMEM((nb, S5_LANES), F32), pltpu.VMEM((nb, S5_LANES), F32)],
        compiler_params=_cparams("arbitrary"),
        name="s5",
    )(u_tm, x0_re, x0_im, ab_re, ab_im, b_re_bd, b_im_bd, c_re_bd, c_im_bd, d_skip, w_glu, b_glu)


def _block_diag_slabs(w):
    g, r, c = w.shape
    w = w.reshape(S5_SLABS, S5_SLAB_GROUPS, r, c)
    eye = jnp.eye(S5_SLAB_GROUPS, dtype=w.dtype)
    bd = w[:, :, :, None, :] * eye[None, :, None, :, None]
    return bd.reshape(S5_SLABS, S5_SLAB_GROUPS * r, S5_SLAB_GROUPS * c)


def _merge_body(oa_ref, ob_ref, ga_ref, gb_ref, x_ref, wa_ref, wb_ref, wo_ref, g_ref, x1_ref, hf_ref):
    pa = jnp.dot(oa_ref[...], wa_ref[...], preferred_element_type=F32)
    pb = jnp.dot(ob_ref[...], wb_ref[...], preferred_element_type=F32)
    merged = ga_ref[...] * pa + gb_ref[...] * pb
    x1 = x_ref[...] + jnp.dot(merged.astype(BF16), wo_ref[...], preferred_element_type=F32)
    x1_ref[...] = x1
    hf = (x1 * lax.rsqrt(jnp.mean(x1 * x1, axis=-1, keepdims=True) + EPS)) * g_ref[...]
    hf_ref[...] = hf.astype(BF16)


def _merge_call(oa, ob, gates, x2d, w_a, w_b, w_o, g_ffn, tm, tiles_per_seq, ob_time_major):
    m = x2d.shape[0]
    row = lambda w: pl.BlockSpec((tm, w), lambda i: (i, 0))
    if ob_time_major:
        ob_spec = pl.BlockSpec((tm, S5_WIDTH), lambda i: (i % tiles_per_seq, i // tiles_per_seq))
    else:
        ob_spec = row(S5_WIDTH)
    return pl.pallas_call(
        _merge_body,
        grid=(m // tm,),
        in_specs=[row(ATT_WIDTH), ob_spec,
                  pl.BlockSpec((tm, D_MODEL), lambda i: (i, 0)), pl.BlockSpec((tm, D_MODEL), lambda i: (i, 1)),
                  row(D_MODEL), _const_spec(w_a.shape), _const_spec(w_b.shape), _const_spec(w_o.shape),
                  _const_spec((1, D_MODEL))],
        out_specs=(row(D_MODEL), row(D_MODEL)),
        out_shape=(jax.ShapeDtypeStruct((m, D_MODEL), F32), jax.ShapeDtypeStruct((m, D_MODEL), BF16)),
        compiler_params=_cparams("parallel"),
        name="merge",
    )(oa, ob, gates, gates, x2d, w_a, w_b, w_o, g_ffn)


def _ffn_body(hf_ref, x1_ref, wg_ref, wu_ref, wd_ref, g_ref, y_ref, acc_ref, *, final_norm):
    f = pl.program_id(1)
    hf = hf_ref[...]
    a = jax.nn.silu(jnp.dot(hf, wg_ref[...], preferred_element_type=F32)) * jnp.dot(
        hf, wu_ref[...], preferred_element_type=F32)
    part = jnp.dot(a.astype(BF16), wd_ref[...], preferred_element_type=F32)

    @pl.when(f == 0)
    def _():
        acc_ref[...] = part

    @pl.when(f > 0)
    def _():
        acc_ref[...] += part

    @pl.when(f == pl.num_programs(1) - 1)
    def _():
        x2 = x1_ref[...] + acc_ref[...]
        if final_norm:
            x2 = (x2 * lax.rsqrt(jnp.mean(x2 * x2, axis=-1, keepdims=True) + EPS)) * g_ref[...]
        y_ref[...] = x2


def _ffn_call(hf, x1, w_gate, w_up, w_down, g_final, tm, tf, final_norm):
    m = hf.shape[0]
    return pl.pallas_call(
        functools.partial(_ffn_body, final_norm=final_norm),
        grid=(m // tm, D_FF // tf),
        in_specs=[pl.BlockSpec((tm, D_MODEL), lambda i, f: (i, 0)),
                  pl.BlockSpec((tm, D_MODEL), lambda i, f: (i, 0)),
                  pl.BlockSpec((D_MODEL, tf), lambda i, f: (0, f)),
                  pl.BlockSpec((D_MODEL, tf), lambda i, f: (0, f)),
                  pl.BlockSpec((tf, D_MODEL), lambda i, f: (f, 0)),
                  _const_spec((1, D_MODEL))],
        out_specs=pl.BlockSpec((tm, D_MODEL), lambda i, f: (i, 0)),
        out_shape=jax.ShapeDtypeStruct((m, D_MODEL), F32),
        scratch_shapes=[pltpu.VMEM((tm, D_MODEL), F32)],
        compiler_params=_cparams("parallel", "arbitrary"),
        name="ffn",
    )(hf, x1, w_gate, w_up, w_down, g_final)


def _rope_tables(pos, dim):
    half = dim // 2
    inv = 1.0 / (ROPE_THETA ** (jnp.arange(half, dtype=F32) * (2.0 / dim)))
    ang = pos[:, None] * inv[None, :]
    cos, sin = jnp.cos(ang), jnp.sin(ang)
    reps = LANES // dim
    return (jnp.tile(jnp.concatenate([cos, cos], axis=-1), (1, reps)),
            jnp.tile(jnp.concatenate([-sin, sin], axis=-1), (1, reps)))


def _pack_w_in(w_in):
    o = IN_OFFS
    seg = lambda i: w_in[:, o[i]:o[i + 1]]
    pad = lambda w: jnp.pad(w, ((0, 0), (0, LANES - w.shape[1])))
    w_pack = jnp.concatenate([seg(0), seg(1), seg(2), seg(3), pad(seg(4)), pad(seg(5)), seg(6)], axis=1)
    return w_pack.astype(BF16), seg(7).astype(BF16)


def _layer(x, pos, past, ssm0, lw, tm_proj, time_major):
    nb, t_len, _ = x.shape
    m = nb * t_len
    x2d = x.reshape(m, D_MODEL)

    cq, sq = _rope_tables(pos, HEAD_DIM)
    ci, si = _rope_tables(pos, IDX_DIM)
    tabs = (cq, sq, ci, si)
    if t_len < tm_proj:
        tabs = tuple(jnp.tile(t, (tm_proj // t_len, 1)) for t in tabs)
    tiles_per_seq = max(t_len // tm_proj, 1)

    h, q, k, v, kb, vb, iq, ik, iw, u = _proj_call(
        x2d, lw['g_mix'], lw['w_pack'], tabs, tm_proj, tiles_per_seq, nb, time_major)
    gates = _gates_call(h, lw['w_gl'], tm_proj, 1024)

    if past is None:
        s_keys, s_chunk = t_len, 512
        kb_all = kb.reshape(nb, t_len, KV_WIDTH)
        vb_all = vb.reshape(nb, t_len, KV_WIDTH)
        ikbd = _indexer_key_blocks(ik.reshape(nb, t_len, IDX_DIM), s_chunk)
        n_blk = t_len // CHUNK
        o_a = _dsa_call(q, iq, iw, kb_all, vb_all, ikbd, nb, 0, n_blk, n_blk, s_keys, s_chunk,
                        None, min(TOPK_MAX, t_len // 4))
    else:
        pk, pv, pik = past
        l_keys = pk.shape[1] + t_len
        s_keys = -(-l_keys // LANES) * LANES
        padk = lambda a: jnp.pad(a, ((0, 0), (0, s_keys - l_keys), (0, 0)))
        kb_all = padk(jnp.concatenate([pk.reshape(nb, -1, KV_WIDTH).astype(BF16),
                                       kb.reshape(nb, t_len, KV_WIDTH)], axis=1))
        vb_all = padk(jnp.concatenate([pv.reshape(nb, -1, KV_WIDTH).astype(BF16),
                                       vb.reshape(nb, t_len, KV_WIDTH)], axis=1))
        ik_all = padk(jnp.concatenate([pik, ik.reshape(nb, t_len, IDX_DIM)], axis=1))
        ikbd = _indexer_key_blocks(ik_all, s_keys)
        o_a = _dsa_call(q, iq, iw, kb_all, vb_all, ikbd, nb, 0, 1, 1, s_keys, s_keys,
                        l_keys, min(TOPK_MAX, l_keys // 4))

    if time_major:
        u_tm = u.reshape(t_len * nb, S5_WIDTH)
    else:
        u_tm = jnp.transpose(u.reshape(nb, t_len, S5_WIDTH), (1, 0, 2)).reshape(t_len * nb, S5_WIDTH)
    x0_re, x0_im = ssm0
    o_b, s_re, s_im = _s5_call(
        u_tm, x0_re.reshape(nb, S5_LANES), x0_im.reshape(nb, S5_LANES), lw['ab_re'], lw['ab_im'],
        lw['b_re_bd'], lw['b_im_bd'], lw['c_re_bd'], lw['c_im_bd'], lw['d_skip'], lw['w_glu'], lw['b_glu'],
        nb, min(t_len, 64))
    if time_major:
        o_b = o_b.reshape(t_len, nb * S5_WIDTH)
    else:
        o_b = jnp.transpose(o_b.reshape(t_len, nb, S5_WIDTH), (1, 0, 2)).reshape(m, S5_WIDTH)

    tm_merge = 256
    x1, hf = _merge_call(o_a, o_b, gates, x2d, lw['w_proj_a'], lw['w_proj_b'], lw['w_out'], lw['g_ffn'],
                         tm_merge, max(t_len // tm_merge, 1), time_major)
    caches = (k.reshape(nb, t_len, N_KV_HEADS, HEAD_DIM), v.reshape(nb, t_len, N_KV_HEADS, HEAD_DIM),
              ik.reshape(nb, t_len, IDX_DIM), s_re.reshape(nb, S5_GROUPS, S5_STATE),
              s_im.reshape(nb, S5_GROUPS, S5_STATE))
    return x1, hf, caches


def kernel(x_prompt, x_sample, cache_k, cache_v, cache_idx_k, state_ssm_re, state_ssm_im,
           g_mix, w_in, a_re, a_im, log_dt, b_re, b_im, c_re, c_im, d_skip, w_glu, b_glu,
           w_proj_a, w_proj_b, w_out, g_ffn, w_gate, w_up, w_down, g_final):
    depth = w_in.shape[0]
    t_p, t_s = x_prompt.shape[1], x_sample.shape[1]
    past_len = cache_k.shape[2]
    pos_p = jnp.arange(t_p, dtype=F32)
    pos_s = past_len + jnp.arange(t_s, dtype=F32)
    nb_p, nb_s = x_prompt.shape[0], x_sample.shape[0]
    g_fin = g_final.reshape(1, D_MODEL)

    hp, hs = x_prompt, x_sample
    outs_p, outs_s = [], []
    for l in range(depth):
        w_pack, w_gl = _pack_w_in(w_in[l])
        ab_re, ab_im, bb_re, bb_im = _s5prep_call(a_re[l], a_im[l], log_dt[l], b_re[l], b_im[l])
        per_group = lambda bb: jnp.transpose(bb.reshape(S5_CH, S5_GROUPS, S5_STATE), (1, 0, 2))
        lw = {
            'g_mix': g_mix[l].reshape(1, D_MODEL), 'w_pack': w_pack, 'w_gl': w_gl,
            'ab_re': ab_re.reshape(1, S5_LANES), 'ab_im': ab_im.reshape(1, S5_LANES),
            'b_re_bd': _block_diag_slabs(per_group(bb_re)).astype(BF16),
            'b_im_bd': _block_diag_slabs(per_group(bb_im)).astype(BF16),
            'c_re_bd': _block_diag_slabs(jnp.transpose(c_re[l], (0, 2, 1))).astype(BF16),
            'c_im_bd': _block_diag_slabs(jnp.transpose(c_im[l], (0, 2, 1))).astype(BF16),
            'd_skip': d_skip[l].reshape(1, S5_WIDTH), 'w_glu': w_glu[l].astype(BF16),
            'b_glu': b_glu[l].reshape(1, S5_WIDTH),
            'w_proj_a': w_proj_a[l].astype(BF16), 'w_proj_b': w_proj_b[l].astype(BF16),
            'w_out': w_out[l].astype(BF16), 'g_ffn': g_ffn[l].reshape(1, D_MODEL),
        }
        wg, wu, wd = w_gate[l].astype(BF16), w_up[l].astype(BF16), w_down[l].astype(BF16)
        last = l == depth - 1

        zeros = jnp.zeros((nb_p, S5_GROUPS, S5_STATE), F32)
        x1p, hfp, cp = _layer(hp, pos_p, None, (zeros, zeros), lw, 512, True)
        x1s, hfs, cs = _layer(hs, pos_s, (cache_k[l], cache_v[l], cache_idx_k[l]),
                              (state_ssm_re[l], state_ssm_im[l]), lw, 512, False)
        yp = _ffn_call(hfp, x1p, wg, wu, wd, g_fin, 512, 512, last)
        ys = _ffn_call(hfs, x1s, wg, wu, wd, g_fin, 512, 512, last)
        hp = yp.reshape(nb_p, t_p, D_MODEL)
        hs = ys.reshape(nb_s, t_s, D_MODEL)
        outs_p.append(cp)
        outs_s.append(cs)

    stack = lambda outs, i: jnp.stack([o[i] for o in outs])
    return (hp, hs,
            stack(outs_p, 0), stack(outs_p, 1), stack(outs_p, 2), stack(outs_p, 3), stack(outs_p, 4),
            stack(outs_s, 0), stack(outs_s, 1), stack(outs_s, 2), stack(outs_s, 3), stack(outs_s, 4))
```

```python
import functools
import math

import jax
import jax.numpy as jnp
import numpy as np
from jax import lax
from jax.experimental import pallas as pl
from jax.experimental.pallas import tpu as pltpu

F32 = jnp.float32
BF16 = jnp.bfloat16

D_MODEL = 2048
CHUNK = 64
HEAD_DIM = 128
N_HEADS = 8
N_KV_HEADS = 2
GROUP = N_HEADS // N_KV_HEADS
ATT_WIDTH = N_HEADS * HEAD_DIM
KV_WIDTH = N_KV_HEADS * HEAD_DIM
IDX_HEADS = 16
IDX_DIM = 64
IDX_WIDTH = IDX_HEADS * IDX_DIM
TOPK_MAX = 256
S5_CH = 16
S5_WIDTH = D_MODEL // 2
S5_GROUPS = S5_WIDTH // S5_CH
S5_STATE = 64
S5_LANES = S5_GROUPS * S5_STATE
D_FF = 5632
ROPE_THETA = 10000.0
EPS = 1e-6
IN_SIZES = (ATT_WIDTH, KV_WIDTH, KV_WIDTH, IDX_WIDTH, IDX_DIM, IDX_HEADS, S5_WIDTH, 2 * D_MODEL)
IN_OFFS = tuple(int(s) for s in np.cumsum((0,) + IN_SIZES))

LANES = 128
S5_SLAB_GROUPS = LANES // S5_CH
S5_SLABS = S5_GROUPS // S5_SLAB_GROUPS
S5_SLAB_STATE = S5_SLAB_GROUPS * S5_STATE
INT_MIN = -2 ** 31
DSA_GROUP_BLOCKS = 4

VMEM_LIMIT = 56 * 2 ** 20


def _cparams(*sem):
    return pltpu.CompilerParams(dimension_semantics=sem, vmem_limit_bytes=VMEM_LIMIT)


def _const_spec(shape):
    nd = len(shape)
    return pl.BlockSpec(shape, lambda *_: (0,) * nd, pipeline_mode=pl.Buffered(1))


_PQ = 0
_PK = _PQ + ATT_WIDTH
_PV = _PK + KV_WIDTH
_PIQ = _PV + KV_WIDTH
_PIK = _PIQ + IDX_WIDTH
_PIW = _PIK + LANES
_PU = _PIW + LANES
_PEND = _PU + S5_WIDTH
IW_SCALE = (IDX_DIM ** -0.5) * (IDX_HEADS ** -0.5)


def _rope128(z, cos, sin):
    return z * cos + pltpu.roll(z, HEAD_DIM // 2, 1) * sin


def _rope64(z, cos, sin, low_half):
    partner = jnp.where(low_half, pltpu.roll(z, LANES - IDX_DIM // 2, 1), pltpu.roll(z, IDX_DIM // 2, 1))
    return z * cos + partner * sin


def _proj_body(x_ref, g_ref, w_ref, cq_ref, sq_ref, ci_ref, si_ref,
               h_ref, q_ref, k_ref, v_ref, kb_ref, vb_ref, iq_ref, ik_ref, iw_ref, u_ref):
    x = x_ref[...]
    h = (x * lax.rsqrt(jnp.mean(x * x, axis=-1, keepdims=True) + EPS)) * g_ref[...]
    hb = h.astype(BF16)
    h_ref[...] = hb

    def proj(lo, hi):
        return jnp.dot(hb, w_ref[:, lo:hi], preferred_element_type=F32)

    cq, sq, ci, si = cq_ref[...], sq_ref[...], ci_ref[...], si_ref[...]
    low_half = (lax.broadcasted_iota(jnp.int32, cq.shape, 1) & (IDX_DIM - 1)) < (IDX_DIM // 2)

    zq = proj(_PQ, _PK)
    for hd in range(N_HEADS):
        sl = slice(hd * HEAD_DIM, (hd + 1) * HEAD_DIM)
        q_ref[:, sl] = _rope128(zq[:, sl], cq, sq).astype(BF16)
    zk = proj(_PK, _PV)
    for hd in range(N_KV_HEADS):
        sl = slice(hd * HEAD_DIM, (hd + 1) * HEAD_DIM)
        r = _rope128(zk[:, sl], cq, sq)
        k_ref[:, sl] = r
        kb_ref[:, sl] = r.astype(BF16)
    zv = proj(_PV, _PIQ)
    v_ref[...] = zv
    vb_ref[...] = zv.astype(BF16)
    ziq = proj(_PIQ, _PIK)
    for p in range(IDX_WIDTH // LANES):
        sl = slice(p * LANES, (p + 1) * LANES)
        iq_ref[:, sl] = _rope64(ziq[:, sl], ci, si, low_half).astype(BF16)
    z2 = proj(_PIK, _PU)
    ik_ref[...] = _rope64(z2[:, :LANES], ci, si, low_half)[:, :IDX_DIM]
    iw_ref[...] = z2[:, LANES:] * IW_SCALE
    u_ref[...] = proj(_PU, _PEND)


def _proj_call(x2d, g_mix, w_pack, tabs, tm, tiles_per_seq, nb, u_time_major):
    m = x2d.shape[0]
    n_tiles = m // tm
    tab_tiles = tabs[0].shape[0] // tm
    row = lambda w: pl.BlockSpec((tm, w), lambda i: (i, 0))
    tab_spec = pl.BlockSpec((tm, LANES), lambda i: (i % tab_tiles, 0))
    if u_time_major:
        t_len = tiles_per_seq * tm
        u_shape = jax.ShapeDtypeStruct((t_len, nb * S5_WIDTH), F32)
        u_spec = pl.BlockSpec((tm, S5_WIDTH), lambda i: (i % tiles_per_seq, i // tiles_per_seq))
    else:
        u_shape = jax.ShapeDtypeStruct((m, S5_WIDTH), F32)
        u_spec = row(S5_WIDTH)
    out_shape = (
        jax.ShapeDtypeStruct((m, D_MODEL), BF16),
        jax.ShapeDtypeStruct((m, ATT_WIDTH), BF16),
        jax.ShapeDtypeStruct((m, KV_WIDTH), F32),
        jax.ShapeDtypeStruct((m, KV_WIDTH), F32),
        jax.ShapeDtypeStruct((m, KV_WIDTH), BF16),
        jax.ShapeDtypeStruct((m, KV_WIDTH), BF16),
        jax.ShapeDtypeStruct((m, IDX_WIDTH), BF16),
        jax.ShapeDtypeStruct((m, IDX_DIM), F32),
        jax.ShapeDtypeStruct((m, LANES), F32),
        u_shape,
    )
    out_specs = (row(D_MODEL), row(ATT_WIDTH), row(KV_WIDTH), row(KV_WIDTH), row(KV_WIDTH), row(KV_WIDTH),
                 row(IDX_WIDTH), row(IDX_DIM), row(LANES), u_spec)
    return pl.pallas_call(
        _proj_body,
        grid=(n_tiles,),
        in_specs=[row(D_MODEL), _const_spec((1, D_MODEL)), _const_spec(w_pack.shape),
                  tab_spec, tab_spec, tab_spec, tab_spec],
        out_specs=out_specs,
        out_shape=out_shape,
        compiler_params=_cparams("parallel"),
        name="proj",
    )(x2d, g_mix, w_pack, *tabs)


def _gates_body(h_ref, w_ref, o_ref):
    o_ref[...] = jax.nn.sigmoid(jnp.dot(h_ref[...], w_ref[...], preferred_element_type=F32))


def _gates_call(h, w_gl, tm, tn):
    m, n = h.shape[0], w_gl.shape[1]
    return pl.pallas_call(
        _gates_body,
        grid=(m // tm, n // tn),
        in_specs=[pl.BlockSpec((tm, D_MODEL), lambda i, j: (i, 0)),
                  pl.BlockSpec((D_MODEL, tn), lambda i, j: (0, j))],
        out_specs=pl.BlockSpec((tm, tn), lambda i, j: (i, j)),
        out_shape=jax.ShapeDtypeStruct((m, n), F32),
        compiler_params=_cparams("parallel", "arbitrary"),
        name="gates",
    )(h, w_gl)


NEG_INF_KEY = INT_MIN + 0x7FFFFF
F32_MAX = float(np.finfo(np.float32).max)


def _key_to_f32(key):
    return lax.bitcast_convert_type(key ^ ((key >> 31) & jnp.int32(0x7FFFFFFF)), F32)


def _row_count(mask):
    return jnp.sum(jnp.where(mask, 1.0, 0.0), axis=1, keepdims=True)


def _resolve_threshold_ties(score, ge, excess, thr, kf, s_keys, bias_ref, cand_ref, pick_ref, rem_ref):
    tied_row = excess > 0.0
    above = score >= _key_to_f32(thr + 1)
    cand_ref[...] = jnp.where(ge & jnp.logical_not(above) & tied_row, 1.0, 0.0)
    pick_ref[...] = jnp.zeros_like(score)
    rem0 = jnp.where(tied_row, kf - _row_count(above), 0.0)
    rem_ref[...] = jnp.broadcast_to(rem0, rem_ref.shape)
    col = lax.broadcasted_iota(jnp.int32, score.shape, 1)

    def take_next_value(_):
        cand = cand_ref[...] > 0.5
        rem = rem_ref[:, :1]
        top = jnp.max(jnp.where(cand, score, -jnp.inf), axis=1, keepdims=True)
        eq = cand & (score == top)
        last = jnp.zeros((CHUNK, 1), jnp.int32)
        for b in range(int(s_keys).bit_length() - 1, -1, -1):
            nxt = last + jnp.int32(1 << b)
            last = jnp.where(_row_count(eq & (col < nxt)) < rem, nxt, last)
        take = eq & (col <= last) & (rem > 0.0)
        pick_ref[...] = jnp.where(take, 1.0, pick_ref[...])
        cand_ref[...] = jnp.where(eq, 0.0, cand_ref[...])
        rem = jnp.where(top > -jnp.inf, rem - _row_count(take), 0.0)
        rem_ref[...] = jnp.broadcast_to(rem, rem_ref.shape)
        return jnp.max(rem) > 0.0

    lax.while_loop(lambda go: go, take_next_value, jnp.max(rem0) > 0.0)
    chosen = above | (pick_ref[...] > 0.5)
    bias_ref[...] = jnp.where(tied_row, jnp.where(chosen, 0.0, -jnp.inf), bias_ref[...])


def _dsa_body(q_ref, iq_ref, iw_ref, kb_ref, vb_ref, ikbd_ref, *rest,
              s_keys, s_chunk, blk0, fixed_valid, topk):
    o_ref, score_ref, bias_ref, cand_ref, pick_ref, rem_ref = rest[-6:]
    if fixed_valid is None:
        valid = (blk0 + pl.program_id(1) + 1) * CHUNK
    else:
        valid = fixed_valid

    iq = iq_ref[...]
    pairs = IDX_WIDTH // LANES
    lhs = jnp.concatenate([iq[:, p * LANES:(p + 1) * LANES] for p in range(pairs)], axis=0)
    iw = iw_ref[...]
    for c in range(s_keys // s_chunk):
        logits = lax.dot_general(lhs, ikbd_ref[c], (((1,), (1,)), ((), ())), preferred_element_type=F32)
        acc = jnp.zeros((CHUNK, s_chunk), F32)
        for p in range(pairs):
            lp = logits[p * CHUNK:(p + 1) * CHUNK]
            acc = acc + jnp.maximum(lp[:, :s_chunk], 0.0) * iw[:, 2 * p:2 * p + 1]
            acc = acc + jnp.maximum(lp[:, s_chunk:], 0.0) * iw[:, 2 * p + 1:2 * p + 2]
        col = c * s_chunk + lax.broadcasted_iota(jnp.int32, acc.shape, 1)
        score_ref[:, c * s_chunk:(c + 1) * s_chunk] = jnp.where(col < valid, acc, -jnp.inf)

    score = score_ref[...]
    if s_keys <= topk:
        bias_ref[...] = jnp.where(score > -jnp.inf, 0.0, -jnp.inf)
    else:
        kf = float(topk)
        thr = jnp.full((CHUNK, 1), INT_MIN, jnp.int32)
        for b in range(31, -1, -1):
            cand = thr + jnp.int32(INT_MIN if b == 31 else 1 << b)
            thr = jnp.where(_row_count(score >= _key_to_f32(cand)) >= kf, cand, thr)
        thr = jnp.maximum(thr, jnp.int32(NEG_INF_KEY))
        ge = score >= jnp.maximum(_key_to_f32(thr), -F32_MAX)
        bias_ref[...] = jnp.where(ge, 0.0, -jnp.inf)
        excess = _row_count(ge) - kf

        @pl.when(jnp.max(excess) > 0.0)
        def _():
            _resolve_threshold_ties(score, ge, excess, thr, kf, s_keys, bias_ref, cand_ref, pick_ref, rem_ref)

    bias = bias_ref[...]
    q = q_ref[...]
    scale = HEAD_DIM ** -0.5
    for c in range(N_KV_HEADS):
        kc = kb_ref[:, c * HEAD_DIM:(c + 1) * HEAD_DIM]
        vc = vb_ref[:, c * HEAD_DIM:(c + 1) * HEAD_DIM]
        qc = jnp.concatenate(
            [q[:, (c * GROUP + g) * HEAD_DIM:(c * GROUP + g + 1) * HEAD_DIM] for g in range(GROUP)], axis=0)
        logits = lax.dot_general(qc, kc, (((1,), (1,)), ((), ())), preferred_element_type=F32) * scale
        ps = []
        for g in range(GROUP):
            lg = logits[g * CHUNK:(g + 1) * CHUNK] + bias
            e = jnp.exp(lg - jnp.max(lg, axis=1, keepdims=True))
            ps.append((e * (1.0 / jnp.sum(e, axis=1, keepdims=True))).astype(BF16))
        oc = jnp.dot(jnp.concatenate(ps, axis=0), vc, preferred_element_type=F32)
        for g in range(GROUP):
            hd = c * GROUP + g
            o_ref[:, hd * HEAD_DIM:(hd + 1) * HEAD_DIM] = oc[g * CHUNK:(g + 1) * CHUNK].astype(BF16)


def _dsa_call(q, iq, iw, kb, vb, ikbd, o_prev, nb, blk0, n_blk, blks_per_seq, s_keys, s_chunk, fixed_valid, topk):
    qrow = lambda w: pl.BlockSpec((CHUNK, w), lambda n, j: (n * blks_per_seq + blk0 + j, 0))
    n_chunks = s_keys // s_chunk
    body = functools.partial(_dsa_body, s_keys=s_keys, s_chunk=s_chunk, blk0=blk0,
                             fixed_valid=fixed_valid, topk=topk)
    args = [q, iq, iw, kb, vb, ikbd]
    in_specs = [qrow(ATT_WIDTH), qrow(IDX_WIDTH), qrow(LANES),
                pl.BlockSpec((None, s_keys, KV_WIDTH), lambda n, j: (n, 0, 0)),
                pl.BlockSpec((None, s_keys, KV_WIDTH), lambda n, j: (n, 0, 0)),
                pl.BlockSpec((None, n_chunks, 2 * s_chunk, LANES), lambda n, j: (n, 0, 0, 0))]
    aliases = {}
    if o_prev is not None:
        aliases = {len(args): 0}
        args.append(o_prev)
        in_specs.append(pl.BlockSpec(memory_space=pl.ANY))
    mask_buf = pltpu.VMEM((CHUNK, s_keys), F32)
    return pl.pallas_call(
        body,
        grid=(nb, n_blk),
        in_specs=in_specs,
        out_specs=qrow(ATT_WIDTH),
        out_shape=jax.ShapeDtypeStruct((nb * blks_per_seq * CHUNK, ATT_WIDTH), BF16),
        scratch_shapes=[mask_buf, mask_buf, mask_buf, mask_buf, pltpu.VMEM((CHUNK, LANES), F32)],
        input_output_aliases=aliases,
        compiler_params=_cparams("parallel", "arbitrary"),
        name="dsa",
    )(*args)


def _indexer_key_blocks(ik, s_chunk):
    nb, s, _ = ik.shape
    ikb = ik.astype(BF16).reshape(nb, s // s_chunk, s_chunk, IDX_DIM)
    z = jnp.zeros_like(ikb)
    return jnp.concatenate([jnp.concatenate([ikb, z], axis=-1), jnp.concatenate([z, ikb], axis=-1)], axis=-2)


def _s5prep_body(are_ref, aim_ref, ldt_ref, bre_ref, bim_ref, abre_ref, abim_ref, bbre_ref, bbim_ref):
    a_re, a_im = are_ref[...], aim_ref[...]
    dt = jnp.exp(ldt_ref[...])
    mag = jnp.exp(dt * a_re)
    ab_re = mag * jnp.cos(dt * a_im)
    ab_im = mag * jnp.sin(dt * a_im)
    den = a_re * a_re + a_im * a_im
    f_re = ((ab_re - 1.0) * a_re + ab_im * a_im) / den
    f_im = (ab_im * a_re - (ab_re - 1.0) * a_im) / den
    abre_ref[...] = ab_re
    abim_ref[...] = ab_im
    for c in range(S5_CH):
        b_re, b_im = bre_ref[c], bim_ref[c]
        bbre_ref[c] = f_re * b_re - f_im * b_im
        bbim_ref[c] = f_re * b_im + f_im * b_re


def _s5prep_call(a_re, a_im, log_dt, b_re, b_im):
    rows = S5_LANES // LANES
    flat = lambda a: a.reshape(rows, LANES)
    ldt = jnp.broadcast_to(log_dt[:, None], (S5_GROUPS, S5_STATE))
    chan_major = lambda b: jnp.transpose(b, (2, 0, 1)).reshape(S5_CH, rows, LANES)
    small = jax.ShapeDtypeStruct((rows, LANES), F32)
    big = jax.ShapeDtypeStruct((S5_CH, rows, LANES), F32)
    return pl.pallas_call(
        _s5prep_body,
        out_shape=(small, small, big, big),
        name="s5prep",
    )(flat(a_re), flat(a_im), flat(ldt), chan_major(b_re), chan_major(b_im))


def _gelu_tanh(x):
    return 0.5 * x * (1.0 + jnp.tanh(math.sqrt(2.0 / math.pi) * (x + 0.044715 * (x * x * x))))


def _s5_body(u_ref, x0re_ref, x0im_ref, abre_ref, abim_ref, bre_ref, bim_ref, cre_ref, cim_ref,
             dskip_ref, wglu_ref, bglu_ref, ob_ref, sre_ref, sim_ref, xre, xim, st_re, st_im, *, tc, nb, lane_w):
    step = pl.program_id(0)

    @pl.when(step == 0)
    def _():
        st_re[...] = x0re_ref[...]
        st_im[...] = x0im_ref[...]

    u = u_ref[...]
    ub = u.astype(BF16)
    for k in range(S5_SLABS):
        us = ub[:, k * LANES:(k + 1) * LANES]
        sl = slice(k * S5_SLAB_STATE, (k + 1) * S5_SLAB_STATE)
        xre[:, :, sl] = jnp.dot(us, bre_ref[k], preferred_element_type=F32).reshape(tc, nb, S5_SLAB_STATE)
        xim[:, :, sl] = jnp.dot(us, bim_ref[k], preferred_element_type=F32).reshape(tc, nb, S5_SLAB_STATE)

    for lc in range(S5_LANES // lane_w):
        sl = slice(lc * lane_w, (lc + 1) * lane_w)
        a_r = jnp.broadcast_to(abre_ref[:, sl], (nb, lane_w))
        a_i = jnp.broadcast_to(abim_ref[:, sl], (nb, lane_w))

        def scan_step(t, carry, sl=sl, a_r=a_r, a_i=a_i):
            s_r, s_i = carry
            n_r = a_r * s_r - a_i * s_i + xre[t, :, sl]
            n_i = a_r * s_i + a_i * s_r + xim[t, :, sl]
            xre[t, :, sl] = n_r
            xim[t, :, sl] = n_i
            return n_r, n_i

        s_r, s_i = lax.fori_loop(0, tc, scan_step, (st_re[:, sl], st_im[:, sl]), unroll=8)
        st_re[:, sl] = s_r
        st_im[:, sl] = s_i

    ys = []
    for k in range(S5_SLABS):
        sl = slice(k * S5_SLAB_STATE, (k + 1) * S5_SLAB_STATE)
        xr = xre[:, :, sl].reshape(tc * nb, S5_SLAB_STATE).astype(BF16)
        xi = xim[:, :, sl].reshape(tc * nb, S5_SLAB_STATE).astype(BF16)
        ys.append(jnp.dot(xr, cre_ref[k], preferred_element_type=F32)
                  - jnp.dot(xi, cim_ref[k], preferred_element_type=F32))
    y = jnp.concatenate(ys, axis=1) + dskip_ref[...] * u
    yb = _gelu_tanh(y)
    gate = jax.nn.sigmoid(jnp.dot(yb.astype(BF16), wglu_ref[...], preferred_element_type=F32) + bglu_ref[...])
    ob_ref[...] = (yb * gate).astype(BF16)

    @pl.when(step == pl.num_programs(0) - 1)
    def _():
        sre_ref[...] = st_re[...]
        sim_ref[...] = st_im[...]


def _s5_call(u_tm, x0_re, x0_im, ab_re, ab_im, b_re_bd, b_im_bd, c_re_bd, c_im_bd, d_skip, w_glu, b_glu, nb, tc):
    rows = u_tm.shape[0]
    t_len = rows // nb
    lane_w = 1024 if nb <= 8 else 512
    body = functools.partial(_s5_body, tc=tc, nb=nb, lane_w=lane_w)
    state = jax.ShapeDtypeStruct((nb, S5_LANES), F32)
    return pl.pallas_call(
        body,
        grid=(t_len // tc,),
        in_specs=[pl.BlockSpec((tc * nb, S5_WIDTH), lambda i: (i, 0)),
                  _const_spec((nb, S5_LANES)), _const_spec((nb, S5_LANES)),
                  _const_spec((1, S5_LANES)), _const_spec((1, S5_LANES)),
                  _const_spec(b_re_bd.shape), _const_spec(b_im_bd.shape),
                  _const_spec(c_re_bd.shape), _const_spec(c_im_bd.shape),
                  _const_spec((1, S5_WIDTH)), _const_spec((S5_WIDTH, S5_WIDTH)), _const_spec((1, S5_WIDTH))],
        out_specs=(pl.BlockSpec((tc * nb, S5_WIDTH), lambda i: (i, 0)),
                   _const_spec((nb, S5_LANES)), _const_spec((nb, S5_LANES))),
        out_shape=(jax.ShapeDtypeStruct((rows, S5_WIDTH), BF16), state, state),
        scratch_shapes=[pltpu.VMEM((tc, nb, S5_LANES), F32), pltpu.VMEM((tc, nb, S5_LANES), F32),
                        pltpu.VMEM((nb, S5_LANES), F32), pltpu.VMEM((nb, S5_LANES), F32)],
        compiler_params=_cparams("arbitrary"),
        name="s5",
    )(u_tm, x0_re, x0_im, ab_re, ab_im, b_re_bd, b_im_bd, c_re_bd, c_im_bd, d_skip, w_glu, b_glu)


def _block_diag_slabs(w):
    g, r, c = w.shape
    w = w.reshape(S5_SLABS, S5_SLAB_GROUPS, r, c)
    eye = jnp.eye(S5_SLAB_GROUPS, dtype=w.dtype)
    bd = w[:, :, :, None, :] * eye[None, :, None, :, None]
    return bd.reshape(S5_SLABS, S5_SLAB_GROUPS * r, S5_SLAB_GROUPS * c)


def _merge_body(oa_ref, ob_ref, ga_ref, gb_ref, x_ref, wa_ref, wb_ref, wo_ref, g_ref, x1_ref, hf_ref):
    pa = jnp.dot(oa_ref[...], wa_ref[...], preferred_element_type=F32)
    pb = jnp.dot(ob_ref[...], wb_ref[...], preferred_element_type=F32)
    merged = ga_ref[...] * pa + gb_ref[...] * pb
    x1 = x_ref[...] + jnp.dot(merged.astype(BF16), wo_ref[...], preferred_element_type=F32)
    x1_ref[...] = x1
    hf = (x1 * lax.rsqrt(jnp.mean(x1 * x1, axis=-1, keepdims=True) + EPS)) * g_ref[...]
    hf_ref[...] = hf.astype(BF16)


def _merge_call(oa, ob, gates, x2d, w_a, w_b, w_o, g_ffn, tm, tiles_per_seq, ob_time_major):
    m = x2d.shape[0]
    row = lambda w: pl.BlockSpec((tm, w), lambda i: (i, 0))
    if ob_time_major:
        ob_spec = pl.BlockSpec((tm, S5_WIDTH), lambda i: (i % tiles_per_seq, i // tiles_per_seq))
    else:
        ob_spec = row(S5_WIDTH)
    return pl.pallas_call(
        _merge_body,
        grid=(m // tm,),
        in_specs=[row(ATT_WIDTH), ob_spec,
                  pl.BlockSpec((tm, D_MODEL), lambda i: (i, 0)), pl.BlockSpec((tm, D_MODEL), lambda i: (i, 1)),
                  row(D_MODEL), _const_spec(w_a.shape), _const_spec(w_b.shape), _const_spec(w_o.shape),
                  _const_spec((1, D_MODEL))],
        out_specs=(row(D_MODEL), row(D_MODEL)),
        out_shape=(jax.ShapeDtypeStruct((m, D_MODEL), F32), jax.ShapeDtypeStruct((m, D_MODEL), BF16)),
        compiler_params=_cparams("parallel"),
        name="merge",
    )(oa, ob, gates, gates, x2d, w_a, w_b, w_o, g_ffn)


def _ffn_body(hf_ref, x1_ref, wg_ref, wu_ref, wd_ref, g_ref, y_ref, acc_ref, *, final_norm):
    f = pl.program_id(1)
    hf = hf_ref[...]
    a = jax.nn.silu(jnp.dot(hf, wg_ref[...], preferred_element_type=F32)) * jnp.dot(
        hf, wu_ref[...], preferred_element_type=F32)
    part = jnp.dot(a.astype(BF16), wd_ref[...], preferred_element_type=F32)

    @pl.when(f == 0)
    def _():
        acc_ref[...] = part

    @pl.when(f > 0)
    def _():
        acc_ref[...] += part

    @pl.when(f == pl.num_programs(1) - 1)
    def _():
        x2 = x1_ref[...] + acc_ref[...]
        if final_norm:
            x2 = (x2 * lax.rsqrt(jnp.mean(x2 * x2, axis=-1, keepdims=True) + EPS)) * g_ref[...]
        y_ref[...] = x2


def _ffn_call(hf, x1, w_gate, w_up, w_down, g_final, tm, tf, final_norm):
    m = hf.shape[0]
    return pl.pallas_call(
        functools.partial(_ffn_body, final_norm=final_norm),
        grid=(m // tm, D_FF // tf),
        in_specs=[pl.BlockSpec((tm, D_MODEL), lambda i, f: (i, 0)),
                  pl.BlockSpec((tm, D_MODEL), lambda i, f: (i, 0)),
                  pl.BlockSpec((D_MODEL, tf), lambda i, f: (0, f)),
                  pl.BlockSpec((D_MODEL, tf), lambda i, f: (0, f)),
                  pl.BlockSpec((tf, D_MODEL), lambda i, f: (f, 0)),
                  _const_spec((1, D_MODEL))],
        out_specs=pl.BlockSpec((tm, D_MODEL), lambda i, f: (i, 0)),
        out_shape=jax.ShapeDtypeStruct((m, D_MODEL), F32),
        scratch_shapes=[pltpu.VMEM((tm, D_MODEL), F32)],
        compiler_params=_cparams("parallel", "arbitrary"),
        name="ffn",
    )(hf, x1, w_gate, w_up, w_down, g_final)


def _rope_tables(pos, dim):
    half = dim // 2
    inv = 1.0 / (ROPE_THETA ** (jnp.arange(half, dtype=F32) * (2.0 / dim)))
    ang = pos[:, None] * inv[None, :]
    cos, sin = jnp.cos(ang), jnp.sin(ang)
    reps = LANES // dim
    return (jnp.tile(jnp.concatenate([cos, cos], axis=-1), (1, reps)),
            jnp.tile(jnp.concatenate([-sin, sin], axis=-1), (1, reps)))


def _pack_w_in(w_in):
    o = IN_OFFS
    seg = lambda i: w_in[:, o[i]:o[i + 1]]
    pad = lambda w: jnp.pad(w, ((0, 0), (0, LANES - w.shape[1])))
    w_pack = jnp.concatenate([seg(0), seg(1), seg(2), seg(3), pad(seg(4)), pad(seg(5)), seg(6)], axis=1)
    return w_pack.astype(BF16), seg(7).astype(BF16)


def _layer(x, pos, past, ssm0, lw, tm_proj, time_major):
    nb, t_len, _ = x.shape
    m = nb * t_len
    x2d = x.reshape(m, D_MODEL)

    cq, sq = _rope_tables(pos, HEAD_DIM)
    ci, si = _rope_tables(pos, IDX_DIM)
    tabs = (cq, sq, ci, si)
    if t_len < tm_proj:
        tabs = tuple(jnp.tile(t, (tm_proj // t_len, 1)) for t in tabs)
    tiles_per_seq = max(t_len // tm_proj, 1)

    h, q, k, v, kb, vb, iq, ik, iw, u = _proj_call(
        x2d, lw['g_mix'], lw['w_pack'], tabs, tm_proj, tiles_per_seq, nb, time_major)
    gates = _gates_call(h, lw['w_gl'], tm_proj, 1024)

    if past is None:
        s_chunk = DSA_GROUP_BLOCKS * CHUNK
        kb_all = kb.reshape(nb, t_len, KV_WIDTH)
        vb_all = vb.reshape(nb, t_len, KV_WIDTH)
        ikbd = _indexer_key_blocks(ik.reshape(nb, t_len, IDX_DIM), s_chunk)
        n_blk = t_len // CHUNK
        o_a = None
        for blk0 in range(0, n_blk, DSA_GROUP_BLOCKS):
            o_a = _dsa_call(q, iq, iw, kb_all, vb_all, ikbd, o_a, nb, blk0, DSA_GROUP_BLOCKS, n_blk,
                            (blk0 + DSA_GROUP_BLOCKS) * CHUNK, s_chunk, None, min(TOPK_MAX, t_len // 4))
    else:
        pk, pv, pik = past
        l_keys = pk.shape[1] + t_len
        s_keys = -(-l_keys // LANES) * LANES
        padk = lambda a: jnp.pad(a, ((0, 0), (0, s_keys - l_keys), (0, 0)))
        kb_all = padk(jnp.concatenate([pk.reshape(nb, -1, KV_WIDTH).astype(BF16),
                                       kb.reshape(nb, t_len, KV_WIDTH)], axis=1))
        vb_all = padk(jnp.concatenate([pv.reshape(nb, -1, KV_WIDTH).astype(BF16),
                                       vb.reshape(nb, t_len, KV_WIDTH)], axis=1))
        ik_all = padk(jnp.concatenate([pik, ik.reshape(nb, t_len, IDX_DIM)], axis=1))
        ikbd = _indexer_key_blocks(ik_all, s_keys)
        o_a = _dsa_call(q, iq, iw, kb_all, vb_all, ikbd, None, nb, 0, 1, 1, s_keys, s_keys,
                        l_keys, min(TOPK_MAX, l_keys // 4))

    if time_major:
        u_tm = u.reshape(t_len * nb, S5_WIDTH)
    else:
        u_tm = jnp.transpose(u.reshape(nb, t_len, S5_WIDTH), (1, 0, 2)).reshape(t_len * nb, S5_WIDTH)
    x0_re, x0_im = ssm0
    o_b, s_re, s_im = _s5_call(
        u_tm, x0_re.reshape(nb, S5_LANES), x0_im.reshape(nb, S5_LANES), lw['ab_re'], lw['ab_im'],
        lw['b_re_bd'], lw['b_im_bd'], lw['c_re_bd'], lw['c_im_bd'], lw['d_skip'], lw['w_glu'], lw['b_glu'],
        nb, min(t_len, 64))
    if time_major:
        o_b = o_b.reshape(t_len, nb * S5_WIDTH)
    else:
        o_b = jnp.transpose(o_b.reshape(t_len, nb, S5_WIDTH), (1, 0, 2)).reshape(m, S5_WIDTH)

    tm_merge = 256
    x1, hf = _merge_call(o_a, o_b, gates, x2d, lw['w_proj_a'], lw['w_proj_b'], lw['w_out'], lw['g_ffn'],
                         tm_merge, max(t_len // tm_merge, 1), time_major)
    caches = (k.reshape(nb, t_len, N_KV_HEADS, HEAD_DIM), v.reshape(nb, t_len, N_KV_HEADS, HEAD_DIM),
              ik.reshape(nb, t_len, IDX_DIM), s_re.reshape(nb, S5_GROUPS, S5_STATE),
              s_im.reshape(nb, S5_GROUPS, S5_STATE))
    return x1, hf, caches


def kernel(x_prompt, x_sample, cache_k, cache_v, cache_idx_k, state_ssm_re, state_ssm_im,
           g_mix, w_in, a_re, a_im, log_dt, b_re, b_im, c_re, c_im, d_skip, w_glu, b_glu,
           w_proj_a, w_proj_b, w_out, g_ffn, w_gate, w_up, w_down, g_final):
    depth = w_in.shape[0]
    t_p, t_s = x_prompt.shape[1], x_sample.shape[1]
    past_len = cache_k.shape[2]
    pos_p = jnp.arange(t_p, dtype=F32)
    pos_s = past_len + jnp.arange(t_s, dtype=F32)
    nb_p, nb_s = x_prompt.shape[0], x_sample.shape[0]
    g_fin = g_final.reshape(1, D_MODEL)

    hp, hs = x_prompt, x_sample
    outs_p, outs_s = [], []
    for l in range(depth):
        w_pack, w_gl = _pack_w_in(w_in[l])
        ab_re, ab_im, bb_re, bb_im = _s5prep_call(a_re[l], a_im[l], log_dt[l], b_re[l], b_im[l])
        per_group = lambda bb: jnp.transpose(bb.reshape(S5_CH, S5_GROUPS, S5_STATE), (1, 0, 2))
        lw = {
            'g_mix': g_mix[l].reshape(1, D_MODEL), 'w_pack': w_pack, 'w_gl': w_gl,
            'ab_re': ab_re.reshape(1, S5_LANES), 'ab_im': ab_im.reshape(1, S5_LANES),
            'b_re_bd': _block_diag_slabs(per_group(bb_re)).astype(BF16),
            'b_im_bd': _block_diag_slabs(per_group(bb_im)).astype(BF16),
            'c_re_bd': _block_diag_slabs(jnp.transpose(c_re[l], (0, 2, 1))).astype(BF16),
            'c_im_bd': _block_diag_slabs(jnp.transpose(c_im[l], (0, 2, 1))).astype(BF16),
            'd_skip': d_skip[l].reshape(1, S5_WIDTH), 'w_glu': w_glu[l].astype(BF16),
            'b_glu': b_glu[l].reshape(1, S5_WIDTH),
            'w_proj_a': w_proj_a[l].astype(BF16), 'w_proj_b': w_proj_b[l].astype(BF16),
            'w_out': w_out[l].astype(BF16), 'g_ffn': g_ffn[l].reshape(1, D_MODEL),
        }
        wg, wu, wd = w_gate[l].astype(BF16), w_up[l].astype(BF16), w_down[l].astype(BF16)
        last = l == depth - 1

        zeros = jnp.zeros((nb_p, S5_GROUPS, S5_STATE), F32)
        x1p, hfp, cp = _layer(hp, pos_p, None, (zeros, zeros), lw, 512, True)
        x1s, hfs, cs = _layer(hs, pos_s, (cache_k[l], cache_v[l], cache_idx_k[l]),
                              (state_ssm_re[l], state_ssm_im[l]), lw, 512, False)
        yp = _ffn_call(hfp, x1p, wg, wu, wd, g_fin, 512, 512, last)
        ys = _ffn_call(hfs, x1s, wg, wu, wd, g_fin, 512, 512, last)
        hp = yp.reshape(nb_p, t_p, D_MODEL)
        hs = ys.reshape(nb_s, t_s, D_MODEL)
        outs_p.append(cp)
        outs_s.append(cs)

    stack = lambda outs, i: jnp.stack([o[i] for o in outs])
    return (hp, hs,
            stack(outs_p, 0), stack(outs_p, 1), stack(outs_p, 2), stack(outs_p, 3), stack(outs_p, 4),
            stack(outs_s, 0), stack(outs_s, 1), stack(outs_s, 2), stack(outs_s, 3), stack(outs_s, 4))
```

```python
import functools
import math

import jax
import jax.numpy as jnp
import numpy as np
from jax import lax
from jax.experimental import pallas as pl
from jax.experimental.pallas import tpu as pltpu

F32 = jnp.float32
BF16 = jnp.bfloat16

D_MODEL = 2048
CHUNK = 64
HEAD_DIM = 128
N_HEADS = 8
N_KV_HEADS = 2
GROUP = N_HEADS // N_KV_HEADS
ATT_WIDTH = N_HEADS * HEAD_DIM
KV_WIDTH = N_KV_HEADS * HEAD_DIM
IDX_HEADS = 16
IDX_DIM = 64
IDX_WIDTH = IDX_HEADS * IDX_DIM
TOPK_MAX = 256
S5_CH = 16
S5_WIDTH = D_MODEL // 2
S5_GROUPS = S5_WIDTH // S5_CH
S5_STATE = 64
S5_LANES = S5_GROUPS * S5_STATE
D_FF = 5632
ROPE_THETA = 10000.0
EPS = 1e-6
IN_SIZES = (ATT_WIDTH, KV_WIDTH, KV_WIDTH, IDX_WIDTH, IDX_DIM, IDX_HEADS, S5_WIDTH, 2 * D_MODEL)
IN_OFFS = tuple(int(s) for s in np.cumsum((0,) + IN_SIZES))

LANES = 128
S5_SLAB_GROUPS = LANES // S5_CH
S5_SLABS = S5_GROUPS // S5_SLAB_GROUPS
S5_SLAB_STATE = S5_SLAB_GROUPS * S5_STATE
INT_MIN = -2 ** 31
DSA_GROUP_BLOCKS = 4
QROWS = 2 * CHUNK
SEARCH_PART = 128

VMEM_LIMIT = 56 * 2 ** 20


def _cparams(*sem):
    return pltpu.CompilerParams(dimension_semantics=sem, vmem_limit_bytes=VMEM_LIMIT)


def _const_spec(shape):
    nd = len(shape)
    return pl.BlockSpec(shape, lambda *_: (0,) * nd, pipeline_mode=pl.Buffered(1))


_PQ = 0
_PK = _PQ + ATT_WIDTH
_PV = _PK + KV_WIDTH
_PIQ = _PV + KV_WIDTH
_PIK = _PIQ + IDX_WIDTH
_PIW = _PIK + LANES
_PU = _PIW + LANES
_PEND = _PU + S5_WIDTH
IW_SCALE = (IDX_DIM ** -0.5) * (IDX_HEADS ** -0.5)


def _rope128(z, cos, sin):
    return z * cos + pltpu.roll(z, HEAD_DIM // 2, 1) * sin


def _rope64(z, cos, sin, low_half):
    partner = jnp.where(low_half, pltpu.roll(z, LANES - IDX_DIM // 2, 1), pltpu.roll(z, IDX_DIM // 2, 1))
    return z * cos + partner * sin


def _proj_body(x_ref, g_ref, w_ref, cq_ref, sq_ref, ci_ref, si_ref,
               h_ref, q_ref, k_ref, v_ref, kb_ref, vb_ref, iq_ref, ik_ref, iw_ref, u_ref):
    x = x_ref[...]
    h = (x * lax.rsqrt(jnp.mean(x * x, axis=-1, keepdims=True) + EPS)) * g_ref[...]
    hb = h.astype(BF16)
    h_ref[...] = hb

    def proj(lo, hi):
        return jnp.dot(hb, w_ref[:, lo:hi], preferred_element_type=F32)

    cq, sq, ci, si = cq_ref[...], sq_ref[...], ci_ref[...], si_ref[...]
    low_half = (lax.broadcasted_iota(jnp.int32, cq.shape, 1) & (IDX_DIM - 1)) < (IDX_DIM // 2)

    zq = proj(_PQ, _PK)
    for hd in range(N_HEADS):
        sl = slice(hd * HEAD_DIM, (hd + 1) * HEAD_DIM)
        q_ref[:, sl] = _rope128(zq[:, sl], cq, sq).astype(BF16)
    zk = proj(_PK, _PV)
    for hd in range(N_KV_HEADS):
        sl = slice(hd * HEAD_DIM, (hd + 1) * HEAD_DIM)
        r = _rope128(zk[:, sl], cq, sq)
        k_ref[:, sl] = r
        kb_ref[:, sl] = r.astype(BF16)
    zv = proj(_PV, _PIQ)
    v_ref[...] = zv
    vb_ref[...] = zv.astype(BF16)
    ziq = proj(_PIQ, _PIK)
    for p in range(IDX_WIDTH // LANES):
        sl = slice(p * LANES, (p + 1) * LANES)
        iq_ref[:, sl] = _rope64(ziq[:, sl], ci, si, low_half).astype(BF16)
    z2 = proj(_PIK, _PU)
    ik_ref[...] = _rope64(z2[:, :LANES], ci, si, low_half)[:, :IDX_DIM]
    iw_ref[...] = z2[:, LANES:] * IW_SCALE
    u_ref[...] = proj(_PU, _PEND)


def _proj_call(x2d, g_mix, w_pack, tabs, tm, tiles_per_seq, nb, u_time_major):
    m = x2d.shape[0]
    n_tiles = m // tm
    tab_tiles = tabs[0].shape[0] // tm
    row = lambda w: pl.BlockSpec((tm, w), lambda i: (i, 0))
    tab_spec = pl.BlockSpec((tm, LANES), lambda i: (i % tab_tiles, 0))
    if u_time_major:
        t_len = tiles_per_seq * tm
        u_shape = jax.ShapeDtypeStruct((t_len, nb * S5_WIDTH), F32)
        u_spec = pl.BlockSpec((tm, S5_WIDTH), lambda i: (i % tiles_per_seq, i // tiles_per_seq))
    else:
        u_shape = jax.ShapeDtypeStruct((m, S5_WIDTH), F32)
        u_spec = row(S5_WIDTH)
    out_shape = (
        jax.ShapeDtypeStruct((m, D_MODEL), BF16),
        jax.ShapeDtypeStruct((m, ATT_WIDTH), BF16),
        jax.ShapeDtypeStruct((m, KV_WIDTH), F32),
        jax.ShapeDtypeStruct((m, KV_WIDTH), F32),
        jax.ShapeDtypeStruct((m, KV_WIDTH), BF16),
        jax.ShapeDtypeStruct((m, KV_WIDTH), BF16),
        jax.ShapeDtypeStruct((m, IDX_WIDTH), BF16),
        jax.ShapeDtypeStruct((m, IDX_DIM), F32),
        jax.ShapeDtypeStruct((m, LANES), F32),
        u_shape,
    )
    out_specs = (row(D_MODEL), row(ATT_WIDTH), row(KV_WIDTH), row(KV_WIDTH), row(KV_WIDTH), row(KV_WIDTH),
                 row(IDX_WIDTH), row(IDX_DIM), row(LANES), u_spec)
    return pl.pallas_call(
        _proj_body,
        grid=(n_tiles,),
        in_specs=[row(D_MODEL), _const_spec((1, D_MODEL)), _const_spec(w_pack.shape),
                  tab_spec, tab_spec, tab_spec, tab_spec],
        out_specs=out_specs,
        out_shape=out_shape,
        compiler_params=_cparams("parallel"),
        name="proj",
    )(x2d, g_mix, w_pack, *tabs)


def _gates_body(h_ref, w_ref, o_ref):
    o_ref[...] = jax.nn.sigmoid(jnp.dot(h_ref[...], w_ref[...], preferred_element_type=F32))


def _gates_call(h, w_gl, tm, tn):
    m, n = h.shape[0], w_gl.shape[1]
    return pl.pallas_call(
        _gates_body,
        grid=(m // tm, n // tn),
        in_specs=[pl.BlockSpec((tm, D_MODEL), lambda i, j: (i, 0)),
                  pl.BlockSpec((D_MODEL, tn), lambda i, j: (0, j))],
        out_specs=pl.BlockSpec((tm, tn), lambda i, j: (i, j)),
        out_shape=jax.ShapeDtypeStruct((m, n), F32),
        compiler_params=_cparams("parallel", "arbitrary"),
        name="gates",
    )(h, w_gl)


NEG_INF_KEY = INT_MIN + 0x7FFFFF
F32_MAX = float(np.finfo(np.float32).max)


def _key_to_f32(key):
    return lax.bitcast_convert_type(key ^ ((key >> 31) & jnp.int32(0x7FFFFFFF)), F32)


def _row_count(mask):
    return jnp.sum(jnp.where(mask, 1.0, 0.0), axis=1, keepdims=True)


def _resolve_threshold_ties(score, t_lo, t_next, excess, kf, s_keys, bias_ref, cand_ref, pick_ref, rem_ref):
    tied_row = excess > 0.0
    ge = score >= t_lo
    above = score >= t_next
    cand_ref[...] = jnp.where(ge & jnp.logical_not(above) & tied_row, 1.0, 0.0)
    pick_ref[...] = jnp.zeros_like(score)
    rem0 = jnp.where(tied_row, kf - _row_count(above), 0.0)
    rem_ref[...] = jnp.broadcast_to(rem0, rem_ref.shape)
    col = lax.broadcasted_iota(jnp.int32, score.shape, 1)

    def take_next_value(_):
        cand = cand_ref[...] > 0.5
        rem = rem_ref[:, :1]
        top = jnp.max(jnp.where(cand, score, -jnp.inf), axis=1, keepdims=True)
        eq = cand & (score == top)
        last = jnp.zeros((score.shape[0], 1), jnp.int32)
        for b in range(int(s_keys).bit_length() - 1, -1, -1):
            nxt = last + jnp.int32(1 << b)
            last = jnp.where(_row_count(eq & (col < nxt)) < rem, nxt, last)
        take = eq & (col <= last) & (rem > 0.0)
        pick_ref[...] = jnp.where(take, 1.0, pick_ref[...])
        cand_ref[...] = jnp.where(eq, 0.0, cand_ref[...])
        rem = jnp.where(top > -jnp.inf, rem - _row_count(take), 0.0)
        rem_ref[...] = jnp.broadcast_to(rem, rem_ref.shape)
        return jnp.max(rem) > 0.0

    lax.while_loop(lambda go: go, take_next_value, jnp.max(rem0) > 0.0)
    chosen = above | (pick_ref[...] > 0.5)
    bias_ref[...] = jnp.where(tied_row, jnp.where(chosen, 0.0, -jnp.inf), bias_ref[...])


def _dsa_body(q_ref, iq_ref, iw_ref, kb_ref, vb_ref, ikbd_ref, *rest,
              s_keys, s_chunk, blk0, fixed_valid, topk):
    o_ref, score_ref, bias_ref, cand_ref, pick_ref, rem_ref = rest[-6:]
    n_keysets = kb_ref.shape[0]
    halves = [(slice(hf * CHUNK, (hf + 1) * CHUNK), hf % n_keysets) for hf in range(QROWS // CHUNK)]

    pairs = IDX_WIDTH // LANES
    for hf, (rows, ks) in enumerate(halves):
        if fixed_valid is None:
            valid = (blk0 + len(halves) * pl.program_id(1) + hf + 1) * CHUNK
        else:
            valid = fixed_valid
        iq = iq_ref[rows, :]
        lhs = jnp.concatenate([iq[:, p * LANES:(p + 1) * LANES] for p in range(pairs)], axis=0)
        iw = iw_ref[rows, :]
        for c in range(s_keys // s_chunk):
            logits = lax.dot_general(lhs, ikbd_ref[ks, c], (((1,), (1,)), ((), ())), preferred_element_type=F32)
            acc = jnp.zeros((CHUNK, s_chunk), F32)
            for p in range(pairs):
                lp = logits[p * CHUNK:(p + 1) * CHUNK]
                acc = acc + jnp.maximum(lp[:, :s_chunk], 0.0) * iw[:, 2 * p:2 * p + 1]
                acc = acc + jnp.maximum(lp[:, s_chunk:], 0.0) * iw[:, 2 * p + 1:2 * p + 2]
            col = c * s_chunk + lax.broadcasted_iota(jnp.int32, acc.shape, 1)
            score_ref[rows, c * s_chunk:(c + 1) * s_chunk] = jnp.where(col < valid, acc, -jnp.inf)

    score = score_ref[...]
    if s_keys <= topk:
        bias_ref[...] = jnp.where(score > -jnp.inf, 0.0, -jnp.inf)
    else:
        kf = float(topk)
        score_t = score.T

        def count_ge(t):
            ind = jnp.where(score_t >= t, 1.0, 0.0).reshape(s_keys // SEARCH_PART, SEARCH_PART, QROWS)
            return jnp.sum(jnp.sum(ind, axis=0), axis=0, keepdims=True)

        thr = jnp.full((1, QROWS), INT_MIN, jnp.int32)
        for b in range(31, -1, -1):
            cand = thr + jnp.int32(INT_MIN if b == 31 else 1 << b)
            thr = jnp.where(count_ge(_key_to_f32(cand)) >= kf, cand, thr)
        thr = jnp.maximum(thr, jnp.int32(NEG_INF_KEY))
        t_lo = jnp.maximum(_key_to_f32(thr), -F32_MAX)
        excess = count_ge(t_lo) - kf
        stats = jnp.concatenate([t_lo, _key_to_f32(thr + 1), excess, jnp.zeros((QROWS - 3, QROWS), F32)], axis=0).T
        t_lo_col = stats[:, 0:1]
        bias_ref[...] = jnp.where(score >= t_lo_col, 0.0, -jnp.inf)

        @pl.when(jnp.max(excess) > 0.0)
        def _():
            _resolve_threshold_ties(score, t_lo_col, stats[:, 1:2], stats[:, 2:3], kf, s_keys,
                                    bias_ref, cand_ref, pick_ref, rem_ref)

    scale = HEAD_DIM ** -0.5
    for rows, ks in halves:
        bias = bias_ref[rows, :]
        q = q_ref[rows, :]
        for c in range(N_KV_HEADS):
            kc = kb_ref[ks, :, c * HEAD_DIM:(c + 1) * HEAD_DIM]
            vc = vb_ref[ks, :, c * HEAD_DIM:(c + 1) * HEAD_DIM]
            qc = jnp.concatenate(
                [q[:, (c * GROUP + g) * HEAD_DIM:(c * GROUP + g + 1) * HEAD_DIM] for g in range(GROUP)], axis=0)
            logits = lax.dot_general(qc, kc, (((1,), (1,)), ((), ())), preferred_element_type=F32) * scale
            ps = []
            for g in range(GROUP):
                lg = logits[g * CHUNK:(g + 1) * CHUNK] + bias
                e = jnp.exp(lg - jnp.max(lg, axis=1, keepdims=True))
                ps.append((e * (1.0 / jnp.sum(e, axis=1, keepdims=True))).astype(BF16))
            oc = jnp.dot(jnp.concatenate(ps, axis=0), vc, preferred_element_type=F32)
            for g in range(GROUP):
                hd = c * GROUP + g
                o_ref[rows, hd * HEAD_DIM:(hd + 1) * HEAD_DIM] = oc[g * CHUNK:(g + 1) * CHUNK].astype(BF16)


def _dsa_call(q, iq, iw, kb, vb, ikbd, o_prev, keysets, blk0, n_blk, blks_per_seq, s_keys, s_chunk,
              fixed_valid, topk):
    nb = kb.shape[0]
    blocks_per_step = QROWS // CHUNK
    steps_per_outer = blks_per_seq * keysets // blocks_per_step
    qrow = lambda w: pl.BlockSpec(
        (QROWS, w), lambda n, j: (n * steps_per_outer + blk0 // blocks_per_step + j, 0))
    n_chunks = s_keys // s_chunk
    body = functools.partial(_dsa_body, s_keys=s_keys, s_chunk=s_chunk, blk0=blk0,
                             fixed_valid=fixed_valid, topk=topk)
    args = [q, iq, iw, kb, vb, ikbd]
    in_specs = [qrow(ATT_WIDTH), qrow(IDX_WIDTH), qrow(LANES),
                pl.BlockSpec((keysets, s_keys, KV_WIDTH), lambda n, j: (n, 0, 0)),
                pl.BlockSpec((keysets, s_keys, KV_WIDTH), lambda n, j: (n, 0, 0)),
                pl.BlockSpec((keysets, n_chunks, 2 * s_chunk, LANES), lambda n, j: (n, 0, 0, 0))]
    aliases = {}
    if o_prev is not None:
        aliases = {len(args): 0}
        args.append(o_prev)
        in_specs.append(pl.BlockSpec(memory_space=pl.ANY))
    mask_buf = pltpu.VMEM((QROWS, s_keys), F32)
    return pl.pallas_call(
        body,
        grid=(nb // keysets, n_blk * keysets // blocks_per_step),
        in_specs=in_specs,
        out_specs=qrow(ATT_WIDTH),
        out_shape=jax.ShapeDtypeStruct((nb * blks_per_seq * CHUNK, ATT_WIDTH), BF16),
        scratch_shapes=[mask_buf, mask_buf, mask_buf, mask_buf, pltpu.VMEM((QROWS, LANES), F32)],
        input_output_aliases=aliases,
        compiler_params=_cparams("parallel", "arbitrary"),
        name="dsa",
    )(*args)


def _indexer_key_blocks(ik, s_chunk):
    nb, s, _ = ik.shape
    ikb = ik.astype(BF16).reshape(nb, s // s_chunk, s_chunk, IDX_DIM)
    z = jnp.zeros_like(ikb)
    return jnp.concatenate([jnp.concatenate([ikb, z], axis=-1), jnp.concatenate([z, ikb], axis=-1)], axis=-2)


def _s5prep_body(are_ref, aim_ref, ldt_ref, bre_ref, bim_ref, abre_ref, abim_ref, bbre_ref, bbim_ref):
    a_re, a_im = are_ref[...], aim_ref[...]
    dt = jnp.exp(ldt_ref[...])
    mag = jnp.exp(dt * a_re)
    ab_re = mag * jnp.cos(dt * a_im)
    ab_im = mag * jnp.sin(dt * a_im)
    den = a_re * a_re + a_im * a_im
    f_re = ((ab_re - 1.0) * a_re + ab_im * a_im) / den
    f_im = (ab_im * a_re - (ab_re - 1.0) * a_im) / den
    abre_ref[...] = ab_re
    abim_ref[...] = ab_im
    for c in range(S5_CH):
        b_re, b_im = bre_ref[c], bim_ref[c]
        bbre_ref[c] = f_re * b_re - f_im * b_im
        bbim_ref[c] = f_re * b_im + f_im * b_re


def _s5prep_call(a_re, a_im, log_dt, b_re, b_im):
    rows = S5_LANES // LANES
    flat = lambda a: a.reshape(rows, LANES)
    ldt = jnp.broadcast_to(log_dt[:, None], (S5_GROUPS, S5_STATE))
    chan_major = lambda b: jnp.transpose(b, (2, 0, 1)).reshape(S5_CH, rows, LANES)
    small = jax.ShapeDtypeStruct((rows, LANES), F32)
    big = jax.ShapeDtypeStruct((S5_CH, rows, LANES), F32)
    return pl.pallas_call(
        _s5prep_body,
        out_shape=(small, small, big, big),
        name="s5prep",
    )(flat(a_re), flat(a_im), flat(ldt), chan_major(b_re), chan_major(b_im))


def _gelu_tanh(x):
    return 0.5 * x * (1.0 + jnp.tanh(math.sqrt(2.0 / math.pi) * (x + 0.044715 * (x * x * x))))


def _s5_body(u_ref, x0re_ref, x0im_ref, abre_ref, abim_ref, bre_ref, bim_ref, cre_ref, cim_ref,
             dskip_ref, wglu_ref, bglu_ref, ob_ref, sre_ref, sim_ref, xre, xim, st_re, st_im, *, tc, nb, lane_w):
    step = pl.program_id(0)

    @pl.when(step == 0)
    def _():
        st_re[...] = x0re_ref[...]
        st_im[...] = x0im_ref[...]

    u = u_ref[...]
    ub = u.astype(BF16)
    for k in range(S5_SLABS):
        us = ub[:, k * LANES:(k + 1) * LANES]
        sl = slice(k * S5_SLAB_STATE, (k + 1) * S5_SLAB_STATE)
        xre[:, :, sl] = jnp.dot(us, bre_ref[k], preferred_element_type=F32).reshape(tc, nb, S5_SLAB_STATE)
        xim[:, :, sl] = jnp.dot(us, bim_ref[k], preferred_element_type=F32).reshape(tc, nb, S5_SLAB_STATE)

    for lc in range(S5_LANES // lane_w):
        sl = slice(lc * lane_w, (lc + 1) * lane_w)
        a_r = jnp.broadcast_to(abre_ref[:, sl], (nb, lane_w))
        a_i = jnp.broadcast_to(abim_ref[:, sl], (nb, lane_w))

        def scan_step(t, carry, sl=sl, a_r=a_r, a_i=a_i):
            s_r, s_i = carry
            n_r = a_r * s_r - a_i * s_i + xre[t, :, sl]
            n_i = a_r * s_i + a_i * s_r + xim[t, :, sl]
            xre[t, :, sl] = n_r
            xim[t, :, sl] = n_i
            return n_r, n_i

        s_r, s_i = lax.fori_loop(0, tc, scan_step, (st_re[:, sl], st_im[:, sl]), unroll=8)
        st_re[:, sl] = s_r
        st_im[:, sl] = s_i

    ys = []
    for k in range(S5_SLABS):
        sl = slice(k * S5_SLAB_STATE, (k + 1) * S5_SLAB_STATE)
        xr = xre[:, :, sl].reshape(tc * nb, S5_SLAB_STATE).astype(BF16)
        xi = xim[:, :, sl].reshape(tc * nb, S5_SLAB_STATE).astype(BF16)
        ys.append(jnp.dot(xr, cre_ref[k], preferred_element_type=F32)
                  - jnp.dot(xi, cim_ref[k], preferred_element_type=F32))
    y = jnp.concatenate(ys, axis=1) + dskip_ref[...] * u
    yb = _gelu_tanh(y)
    gate = jax.nn.sigmoid(jnp.dot(yb.astype(BF16), wglu_ref[...], preferred_element_type=F32) + bglu_ref[...])
    ob_ref[...] = (yb * gate).astype(BF16)

    @pl.when(step == pl.num_programs(0) - 1)
    def _():
        sre_ref[...] = st_re[...]
        sim_ref[...] = st_im[...]


def _s5_call(u_tm, x0_re, x0_im, ab_re, ab_im, b_re_bd, b_im_bd, c_re_bd, c_im_bd, d_skip, w_glu, b_glu, nb, tc):
    rows = u_tm.shape[0]
    t_len = rows // nb
    lane_w = 1024 if nb <= 8 else 512
    body = functools.partial(_s5_body, tc=tc, nb=nb, lane_w=lane_w)
    state = jax.ShapeDtypeStruct((nb, S5_LANES), F32)
    return pl.pallas_call(
        body,
        grid=(t_len // tc,),
        in_specs=[pl.BlockSpec((tc * nb, S5_WIDTH), lambda i: (i, 0)),
                  _const_spec((nb, S5_LANES)), _const_spec((nb, S5_LANES)),
                  _const_spec((1, S5_LANES)), _const_spec((1, S5_LANES)),
                  _const_spec(b_re_bd.shape), _const_spec(b_im_bd.shape),
                  _const_spec(c_re_bd.shape), _const_spec(c_im_bd.shape),
                  _const_spec((1, S5_WIDTH)), _const_spec((S5_WIDTH, S5_WIDTH)), _const_spec((1, S5_WIDTH))],
        out_specs=(pl.BlockSpec((tc * nb, S5_WIDTH), lambda i: (i, 0)),
                   _const_spec((nb, S5_LANES)), _const_spec((nb, S5_LANES))),
        out_shape=(jax.ShapeDtypeStruct((rows, S5_WIDTH), BF16), state, state),
        scratch_shapes=[pltpu.VMEM((tc, nb, S5_LANES), F32), pltpu.VMEM((tc, nb, S5_LANES), F32),
                        pltpu.VMEM((nb, S5_LANES), F32), pltpu.VMEM((nb, S5_LANES), F32)],
        compiler_params=_cparams("arbitrary"),
        name="s5",
    )(u_tm, x0_re, x0_im, ab_re, ab_im, b_re_bd, b_im_bd, c_re_bd, c_im_bd, d_skip, w_glu, b_glu)


def _block_diag_slabs(w):
    g, r, c = w.shape
    w = w.reshape(S5_SLABS, S5_SLAB_GROUPS, r, c)
    eye = jnp.eye(S5_SLAB_GROUPS, dtype=w.dtype)
    bd = w[:, :, :, None, :] * eye[None, :, None, :, None]
    return bd.reshape(S5_SLABS, S5_SLAB_GROUPS * r, S5_SLAB_GROUPS * c)


def _merge_body(oa_ref, ob_ref, ga_ref, gb_ref, x_ref, wa_ref, wb_ref, wo_ref, g_ref, x1_ref, hf_ref):
    pa = jnp.dot(oa_ref[...], wa_ref[...], preferred_element_type=F32)
    pb = jnp.dot(ob_ref[...], wb_ref[...], preferred_element_type=F32)
    merged = ga_ref[...] * pa + gb_ref[...] * pb
    x1 = x_ref[...] + jnp.dot(merged.astype(BF16), wo_ref[...], preferred_element_type=F32)
    x1_ref[...] = x1
    hf = (x1 * lax.rsqrt(jnp.mean(x1 * x1, axis=-1, keepdims=True) + EPS)) * g_ref[...]
    hf_ref[...] = hf.astype(BF16)


def _merge_call(oa, ob, gates, x2d, w_a, w_b, w_o, g_ffn, tm, tiles_per_seq, ob_time_major):
    m = x2d.shape[0]
    row = lambda w: pl.BlockSpec((tm, w), lambda i: (i, 0))
    if ob_time_major:
        ob_spec = pl.BlockSpec((tm, S5_WIDTH), lambda i: (i % tiles_per_seq, i // tiles_per_seq))
    else:
        ob_spec = row(S5_WIDTH)
    return pl.pallas_call(
        _merge_body,
        grid=(m // tm,),
        in_specs=[row(ATT_WIDTH), ob_spec,
                  pl.BlockSpec((tm, D_MODEL), lambda i: (i, 0)), pl.BlockSpec((tm, D_MODEL), lambda i: (i, 1)),
                  row(D_MODEL), _const_spec(w_a.shape), _const_spec(w_b.shape), _const_spec(w_o.shape),
                  _const_spec((1, D_MODEL))],
        out_specs=(row(D_MODEL), row(D_MODEL)),
        out_shape=(jax.ShapeDtypeStruct((m, D_MODEL), F32), jax.ShapeDtypeStruct((m, D_MODEL), BF16)),
        compiler_params=_cparams("parallel"),
        name="merge",
    )(oa, ob, gates, gates, x2d, w_a, w_b, w_o, g_ffn)


def _ffn_body(hf_ref, x1_ref, wg_ref, wu_ref, wd_ref, g_ref, y_ref, acc_ref, *, final_norm):
    f = pl.program_id(1)
    hf = hf_ref[...]
    a = jax.nn.silu(jnp.dot(hf, wg_ref[...], preferred_element_type=F32)) * jnp.dot(
        hf, wu_ref[...], preferred_element_type=F32)
    part = jnp.dot(a.astype(BF16), wd_ref[...], preferred_element_type=F32)

    @pl.when(f == 0)
    def _():
        acc_ref[...] = part

    @pl.when(f > 0)
    def _():
        acc_ref[...] += part

    @pl.when(f == pl.num_programs(1) - 1)
    def _():
        x2 = x1_ref[...] + acc_ref[...]
        if final_norm:
            x2 = (x2 * lax.rsqrt(jnp.mean(x2 * x2, axis=-1, keepdims=True) + EPS)) * g_ref[...]
        y_ref[...] = x2


def _ffn_call(hf, x1, w_gate, w_up, w_down, g_final, tm, tf, final_norm):
    m = hf.shape[0]
    return pl.pallas_call(
        functools.partial(_ffn_body, final_norm=final_norm),
        grid=(m // tm, D_FF // tf),
        in_specs=[pl.BlockSpec((tm, D_MODEL), lambda i, f: (i, 0)),
                  pl.BlockSpec((tm, D_MODEL), lambda i, f: (i, 0)),
                  pl.BlockSpec((D_MODEL, tf), lambda i, f: (0, f)),
                  pl.BlockSpec((D_MODEL, tf), lambda i, f: (0, f)),
                  pl.BlockSpec((tf, D_MODEL), lambda i, f: (f, 0)),
                  _const_spec((1, D_MODEL))],
        out_specs=pl.BlockSpec((tm, D_MODEL), lambda i, f: (i, 0)),
        out_shape=jax.ShapeDtypeStruct((m, D_MODEL), F32),
        scratch_shapes=[pltpu.VMEM((tm, D_MODEL), F32)],
        compiler_params=_cparams("parallel", "arbitrary"),
        name="ffn",
    )(hf, x1, w_gate, w_up, w_down, g_final)


def _rope_tables(pos, dim):
    half = dim // 2
    inv = 1.0 / (ROPE_THETA ** (jnp.arange(half, dtype=F32) * (2.0 / dim)))
    ang = pos[:, None] * inv[None, :]
    cos, sin = jnp.cos(ang), jnp.sin(ang)
    reps = LANES // dim
    return (jnp.tile(jnp.concatenate([cos, cos], axis=-1), (1, reps)),
            jnp.tile(jnp.concatenate([-sin, sin], axis=-1), (1, reps)))


def _pack_w_in(w_in):
    o = IN_OFFS
    seg = lambda i: w_in[:, o[i]:o[i + 1]]
    pad = lambda w: jnp.pad(w, ((0, 0), (0, LANES - w.shape[1])))
    w_pack = jnp.concatenate([seg(0), seg(1), seg(2), seg(3), pad(seg(4)), pad(seg(5)), seg(6)], axis=1)
    return w_pack.astype(BF16), seg(7).astype(BF16)


def _layer(x, pos, past, ssm0, lw, tm_proj, time_major):
    nb, t_len, _ = x.shape
    m = nb * t_len
    x2d = x.reshape(m, D_MODEL)

    cq, sq = _rope_tables(pos, HEAD_DIM)
    ci, si = _rope_tables(pos, IDX_DIM)
    tabs = (cq, sq, ci, si)
    if t_len < tm_proj:
        tabs = tuple(jnp.tile(t, (tm_proj // t_len, 1)) for t in tabs)
    tiles_per_seq = max(t_len // tm_proj, 1)

    h, q, k, v, kb, vb, iq, ik, iw, u = _proj_call(
        x2d, lw['g_mix'], lw['w_pack'], tabs, tm_proj, tiles_per_seq, nb, time_major)
    gates = _gates_call(h, lw['w_gl'], tm_proj, 1024)

    if past is None:
        s_chunk = DSA_GROUP_BLOCKS * CHUNK
        kb_all = kb.reshape(nb, t_len, KV_WIDTH)
        vb_all = vb.reshape(nb, t_len, KV_WIDTH)
        ikbd = _indexer_key_blocks(ik.reshape(nb, t_len, IDX_DIM), s_chunk)
        n_blk = t_len // CHUNK
        o_a = None
        for blk0 in range(0, n_blk, DSA_GROUP_BLOCKS):
            o_a = _dsa_call(q, iq, iw, kb_all, vb_all, ikbd, o_a, 1, blk0, DSA_GROUP_BLOCKS, n_blk,
                            (blk0 + DSA_GROUP_BLOCKS) * CHUNK, s_chunk, None, min(TOPK_MAX, t_len // 4))
    else:
        pk, pv, pik = past
        l_keys = pk.shape[1] + t_len
        s_keys = -(-l_keys // LANES) * LANES
        padk = lambda a: jnp.pad(a, ((0, 0), (0, s_keys - l_keys), (0, 0)))
        kb_all = padk(jnp.concatenate([pk.reshape(nb, -1, KV_WIDTH).astype(BF16),
                                       kb.reshape(nb, t_len, KV_WIDTH)], axis=1))
        vb_all = padk(jnp.concatenate([pv.reshape(nb, -1, KV_WIDTH).astype(BF16),
                                       vb.reshape(nb, t_len, KV_WIDTH)], axis=1))
        ik_all = padk(jnp.concatenate([pik, ik.reshape(nb, t_len, IDX_DIM)], axis=1))
        ikbd = _indexer_key_blocks(ik_all, s_keys)
        o_a = _dsa_call(q, iq, iw, kb_all, vb_all, ikbd, None, 2, 0, 1, 1, s_keys, s_keys,
                        l_keys, min(TOPK_MAX, l_keys // 4))

    if time_major:
        u_tm = u.reshape(t_len * nb, S5_WIDTH)
    else:
        u_tm = jnp.transpose(u.reshape(nb, t_len, S5_WIDTH), (1, 0, 2)).reshape(t_len * nb, S5_WIDTH)
    x0_re, x0_im = ssm0
    o_b, s_re, s_im = _s5_call(
        u_tm, x0_re.reshape(nb, S5_LANES), x0_im.reshape(nb, S5_LANES), lw['ab_re'], lw['ab_im'],
        lw['b_re_bd'], lw['b_im_bd'], lw['c_re_bd'], lw['c_im_bd'], lw['d_skip'], lw['w_glu'], lw['b_glu'],
        nb, min(t_len, 64))
    if time_major:
        o_b = o_b.reshape(t_len, nb * S5_WIDTH)
    else:
        o_b = jnp.transpose(o_b.reshape(t_len, nb, S5_WIDTH), (1, 0, 2)).reshape(m, S5_WIDTH)

    tm_merge = 256
    x1, hf = _merge_call(o_a, o_b, gates, x2d, lw['w_proj_a'], lw['w_proj_b'], lw['w_out'], lw['g_ffn'],
                         tm_merge, max(t_len // tm_merge, 1), time_major)
    caches = (k.reshape(nb, t_len, N_KV_HEADS, HEAD_DIM), v.reshape(nb, t_len, N_KV_HEADS, HEAD_DIM),
              ik.reshape(nb, t_len, IDX_DIM), s_re.reshape(nb, S5_GROUPS, S5_STATE),
              s_im.reshape(nb, S5_GROUPS, S5_STATE))
    return x1, hf, caches


def kernel(x_prompt, x_sample, cache_k, cache_v, cache_idx_k, state_ssm_re, state_ssm_im,
           g_mix, w_in, a_re, a_im, log_dt, b_re, b_im, c_re, c_im, d_skip, w_glu, b_glu,
           w_proj_a, w_proj_b, w_out, g_ffn, w_gate, w_up, w_down, g_final):
    depth = w_in.shape[0]
    t_p, t_s = x_prompt.shape[1], x_sample.shape[1]
    past_len = cache_k.shape[2]
    pos_p = jnp.arange(t_p, dtype=F32)
    pos_s = past_len + jnp.arange(t_s, dtype=F32)
    nb_p, nb_s = x_prompt.shape[0], x_sample.shape[0]
    g_fin = g_final.reshape(1, D_MODEL)

    hp, hs = x_prompt, x_sample
    outs_p, outs_s = [], []
    for l in range(depth):
        w_pack, w_gl = _pack_w_in(w_in[l])
        ab_re, ab_im, bb_re, bb_im = _s5prep_call(a_re[l], a_im[l], log_dt[l], b_re[l], b_im[l])
        per_group = lambda bb: jnp.transpose(bb.reshape(S5_CH, S5_GROUPS, S5_STATE), (1, 0, 2))
        lw = {
            'g_mix': g_mix[l].reshape(1, D_MODEL), 'w_pack': w_pack, 'w_gl': w_gl,
            'ab_re': ab_re.reshape(1, S5_LANES), 'ab_im': ab_im.reshape(1, S5_LANES),
            'b_re_bd': _block_diag_slabs(per_group(bb_re)).astype(BF16),
            'b_im_bd': _block_diag_slabs(per_group(bb_im)).astype(BF16),
            'c_re_bd': _block_diag_slabs(jnp.transpose(c_re[l], (0, 2, 1))).astype(BF16),
            'c_im_bd': _block_diag_slabs(jnp.transpose(c_im[l], (0, 2, 1))).astype(BF16),
            'd_skip': d_skip[l].reshape(1, S5_WIDTH), 'w_glu': w_glu[l].astype(BF16),
            'b_glu': b_glu[l].reshape(1, S5_WIDTH),
            'w_proj_a': w_proj_a[l].astype(BF16), 'w_proj_b': w_proj_b[l].astype(BF16),
            'w_out': w_out[l].astype(BF16), 'g_ffn': g_ffn[l].reshape(1, D_MODEL),
        }
        wg, wu, wd = w_gate[l].astype(BF16), w_up[l].astype(BF16), w_down[l].astype(BF16)
        last = l == depth - 1

        zeros = jnp.zeros((nb_p, S5_GROUPS, S5_STATE), F32)
        x1p, hfp, cp = _layer(hp, pos_p, None, (zeros, zeros), lw, 512, True)
        x1s, hfs, cs = _layer(hs, pos_s, (cache_k[l], cache_v[l], cache_idx_k[l]),
                              (state_ssm_re[l], state_ssm_im[l]), lw, 512, False)
        yp = _ffn_call(hfp, x1p, wg, wu, wd, g_fin, 512, 512, last)
        ys = _ffn_call(hfs, x1s, wg, wu, wd, g_fin, 512, 512, last)
        hp = yp.reshape(nb_p, t_p, D_MODEL)
        hs = ys.reshape(nb_s, t_s, D_MODEL)
        outs_p.append(cp)
        outs_s.append(cs)

    stack = lambda outs, i: jnp.stack([o[i] for o in outs])
    return (hp, hs,
            stack(outs_p, 0), stack(outs_p, 1), stack(outs_p, 2), stack(outs_p, 3), stack(outs_p, 4),
            stack(outs_s, 0), stack(outs_s, 1), stack(outs_s, 2), stack(outs_s, 3), stack(outs_s, 4))
```

```python
import functools
import math

import jax
import jax.numpy as jnp
import numpy as np
from jax import lax
from jax.experimental import pallas as pl
from jax.experimental.pallas import tpu as pltpu

F32 = jnp.float32
BF16 = jnp.bfloat16

D_MODEL = 2048
CHUNK = 64
HEAD_DIM = 128
N_HEADS = 8
N_KV_HEADS = 2
GROUP = N_HEADS // N_KV_HEADS
ATT_WIDTH = N_HEADS * HEAD_DIM
KV_WIDTH = N_KV_HEADS * HEAD_DIM
IDX_HEADS = 16
IDX_DIM = 64
IDX_WIDTH = IDX_HEADS * IDX_DIM
TOPK_MAX = 256
S5_CH = 16
S5_WIDTH = D_MODEL // 2
S5_GROUPS = S5_WIDTH // S5_CH
S5_STATE = 64
S5_LANES = S5_GROUPS * S5_STATE
D_FF = 5632
ROPE_THETA = 10000.0
EPS = 1e-6
IN_SIZES = (ATT_WIDTH, KV_WIDTH, KV_WIDTH, IDX_WIDTH, IDX_DIM, IDX_HEADS, S5_WIDTH, 2 * D_MODEL)
IN_OFFS = tuple(int(s) for s in np.cumsum((0,) + IN_SIZES))

LANES = 128
S5_SLAB_GROUPS = LANES // S5_CH
S5_SLABS = S5_GROUPS // S5_SLAB_GROUPS
S5_SLAB_STATE = S5_SLAB_GROUPS * S5_STATE
INT_MIN = -2 ** 31
DSA_GROUP_BLOCKS = 4
QROWS = 2 * CHUNK
S5_STEP_ROWS = 512
SEARCH_PART = 128

VMEM_LIMIT = 56 * 2 ** 20


def _cparams(*sem):
    return pltpu.CompilerParams(dimension_semantics=sem, vmem_limit_bytes=VMEM_LIMIT)


def _const_spec(shape):
    nd = len(shape)
    return pl.BlockSpec(shape, lambda *_: (0,) * nd, pipeline_mode=pl.Buffered(1))


_PQ = 0
_PK = _PQ + ATT_WIDTH
_PV = _PK + KV_WIDTH
_PIQ = _PV + KV_WIDTH
_PIK = _PIQ + IDX_WIDTH
_PIW = _PIK + LANES
_PU = _PIW + LANES
_PEND = _PU + S5_WIDTH
IW_SCALE = (IDX_DIM ** -0.5) * (IDX_HEADS ** -0.5)


def _rope128(z, cos, sin):
    return z * cos + pltpu.roll(z, HEAD_DIM // 2, 1) * sin


def _rope64(z, cos, sin, low_half):
    partner = jnp.where(low_half, pltpu.roll(z, LANES - IDX_DIM // 2, 1), pltpu.roll(z, IDX_DIM // 2, 1))
    return z * cos + partner * sin


def _proj_body(x_ref, g_ref, w_ref, cq_ref, sq_ref, ci_ref, si_ref,
               h_ref, q_ref, k_ref, v_ref, kb_ref, vb_ref, iq_ref, ik_ref, iw_ref, u_ref):
    x = x_ref[...]
    h = (x * lax.rsqrt(jnp.mean(x * x, axis=-1, keepdims=True) + EPS)) * g_ref[...]
    hb = h.astype(BF16)
    h_ref[...] = hb

    def proj(lo, hi):
        return jnp.dot(hb, w_ref[:, lo:hi], preferred_element_type=F32)

    cq, sq, ci, si = cq_ref[...], sq_ref[...], ci_ref[...], si_ref[...]
    low_half = (lax.broadcasted_iota(jnp.int32, cq.shape, 1) & (IDX_DIM - 1)) < (IDX_DIM // 2)

    zq = proj(_PQ, _PK)
    for hd in range(N_HEADS):
        sl = slice(hd * HEAD_DIM, (hd + 1) * HEAD_DIM)
        q_ref[:, sl] = _rope128(zq[:, sl], cq, sq).astype(BF16)
    zk = proj(_PK, _PV)
    for hd in range(N_KV_HEADS):
        sl = slice(hd * HEAD_DIM, (hd + 1) * HEAD_DIM)
        r = _rope128(zk[:, sl], cq, sq)
        k_ref[:, sl] = r
        kb_ref[:, sl] = r.astype(BF16)
    zv = proj(_PV, _PIQ)
    v_ref[...] = zv
    vb_ref[...] = zv.astype(BF16)
    ziq = proj(_PIQ, _PIK)
    for p in range(IDX_WIDTH // LANES):
        sl = slice(p * LANES, (p + 1) * LANES)
        iq_ref[:, sl] = _rope64(ziq[:, sl], ci, si, low_half).astype(BF16)
    z2 = proj(_PIK, _PU)
    ik_ref[...] = _rope64(z2[:, :LANES], ci, si, low_half)[:, :IDX_DIM]
    iw_ref[...] = z2[:, LANES:] * IW_SCALE
    u_ref[...] = proj(_PU, _PEND)


def _proj_call(x2d, g_mix, w_pack, tabs, tm):
    m = x2d.shape[0]
    n_tiles = m // tm
    tab_tiles = tabs[0].shape[0] // tm
    row = lambda w: pl.BlockSpec((tm, w), lambda i: (i, 0))
    tab_spec = pl.BlockSpec((tm, LANES), lambda i: (i % tab_tiles, 0))
    out_shape = (
        jax.ShapeDtypeStruct((m, D_MODEL), BF16),
        jax.ShapeDtypeStruct((m, ATT_WIDTH), BF16),
        jax.ShapeDtypeStruct((m, KV_WIDTH), F32),
        jax.ShapeDtypeStruct((m, KV_WIDTH), F32),
        jax.ShapeDtypeStruct((m, KV_WIDTH), BF16),
        jax.ShapeDtypeStruct((m, KV_WIDTH), BF16),
        jax.ShapeDtypeStruct((m, IDX_WIDTH), BF16),
        jax.ShapeDtypeStruct((m, IDX_DIM), F32),
        jax.ShapeDtypeStruct((m, LANES), F32),
        jax.ShapeDtypeStruct((m, S5_WIDTH), F32),
    )
    out_specs = (row(D_MODEL), row(ATT_WIDTH), row(KV_WIDTH), row(KV_WIDTH), row(KV_WIDTH), row(KV_WIDTH),
                 row(IDX_WIDTH), row(IDX_DIM), row(LANES), row(S5_WIDTH))
    return pl.pallas_call(
        _proj_body,
        grid=(n_tiles,),
        in_specs=[row(D_MODEL), _const_spec((1, D_MODEL)), _const_spec(w_pack.shape),
                  tab_spec, tab_spec, tab_spec, tab_spec],
        out_specs=out_specs,
        out_shape=out_shape,
        compiler_params=_cparams("parallel"),
        name="proj",
    )(x2d, g_mix, w_pack, *tabs)


def _gates_body(h_ref, w_ref, o_ref):
    o_ref[...] = jax.nn.sigmoid(jnp.dot(h_ref[...], w_ref[...], preferred_element_type=F32))


def _gates_call(h, w_gl, tm, tn):
    m, n = h.shape[0], w_gl.shape[1]
    return pl.pallas_call(
        _gates_body,
        grid=(m // tm, n // tn),
        in_specs=[pl.BlockSpec((tm, D_MODEL), lambda i, j: (i, 0)),
                  pl.BlockSpec((D_MODEL, tn), lambda i, j: (0, j))],
        out_specs=pl.BlockSpec((tm, tn), lambda i, j: (i, j)),
        out_shape=jax.ShapeDtypeStruct((m, n), F32),
        compiler_params=_cparams("parallel", "arbitrary"),
        name="gates",
    )(h, w_gl)


LOG2_E = math.log2(math.e)
NEG_INF_KEY = INT_MIN + 0x7FFFFF
F32_MAX = float(np.finfo(np.float32).max)


def _key_to_f32(key):
    return lax.bitcast_convert_type(key ^ ((key >> 31) & jnp.int32(0x7FFFFFFF)), F32)


def _row_count(mask):
    return jnp.sum(jnp.where(mask, 1.0, 0.0), axis=1, keepdims=True)


def _resolve_threshold_ties(score, t_lo, t_next, excess, kf, s_keys, bias_ref, cand_ref, pick_ref, rem_ref):
    tied_row = excess > 0.0
    ge = score >= t_lo
    above = score >= t_next
    cand_ref[...] = jnp.where(ge & jnp.logical_not(above) & tied_row, 1.0, 0.0)
    pick_ref[...] = jnp.zeros_like(score)
    rem0 = jnp.where(tied_row, kf - _row_count(above), 0.0)
    rem_ref[...] = jnp.broadcast_to(rem0, rem_ref.shape)
    col = lax.broadcasted_iota(jnp.int32, score.shape, 1)

    def take_next_value(_):
        cand = cand_ref[...] > 0.5
        rem = rem_ref[:, :1]
        top = jnp.max(jnp.where(cand, score, -jnp.inf), axis=1, keepdims=True)
        eq = cand & (score == top)
        last = jnp.zeros((score.shape[0], 1), jnp.int32)
        for b in range(int(s_keys).bit_length() - 1, -1, -1):
            nxt = last + jnp.int32(1 << b)
            last = jnp.where(_row_count(eq & (col < nxt)) < rem, nxt, last)
        take = eq & (col <= last) & (rem > 0.0)
        pick_ref[...] = jnp.where(take, 1.0, pick_ref[...])
        cand_ref[...] = jnp.where(eq, 0.0, cand_ref[...])
        rem = jnp.where(top > -jnp.inf, rem - _row_count(take), 0.0)
        rem_ref[...] = jnp.broadcast_to(rem, rem_ref.shape)
        return jnp.max(rem) > 0.0

    lax.while_loop(lambda go: go, take_next_value, jnp.max(rem0) > 0.0)
    chosen = above | (pick_ref[...] > 0.5)
    bias_ref[...] = jnp.where(tied_row, jnp.where(chosen, 0.0, -jnp.inf), bias_ref[...])


def _dsa_body(q_ref, iq_ref, iw_ref, kb_ref, vb_ref, ikbd_ref, *rest,
              s_keys, s_chunk, blk0, fixed_valid, topk):
    o_ref, score_ref, bias_ref, cand_ref, pick_ref, rem_ref = rest[-6:]
    n_keysets = kb_ref.shape[0]
    halves = [(slice(hf * CHUNK, (hf + 1) * CHUNK), hf % n_keysets) for hf in range(QROWS // CHUNK)]

    pairs = IDX_WIDTH // LANES
    for hf, (rows, ks) in enumerate(halves):
        if fixed_valid is None:
            valid = (blk0 + len(halves) * pl.program_id(1) + hf + 1) * CHUNK
        else:
            valid = fixed_valid
        iq = iq_ref[rows, :]
        lhs = jnp.concatenate([iq[:, p * LANES:(p + 1) * LANES] for p in range(pairs)], axis=0)
        iw = iw_ref[rows, :]
        for c in range(s_keys // s_chunk):
            logits = lax.dot_general(lhs, ikbd_ref[ks, c], (((1,), (1,)), ((), ())), preferred_element_type=F32)
            acc = jnp.zeros((CHUNK, s_chunk), F32)
            for p in range(pairs):
                lp = logits[p * CHUNK:(p + 1) * CHUNK]
                acc = acc + jnp.maximum(lp[:, :s_chunk], 0.0) * iw[:, 2 * p:2 * p + 1]
                acc = acc + jnp.maximum(lp[:, s_chunk:], 0.0) * iw[:, 2 * p + 1:2 * p + 2]
            col = c * s_chunk + lax.broadcasted_iota(jnp.int32, acc.shape, 1)
            score_ref[rows, c * s_chunk:(c + 1) * s_chunk] = jnp.where(col < valid, acc, -jnp.inf)

    score = score_ref[...]
    if s_keys <= topk:
        bias_ref[...] = jnp.where(score > -jnp.inf, 0.0, -jnp.inf)
    else:
        kf = float(topk)
        score_t = score.T

        def count_ge(t):
            ind = jnp.where(score_t >= t, 1.0, 0.0).reshape(s_keys // SEARCH_PART, SEARCH_PART, QROWS)
            return jnp.sum(jnp.sum(ind, axis=0), axis=0, keepdims=True)

        thr = jnp.full((1, QROWS), INT_MIN, jnp.int32)
        for b in range(31, -1, -1):
            cand = thr + jnp.int32(INT_MIN if b == 31 else 1 << b)
            thr = jnp.where(count_ge(_key_to_f32(cand)) >= kf, cand, thr)
        thr = jnp.maximum(thr, jnp.int32(NEG_INF_KEY))
        t_lo = jnp.maximum(_key_to_f32(thr), -F32_MAX)
        excess = count_ge(t_lo) - kf
        stats = jnp.concatenate([t_lo, _key_to_f32(thr + 1), excess, jnp.zeros((QROWS - 3, QROWS), F32)], axis=0).T
        t_lo_col = stats[:, 0:1]
        bias_ref[...] = jnp.where(score >= t_lo_col, 0.0, -jnp.inf)

        @pl.when(jnp.max(excess) > 0.0)
        def _():
            _resolve_threshold_ties(score, t_lo_col, stats[:, 1:2], stats[:, 2:3], kf, s_keys,
                                    bias_ref, cand_ref, pick_ref, rem_ref)

    scale = HEAD_DIM ** -0.5
    for rows, ks in halves:
        bias = bias_ref[rows, :]
        q = q_ref[rows, :]
        for c in range(N_KV_HEADS):
            kc = kb_ref[ks, :, c * HEAD_DIM:(c + 1) * HEAD_DIM]
            vc = vb_ref[ks, :, c * HEAD_DIM:(c + 1) * HEAD_DIM]
            qc = jnp.concatenate(
                [q[:, (c * GROUP + g) * HEAD_DIM:(c * GROUP + g + 1) * HEAD_DIM] for g in range(GROUP)], axis=0)
            logits = lax.dot_general(qc, kc, (((1,), (1,)), ((), ())), preferred_element_type=F32)
            es, inv = [], []
            for g in range(GROUP):
                lg = logits[g * CHUNK:(g + 1) * CHUNK] + bias
                e = jnp.exp2((lg - jnp.max(lg, axis=1, keepdims=True)) * (scale * LOG2_E))
                inv.append(1.0 / jnp.sum(e, axis=1, keepdims=True))
                es.append(e.astype(BF16))
            oc = jnp.dot(jnp.concatenate(es, axis=0), vc, preferred_element_type=F32)
            for g in range(GROUP):
                hd = c * GROUP + g
                o_ref[rows, hd * HEAD_DIM:(hd + 1) * HEAD_DIM] = (
                    oc[g * CHUNK:(g + 1) * CHUNK] * inv[g]).astype(BF16)


def _dsa_call(q, iq, iw, kb, vb, ikbd, o_prev, keysets, blk0, n_blk, blks_per_seq, s_keys, s_chunk,
              fixed_valid, topk):
    nb = kb.shape[0]
    blocks_per_step = QROWS // CHUNK
    steps_per_outer = blks_per_seq * keysets // blocks_per_step
    qrow = lambda w: pl.BlockSpec(
        (QROWS, w), lambda n, j: (n * steps_per_outer + blk0 // blocks_per_step + j, 0))
    n_chunks = s_keys // s_chunk
    body = functools.partial(_dsa_body, s_keys=s_keys, s_chunk=s_chunk, blk0=blk0,
                             fixed_valid=fixed_valid, topk=topk)
    args = [q, iq, iw, kb, vb, ikbd]
    in_specs = [qrow(ATT_WIDTH), qrow(IDX_WIDTH), qrow(LANES),
                pl.BlockSpec((keysets, s_keys, KV_WIDTH), lambda n, j: (n, 0, 0)),
                pl.BlockSpec((keysets, s_keys, KV_WIDTH), lambda n, j: (n, 0, 0)),
                pl.BlockSpec((keysets, n_chunks, 2 * s_chunk, LANES), lambda n, j: (n, 0, 0, 0))]
    aliases = {}
    if o_prev is not None:
        aliases = {len(args): 0}
        args.append(o_prev)
        in_specs.append(pl.BlockSpec(memory_space=pl.ANY))
    mask_buf = pltpu.VMEM((QROWS, s_keys), F32)
    return pl.pallas_call(
        body,
        grid=(nb // keysets, n_blk * keysets // blocks_per_step),
        in_specs=in_specs,
        out_specs=qrow(ATT_WIDTH),
        out_shape=jax.ShapeDtypeStruct((nb * blks_per_seq * CHUNK, ATT_WIDTH), BF16),
        scratch_shapes=[mask_buf, mask_buf, mask_buf, mask_buf, pltpu.VMEM((QROWS, LANES), F32)],
        input_output_aliases=aliases,
        compiler_params=_cparams("parallel", "arbitrary"),
        name="dsa",
    )(*args)


def _indexer_key_blocks(ik, s_chunk):
    nb, s, _ = ik.shape
    ikb = ik.astype(BF16).reshape(nb, s // s_chunk, s_chunk, IDX_DIM)
    z = jnp.zeros_like(ikb)
    return jnp.concatenate([jnp.concatenate([ikb, z], axis=-1), jnp.concatenate([z, ikb], axis=-1)], axis=-2)


def _s5prep_body(are_ref, aim_ref, ldt_ref, bre_ref, bim_ref, abre_ref, abim_ref, bbre_ref, bbim_ref):
    a_re, a_im = are_ref[...], aim_ref[...]
    dt = jnp.exp(ldt_ref[...])
    mag = jnp.exp(dt * a_re)
    ab_re = mag * jnp.cos(dt * a_im)
    ab_im = mag * jnp.sin(dt * a_im)
    den = a_re * a_re + a_im * a_im
    f_re = ((ab_re - 1.0) * a_re + ab_im * a_im) / den
    f_im = (ab_im * a_re - (ab_re - 1.0) * a_im) / den
    abre_ref[...] = ab_re
    abim_ref[...] = ab_im
    for c in range(S5_CH):
        b_re, b_im = bre_ref[c], bim_ref[c]
        bbre_ref[c] = f_re * b_re - f_im * b_im
        bbim_ref[c] = f_re * b_im + f_im * b_re


def _s5prep_call(a_re, a_im, log_dt, b_re, b_im):
    rows = S5_LANES // LANES
    flat = lambda a: a.reshape(rows, LANES)
    ldt = jnp.broadcast_to(log_dt[:, None], (S5_GROUPS, S5_STATE))
    chan_major = lambda b: jnp.transpose(b, (2, 0, 1)).reshape(S5_CH, rows, LANES)
    small = jax.ShapeDtypeStruct((rows, LANES), F32)
    big = jax.ShapeDtypeStruct((S5_CH, rows, LANES), F32)
    return pl.pallas_call(
        _s5prep_body,
        out_shape=(small, small, big, big),
        name="s5prep",
    )(flat(a_re), flat(a_im), flat(ldt), chan_major(b_re), chan_major(b_im))


def _gelu_tanh(x):
    return 0.5 * x * (1.0 + jnp.tanh(math.sqrt(2.0 / math.pi) * (x + 0.044715 * (x * x * x))))


def _s5_body(u_ref, x0re_ref, x0im_ref, abre_ref, abim_ref, bre_ref, bim_ref, cre_ref, cim_ref,
             dskip_ref, wglu_ref, bglu_ref, ob_ref, sre_ref, sim_ref, utm, otm, xre, xim, st_re, st_im,
             *, tc, nb, lane_w):
    step = pl.program_id(0)

    @pl.when(step == 0)
    def _():
        st_re[...] = x0re_ref[...]
        st_im[...] = x0im_ref[...]

    for n in range(nb):
        for k in range(S5_SLABS):
            utm[k, pl.ds(n, tc, stride=nb), :] = u_ref[n, :, k * LANES:(k + 1) * LANES]

    for k in range(S5_SLABS):
        us = utm[k].astype(BF16)
        sl = slice(k * S5_SLAB_STATE, (k + 1) * S5_SLAB_STATE)
        xre[:, :, sl] = jnp.dot(us, bre_ref[k], preferred_element_type=F32).reshape(tc, nb, S5_SLAB_STATE)
        xim[:, :, sl] = jnp.dot(us, bim_ref[k], preferred_element_type=F32).reshape(tc, nb, S5_SLAB_STATE)

    for lc in range(S5_LANES // lane_w):
        sl = slice(lc * lane_w, (lc + 1) * lane_w)
        a_r = jnp.broadcast_to(abre_ref[:, sl], (nb, lane_w))
        a_i = jnp.broadcast_to(abim_ref[:, sl], (nb, lane_w))

        def scan_step(t, carry, sl=sl, a_r=a_r, a_i=a_i):
            s_r, s_i = carry
            n_r = a_r * s_r - a_i * s_i + xre[t, :, sl]
            n_i = a_r * s_i + a_i * s_r + xim[t, :, sl]
            xre[t, :, sl] = n_r
            xim[t, :, sl] = n_i
            return n_r, n_i

        s_r, s_i = lax.fori_loop(0, tc, scan_step, (st_re[:, sl], st_im[:, sl]), unroll=8)
        st_re[:, sl] = s_r
        st_im[:, sl] = s_i

    ys = []
    for k in range(S5_SLABS):
        sl = slice(k * S5_SLAB_STATE, (k + 1) * S5_SLAB_STATE)
        xr = xre[:, :, sl].reshape(tc * nb, S5_SLAB_STATE).astype(BF16)
        xi = xim[:, :, sl].reshape(tc * nb, S5_SLAB_STATE).astype(BF16)
        ys.append(jnp.dot(xr, cre_ref[k], preferred_element_type=F32)
                  - jnp.dot(xi, cim_ref[k], preferred_element_type=F32)
                  + dskip_ref[:, k * LANES:(k + 1) * LANES] * utm[k])
    yb = _gelu_tanh(jnp.concatenate(ys, axis=1))
    gate = jax.nn.sigmoid(jnp.dot(yb.astype(BF16), wglu_ref[...], preferred_element_type=F32) + bglu_ref[...])
    o = yb * gate
    for k in range(S5_SLABS):
        otm[k] = o[:, k * LANES:(k + 1) * LANES]
    for n in range(nb):
        for k in range(S5_SLABS):
            ob_ref[n, :, k * LANES:(k + 1) * LANES] = otm[k, pl.ds(n, tc, stride=nb), :].astype(BF16)

    @pl.when(step == pl.num_programs(0) - 1)
    def _():
        sre_ref[...] = st_re[...]
        sim_ref[...] = st_im[...]


def _s5_call(u, x0_re, x0_im, ab_re, ab_im, b_re_bd, b_im_bd, c_re_bd, c_im_bd, d_skip, w_glu, b_glu, tc):
    nb, t_len, _ = u.shape
    lane_w = 1024 if nb <= 8 else 512
    body = functools.partial(_s5_body, tc=tc, nb=nb, lane_w=lane_w)
    state = jax.ShapeDtypeStruct((nb, S5_LANES), F32)
    seq_spec = pl.BlockSpec((nb, tc, S5_WIDTH), lambda i: (0, i, 0))
    slabs = pltpu.VMEM((S5_SLABS, tc * nb, LANES), F32)
    return pl.pallas_call(
        body,
        grid=(t_len // tc,),
        in_specs=[seq_spec,
                  _const_spec((nb, S5_LANES)), _const_spec((nb, S5_LANES)),
                  _const_spec((1, S5_LANES)), _const_spec((1, S5_LANES)),
                  _const_spec(b_re_bd.shape), _const_spec(b_im_bd.shape),
                  _const_spec(c_re_bd.shape), _const_spec(c_im_bd.shape),
                  _const_spec((1, S5_WIDTH)), _const_spec((S5_WIDTH, S5_WIDTH)), _const_spec((1, S5_WIDTH))],
        out_specs=(seq_spec, _const_spec((nb, S5_LANES)), _const_spec((nb, S5_LANES))),
        out_shape=(jax.ShapeDtypeStruct((nb, t_len, S5_WIDTH), BF16), state, state),
        scratch_shapes=[slabs, slabs,
                        pltpu.VMEM((tc, nb, S5_LANES), F32), pltpu.VMEM((tc, nb, S5_LANES), F32),
                        pltpu.VMEM((nb, S5_LANES), F32), pltpu.VMEM((nb, S5_LANES), F32)],
        compiler_params=_cparams("arbitrary"),
        name="s5",
    )(u, x0_re, x0_im, ab_re, ab_im, b_re_bd, b_im_bd, c_re_bd, c_im_bd, d_skip, w_glu, b_glu)


def _block_diag_slabs(w):
    g, r, c = w.shape
    w = w.reshape(S5_SLABS, S5_SLAB_GROUPS, r, c)
    eye = jnp.eye(S5_SLAB_GROUPS, dtype=w.dtype)
    bd = w[:, :, :, None, :] * eye[None, :, None, :, None]
    return bd.reshape(S5_SLABS, S5_SLAB_GROUPS * r, S5_SLAB_GROUPS * c)


def _merge_body(oa_ref, ob_ref, ga_ref, gb_ref, x_ref, wa_ref, wb_ref, wo_ref, g_ref, x1_ref, hf_ref):
    pa = jnp.dot(oa_ref[...], wa_ref[...], preferred_element_type=F32)
    pb = jnp.dot(ob_ref[...], wb_ref[...], preferred_element_type=F32)
    merged = ga_ref[...] * pa + gb_ref[...] * pb
    x1 = x_ref[...] + jnp.dot(merged.astype(BF16), wo_ref[...], preferred_element_type=F32)
    x1_ref[...] = x1
    hf = (x1 * lax.rsqrt(jnp.mean(x1 * x1, axis=-1, keepdims=True) + EPS)) * g_ref[...]
    hf_ref[...] = hf.astype(BF16)


def _merge_call(oa, ob, gates, x2d, w_a, w_b, w_o, g_ffn, tm):
    m = x2d.shape[0]
    row = lambda w: pl.BlockSpec((tm, w), lambda i: (i, 0))
    return pl.pallas_call(
        _merge_body,
        grid=(m // tm,),
        in_specs=[row(ATT_WIDTH), row(S5_WIDTH),
                  pl.BlockSpec((tm, D_MODEL), lambda i: (i, 0)), pl.BlockSpec((tm, D_MODEL), lambda i: (i, 1)),
                  row(D_MODEL), _const_spec(w_a.shape), _const_spec(w_b.shape), _const_spec(w_o.shape),
                  _const_spec((1, D_MODEL))],
        out_specs=(row(D_MODEL), row(D_MODEL)),
        out_shape=(jax.ShapeDtypeStruct((m, D_MODEL), F32), jax.ShapeDtypeStruct((m, D_MODEL), BF16)),
        compiler_params=_cparams("parallel"),
        name="merge",
    )(oa, ob, gates, gates, x2d, w_a, w_b, w_o, g_ffn)


def _ffn_body(hf_ref, x1_ref, wg_ref, wu_ref, wd_ref, g_ref, y_ref, acc_ref, *, final_norm):
    f = pl.program_id(1)
    hf = hf_ref[...]
    a = jax.nn.silu(jnp.dot(hf, wg_ref[...], preferred_element_type=F32)) * jnp.dot(
        hf, wu_ref[...], preferred_element_type=F32)
    part = jnp.dot(a.astype(BF16), wd_ref[...], preferred_element_type=F32)

    @pl.when(f == 0)
    def _():
        acc_ref[...] = part

    @pl.when(f > 0)
    def _():
        acc_ref[...] += part

    @pl.when(f == pl.num_programs(1) - 1)
    def _():
        x2 = x1_ref[...] + acc_ref[...]
        if final_norm:
            x2 = (x2 * lax.rsqrt(jnp.mean(x2 * x2, axis=-1, keepdims=True) + EPS)) * g_ref[...]
        y_ref[...] = x2


def _ffn_call(hf, x1, w_gate, w_up, w_down, g_final, tm, tf, final_norm):
    m = hf.shape[0]
    return pl.pallas_call(
        functools.partial(_ffn_body, final_norm=final_norm),
        grid=(m // tm, D_FF // tf),
        in_specs=[pl.BlockSpec((tm, D_MODEL), lambda i, f: (i, 0)),
                  pl.BlockSpec((tm, D_MODEL), lambda i, f: (i, 0)),
                  pl.BlockSpec((D_MODEL, tf), lambda i, f: (0, f)),
                  pl.BlockSpec((D_MODEL, tf), lambda i, f: (0, f)),
                  pl.BlockSpec((tf, D_MODEL), lambda i, f: (f, 0)),
                  _const_spec((1, D_MODEL))],
        out_specs=pl.BlockSpec((tm, D_MODEL), lambda i, f: (i, 0)),
        out_shape=jax.ShapeDtypeStruct((m, D_MODEL), F32),
        scratch_shapes=[pltpu.VMEM((tm, D_MODEL), F32)],
        compiler_params=_cparams("parallel", "arbitrary"),
        name="ffn",
    )(hf, x1, w_gate, w_up, w_down, g_final)


def _rope_tables(pos, dim):
    half = dim // 2
    inv = 1.0 / (ROPE_THETA ** (jnp.arange(half, dtype=F32) * (2.0 / dim)))
    ang = pos[:, None] * inv[None, :]
    cos, sin = jnp.cos(ang), jnp.sin(ang)
    reps = LANES // dim
    return (jnp.tile(jnp.concatenate([cos, cos], axis=-1), (1, reps)),
            jnp.tile(jnp.concatenate([-sin, sin], axis=-1), (1, reps)))


def _pack_w_in(w_in):
    o = IN_OFFS
    seg = lambda i: w_in[:, o[i]:o[i + 1]]
    pad = lambda w: jnp.pad(w, ((0, 0), (0, LANES - w.shape[1])))
    w_pack = jnp.concatenate([seg(0), seg(1), seg(2), seg(3), pad(seg(4)), pad(seg(5)), seg(6)], axis=1)
    return w_pack.astype(BF16), seg(7).astype(BF16)


def _layer(x, pos, past, ssm0, lw, tm_proj):
    nb, t_len, _ = x.shape
    m = nb * t_len
    x2d = x.reshape(m, D_MODEL)

    cq, sq = _rope_tables(pos, HEAD_DIM)
    ci, si = _rope_tables(pos, IDX_DIM)
    tabs = (cq, sq, ci, si)
    if t_len < tm_proj:
        tabs = tuple(jnp.tile(t, (tm_proj // t_len, 1)) for t in tabs)

    h, q, k, v, kb, vb, iq, ik, iw, u = _proj_call(x2d, lw['g_mix'], lw['w_pack'], tabs, tm_proj)
    gates = _gates_call(h, lw['w_gl'], tm_proj, 1024)

    if past is None:
        s_chunk = DSA_GROUP_BLOCKS * CHUNK
        kb_all = kb.reshape(nb, t_len, KV_WIDTH)
        vb_all = vb.reshape(nb, t_len, KV_WIDTH)
        ikbd = _indexer_key_blocks(ik.reshape(nb, t_len, IDX_DIM), s_chunk)
        n_blk = t_len // CHUNK
        o_a = jnp.zeros((m, ATT_WIDTH), BF16)
        for blk0 in range(0, n_blk, DSA_GROUP_BLOCKS):
            o_a = _dsa_call(q, iq, iw, kb_all, vb_all, ikbd, o_a, 1, blk0, DSA_GROUP_BLOCKS, n_blk,
                            (blk0 + DSA_GROUP_BLOCKS) * CHUNK, s_chunk, None, min(TOPK_MAX, t_len // 4))
    else:
        pk, pv, pik = past
        l_keys = pk.shape[1] + t_len
        s_keys = -(-l_keys // LANES) * LANES
        padk = lambda a: jnp.pad(a, ((0, 0), (0, s_keys - l_keys), (0, 0)))
        kb_all = padk(jnp.concatenate([pk.reshape(nb, -1, KV_WIDTH).astype(BF16),
                                       kb.reshape(nb, t_len, KV_WIDTH)], axis=1))
        vb_all = padk(jnp.concatenate([pv.reshape(nb, -1, KV_WIDTH).astype(BF16),
                                       vb.reshape(nb, t_len, KV_WIDTH)], axis=1))
        ik_all = padk(jnp.concatenate([pik, ik.reshape(nb, t_len, IDX_DIM)], axis=1))
        ikbd = _indexer_key_blocks(ik_all, s_keys)
        o_a = _dsa_call(q, iq, iw, kb_all, vb_all, ikbd, None, 2, 0, 1, 1, s_keys, s_keys,
                        l_keys, min(TOPK_MAX, l_keys // 4))

    x0_re, x0_im = ssm0
    o_b, s_re, s_im = _s5_call(
        u.reshape(nb, t_len, S5_WIDTH), x0_re.reshape(nb, S5_LANES), x0_im.reshape(nb, S5_LANES),
        lw['ab_re'], lw['ab_im'], lw['b_re_bd'], lw['b_im_bd'], lw['c_re_bd'], lw['c_im_bd'],
        lw['d_skip'], lw['w_glu'], lw['b_glu'], min(t_len, S5_STEP_ROWS // nb))

    x1, hf = _merge_call(o_a, o_b.reshape(m, S5_WIDTH), gates, x2d, lw['w_proj_a'], lw['w_proj_b'],
                         lw['w_out'], lw['g_ffn'], 256)
    caches = (k.reshape(nb, t_len, N_KV_HEADS, HEAD_DIM), v.reshape(nb, t_len, N_KV_HEADS, HEAD_DIM),
              ik.reshape(nb, t_len, IDX_DIM), s_re.reshape(nb, S5_GROUPS, S5_STATE),
              s_im.reshape(nb, S5_GROUPS, S5_STATE))
    return x1, hf, caches


def kernel(x_prompt, x_sample, cache_k, cache_v, cache_idx_k, state_ssm_re, state_ssm_im,
           g_mix, w_in, a_re, a_im, log_dt, b_re, b_im, c_re, c_im, d_skip, w_glu, b_glu,
           w_proj_a, w_proj_b, w_out, g_ffn, w_gate, w_up, w_down, g_final):
    depth = w_in.shape[0]
    t_p, t_s = x_prompt.shape[1], x_sample.shape[1]
    past_len = cache_k.shape[2]
    pos_p = jnp.arange(t_p, dtype=F32)
    pos_s = past_len + jnp.arange(t_s, dtype=F32)
    nb_p, nb_s = x_prompt.shape[0], x_sample.shape[0]
    g_fin = g_final.reshape(1, D_MODEL)

    hp, hs = x_prompt, x_sample
    outs_p, outs_s = [], []
    for l in range(depth):
        w_pack, w_gl = _pack_w_in(w_in[l])
        ab_re, ab_im, bb_re, bb_im = _s5prep_call(a_re[l], a_im[l], log_dt[l], b_re[l], b_im[l])
        per_group = lambda bb: jnp.transpose(bb.reshape(S5_CH, S5_GROUPS, S5_STATE), (1, 0, 2))
        lw = {
            'g_mix': g_mix[l].reshape(1, D_MODEL), 'w_pack': w_pack, 'w_gl': w_gl,
            'ab_re': ab_re.reshape(1, S5_LANES), 'ab_im': ab_im.reshape(1, S5_LANES),
            'b_re_bd': _block_diag_slabs(per_group(bb_re)).astype(BF16),
            'b_im_bd': _block_diag_slabs(per_group(bb_im)).astype(BF16),
            'c_re_bd': _block_diag_slabs(jnp.transpose(c_re[l], (0, 2, 1))).astype(BF16),
            'c_im_bd': _block_diag_slabs(jnp.transpose(c_im[l], (0, 2, 1))).astype(BF16),
            'd_skip': d_skip[l].reshape(1, S5_WIDTH), 'w_glu': w_glu[l].astype(BF16),
            'b_glu': b_glu[l].reshape(1, S5_WIDTH),
            'w_proj_a': w_proj_a[l].astype(BF16), 'w_proj_b': w_proj_b[l].astype(BF16),
            'w_out': w_out[l].astype(BF16), 'g_ffn': g_ffn[l].reshape(1, D_MODEL),
        }
        wg, wu, wd = w_gate[l].astype(BF16), w_up[l].astype(BF16), w_down[l].astype(BF16)
        last = l == depth - 1

        zeros = jnp.zeros((nb_p, S5_GROUPS, S5_STATE), F32)
        x1p, hfp, cp = _layer(hp, pos_p, None, (zeros, zeros), lw, 512)
        x1s, hfs, cs = _layer(hs, pos_s, (cache_k[l], cache_v[l], cache_idx_k[l]),
                              (state_ssm_re[l], state_ssm_im[l]), lw, 512)
        yp = _ffn_call(hfp, x1p, wg, wu, wd, g_fin, 512, 512, last)
        ys = _ffn_call(hfs, x1s, wg, wu, wd, g_fin, 512, 512, last)
        hp = yp.reshape(nb_p, t_p, D_MODEL)
        hs = ys.reshape(nb_s, t_s, D_MODEL)
        outs_p.append(cp)
        outs_s.append(cs)

    stack = lambda outs, i: jnp.stack([o[i] for o in outs])
    return (hp, hs,
            stack(outs_p, 0), stack(outs_p, 1), stack(outs_p, 2), stack(outs_p, 3), stack(outs_p, 4),
            stack(outs_s, 0), stack(outs_s, 1), stack(outs_s, 2), stack(outs_s, 3), stack(outs_s, 4))
```

```python
import functools
import math

import jax
import jax.numpy as jnp
import numpy as np
from jax import lax
from jax.experimental import pallas as pl
from jax.experimental.pallas import tpu as pltpu

F32 = jnp.float32
BF16 = jnp.bfloat16

D_MODEL = 2048
CHUNK = 64
HEAD_DIM = 128
N_HEADS = 8
N_KV_HEADS = 2
GROUP = N_HEADS // N_KV_HEADS
ATT_WIDTH = N_HEADS * HEAD_DIM
KV_WIDTH = N_KV_HEADS * HEAD_DIM
IDX_HEADS = 16
IDX_DIM = 64
IDX_WIDTH = IDX_HEADS * IDX_DIM
TOPK_MAX = 256
S5_CH = 16
S5_WIDTH = D_MODEL // 2
S5_GROUPS = S5_WIDTH // S5_CH
S5_STATE = 64
S5_LANES = S5_GROUPS * S5_STATE
D_FF = 5632
ROPE_THETA = 10000.0
EPS = 1e-6
IN_SIZES = (ATT_WIDTH, KV_WIDTH, KV_WIDTH, IDX_WIDTH, IDX_DIM, IDX_HEADS, S5_WIDTH, 2 * D_MODEL)
IN_OFFS = tuple(int(s) for s in np.cumsum((0,) + IN_SIZES))

LANES = 128
S5_SLAB_GROUPS = LANES // S5_CH
S5_SLABS = S5_GROUPS // S5_SLAB_GROUPS
S5_SLAB_STATE = S5_SLAB_GROUPS * S5_STATE
INT_MIN = -2 ** 31
DSA_GROUP_BLOCKS = 4
QROWS = 2 * CHUNK
S5_STEP_ROWS = 512
SEARCH_PART = 128

VMEM_LIMIT = 56 * 2 ** 20


def _cparams(*sem):
    return pltpu.CompilerParams(dimension_semantics=sem, vmem_limit_bytes=VMEM_LIMIT)


def _const_spec(shape):
    nd = len(shape)
    return pl.BlockSpec(shape, lambda *_: (0,) * nd, pipeline_mode=pl.Buffered(1))


_PQ = 0
_PK = _PQ + ATT_WIDTH
_PV = _PK + KV_WIDTH
_PIQ = _PV + KV_WIDTH
_PIK = _PIQ + IDX_WIDTH
_PIW = _PIK + LANES
_PU = _PIW + LANES
_PEND = _PU + S5_WIDTH
IW_SCALE = (IDX_DIM ** -0.5) * (IDX_HEADS ** -0.5)


def _rope128(z, cos, sin):
    return z * cos + pltpu.roll(z, HEAD_DIM // 2, 1) * sin


def _rope64(z, cos, sin, low_half):
    partner = jnp.where(low_half, pltpu.roll(z, LANES - IDX_DIM // 2, 1), pltpu.roll(z, IDX_DIM // 2, 1))
    return z * cos + partner * sin


def _proj_body(x_ref, g_ref, w_ref, cq_ref, sq_ref, ci_ref, si_ref,
               h_ref, q_ref, k_ref, v_ref, kb_ref, vb_ref, iq_ref, ik_ref, iw_ref, u_ref):
    x = x_ref[...]
    h = (x * lax.rsqrt(jnp.mean(x * x, axis=-1, keepdims=True) + EPS)) * g_ref[...]
    hb = h.astype(BF16)
    h_ref[...] = hb

    def proj(lo, hi):
        return jnp.dot(hb, w_ref[:, lo:hi], preferred_element_type=F32)

    cq, sq, ci, si = cq_ref[...], sq_ref[...], ci_ref[...], si_ref[...]
    low_half = (lax.broadcasted_iota(jnp.int32, cq.shape, 1) & (IDX_DIM - 1)) < (IDX_DIM // 2)

    zq = proj(_PQ, _PK)
    for hd in range(N_HEADS):
        sl = slice(hd * HEAD_DIM, (hd + 1) * HEAD_DIM)
        q_ref[:, sl] = _rope128(zq[:, sl], cq, sq).astype(BF16)
    zk = proj(_PK, _PV)
    for hd in range(N_KV_HEADS):
        sl = slice(hd * HEAD_DIM, (hd + 1) * HEAD_DIM)
        r = _rope128(zk[:, sl], cq, sq)
        k_ref[:, sl] = r
        kb_ref[:, sl] = r.astype(BF16)
    zv = proj(_PV, _PIQ)
    v_ref[...] = zv
    vb_ref[...] = zv.astype(BF16)
    ziq = proj(_PIQ, _PIK)
    for p in range(IDX_WIDTH // LANES):
        sl = slice(p * LANES, (p + 1) * LANES)
        iq_ref[:, sl] = _rope64(ziq[:, sl], ci, si, low_half).astype(BF16)
    z2 = proj(_PIK, _PU)
    ik_ref[...] = _rope64(z2[:, :LANES], ci, si, low_half)[:, :IDX_DIM]
    iw_ref[...] = z2[:, LANES:] * IW_SCALE
    u_ref[...] = proj(_PU, _PEND)


def _proj_call(x2d, g_mix, w_pack, tabs, tm):
    m = x2d.shape[0]
    n_tiles = m // tm
    tab_tiles = tabs[0].shape[0] // tm
    row = lambda w: pl.BlockSpec((tm, w), lambda i: (i, 0))
    tab_spec = pl.BlockSpec((tm, LANES), lambda i: (i % tab_tiles, 0))
    out_shape = (
        jax.ShapeDtypeStruct((m, D_MODEL), BF16),
        jax.ShapeDtypeStruct((m, ATT_WIDTH), BF16),
        jax.ShapeDtypeStruct((m, KV_WIDTH), F32),
        jax.ShapeDtypeStruct((m, KV_WIDTH), F32),
        jax.ShapeDtypeStruct((m, KV_WIDTH), BF16),
        jax.ShapeDtypeStruct((m, KV_WIDTH), BF16),
        jax.ShapeDtypeStruct((m, IDX_WIDTH), BF16),
        jax.ShapeDtypeStruct((m, IDX_DIM), F32),
        jax.ShapeDtypeStruct((m, LANES), F32),
        jax.ShapeDtypeStruct((m, S5_WIDTH), F32),
    )
    out_specs = (row(D_MODEL), row(ATT_WIDTH), row(KV_WIDTH), row(KV_WIDTH), row(KV_WIDTH), row(KV_WIDTH),
                 row(IDX_WIDTH), row(IDX_DIM), row(LANES), row(S5_WIDTH))
    return pl.pallas_call(
        _proj_body,
        grid=(n_tiles,),
        in_specs=[row(D_MODEL), _const_spec((1, D_MODEL)), _const_spec(w_pack.shape),
                  tab_spec, tab_spec, tab_spec, tab_spec],
        out_specs=out_specs,
        out_shape=out_shape,
        compiler_params=_cparams("parallel"),
        name="proj",
    )(x2d, g_mix, w_pack, *tabs)


def _gates_body(h_ref, w_ref, o_ref):
    o_ref[...] = jax.nn.sigmoid(jnp.dot(h_ref[...], w_ref[...], preferred_element_type=F32))


def _gates_call(h, w_gl, tm):
    m, n = h.shape[0], w_gl.shape[1]
    return pl.pallas_call(
        _gates_body,
        grid=(m // tm,),
        in_specs=[pl.BlockSpec((tm, D_MODEL), lambda i: (i, 0)), _const_spec(w_gl.shape)],
        out_specs=pl.BlockSpec((tm, n), lambda i: (i, 0)),
        out_shape=jax.ShapeDtypeStruct((m, n), F32),
        compiler_params=_cparams("parallel"),
        name="gates",
    )(h, w_gl)


LOG2_E = math.log2(math.e)
NEG_INF_KEY = INT_MIN + 0x7FFFFF
F32_MAX = float(np.finfo(np.float32).max)


def _key_to_f32(key):
    return lax.bitcast_convert_type(key ^ ((key >> 31) & jnp.int32(0x7FFFFFFF)), F32)


def _row_count(mask):
    return jnp.sum(jnp.where(mask, 1.0, 0.0), axis=1, keepdims=True)


def _resolve_threshold_ties(score, t_lo, t_next, excess, kf, s_keys, bias_ref, cand_ref, pick_ref, rem_ref):
    tied_row = excess > 0.0
    ge = score >= t_lo
    above = score >= t_next
    cand_ref[...] = jnp.where(ge & jnp.logical_not(above) & tied_row, 1.0, 0.0)
    pick_ref[...] = jnp.zeros_like(score)
    rem0 = jnp.where(tied_row, kf - _row_count(above), 0.0)
    rem_ref[...] = jnp.broadcast_to(rem0, rem_ref.shape)
    col = lax.broadcasted_iota(jnp.int32, score.shape, 1)

    def take_next_value(_):
        cand = cand_ref[...] > 0.5
        rem = rem_ref[:, :1]
        top = jnp.max(jnp.where(cand, score, -jnp.inf), axis=1, keepdims=True)
        eq = cand & (score == top)
        last = jnp.zeros((score.shape[0], 1), jnp.int32)
        for b in range(int(s_keys).bit_length() - 1, -1, -1):
            nxt = last + jnp.int32(1 << b)
            last = jnp.where(_row_count(eq & (col < nxt)) < rem, nxt, last)
        take = eq & (col <= last) & (rem > 0.0)
        pick_ref[...] = jnp.where(take, 1.0, pick_ref[...])
        cand_ref[...] = jnp.where(eq, 0.0, cand_ref[...])
        rem = jnp.where(top > -jnp.inf, rem - _row_count(take), 0.0)
        rem_ref[...] = jnp.broadcast_to(rem, rem_ref.shape)
        return jnp.max(rem) > 0.0

    lax.while_loop(lambda go: go, take_next_value, jnp.max(rem0) > 0.0)
    chosen = above | (pick_ref[...] > 0.5)
    bias_ref[...] = jnp.where(tied_row, jnp.where(chosen, 0.0, -jnp.inf), bias_ref[...])


def _dsa_core(q_ref, iq_ref, iw_ref, get_k, get_v, get_ikbd, n_keysets,
              o_ref, score_ref, bias_ref, cand_ref, pick_ref, rem_ref, *,
              s_keys, s_chunk, blk0, fixed_valid, topk):
    halves = [(slice(hf * CHUNK, (hf + 1) * CHUNK), hf % n_keysets) for hf in range(QROWS // CHUNK)]

    pairs = IDX_WIDTH // LANES
    for hf, (rows, ks) in enumerate(halves):
        if fixed_valid is None:
            valid = (blk0 + len(halves) * pl.program_id(1) + hf + 1) * CHUNK
        else:
            valid = fixed_valid
        iq = iq_ref[rows, :]
        lhs = jnp.concatenate([iq[:, p * LANES:(p + 1) * LANES] for p in range(pairs)], axis=0)
        iw = iw_ref[rows, :]
        for c in range(s_keys // s_chunk):
            logits = lax.dot_general(lhs, get_ikbd(ks, c), (((1,), (1,)), ((), ())), preferred_element_type=F32)
            acc = jnp.zeros((CHUNK, s_chunk), F32)
            for p in range(pairs):
                lp = logits[p * CHUNK:(p + 1) * CHUNK]
                acc = acc + jnp.maximum(lp[:, :s_chunk], 0.0) * iw[:, 2 * p:2 * p + 1]
                acc = acc + jnp.maximum(lp[:, s_chunk:], 0.0) * iw[:, 2 * p + 1:2 * p + 2]
            col = c * s_chunk + lax.broadcasted_iota(jnp.int32, acc.shape, 1)
            score_ref[rows, c * s_chunk:(c + 1) * s_chunk] = jnp.where(col < valid, acc, -jnp.inf)

    score = score_ref[...]
    if s_keys <= topk:
        bias_ref[...] = jnp.where(score > -jnp.inf, 0.0, -jnp.inf)
    else:
        kf = float(topk)
        score_t = score.T

        def count_ge(t):
            ind = jnp.where(score_t >= t, 1.0, 0.0).reshape(s_keys // SEARCH_PART, SEARCH_PART, QROWS)
            return jnp.sum(jnp.sum(ind, axis=0), axis=0, keepdims=True)

        thr = jnp.full((1, QROWS), INT_MIN, jnp.int32)
        for b in range(31, -1, -1):
            cand = thr + jnp.int32(INT_MIN if b == 31 else 1 << b)
            thr = jnp.where(count_ge(_key_to_f32(cand)) >= kf, cand, thr)
        thr = jnp.maximum(thr, jnp.int32(NEG_INF_KEY))
        t_lo = jnp.maximum(_key_to_f32(thr), -F32_MAX)
        excess = count_ge(t_lo) - kf
        stats = jnp.concatenate([t_lo, _key_to_f32(thr + 1), excess, jnp.zeros((QROWS - 3, QROWS), F32)], axis=0).T
        t_lo_col = stats[:, 0:1]
        bias_ref[...] = jnp.where(score >= t_lo_col, 0.0, -jnp.inf)

        @pl.when(jnp.max(excess) > 0.0)
        def _():
            _resolve_threshold_ties(score, t_lo_col, stats[:, 1:2], stats[:, 2:3], kf, s_keys,
                                    bias_ref, cand_ref, pick_ref, rem_ref)

    scale = HEAD_DIM ** -0.5
    for rows, ks in halves:
        bias = bias_ref[rows, :]
        q = q_ref[rows, :]
        for c in range(N_KV_HEADS):
            kc = get_k(ks, c)
            vc = get_v(ks, c)
            qc = jnp.concatenate(
                [q[:, (c * GROUP + g) * HEAD_DIM:(c * GROUP + g + 1) * HEAD_DIM] for g in range(GROUP)], axis=0)
            logits = lax.dot_general(qc, kc, (((1,), (1,)), ((), ())), preferred_element_type=F32)
            es, inv = [], []
            for g in range(GROUP):
                lg = logits[g * CHUNK:(g + 1) * CHUNK] + bias
                e = jnp.exp2((lg - jnp.max(lg, axis=1, keepdims=True)) * (scale * LOG2_E))
                inv.append(1.0 / jnp.sum(e, axis=1, keepdims=True))
                es.append(e.astype(BF16))
            oc = jnp.dot(jnp.concatenate(es, axis=0), vc, preferred_element_type=F32)
            for g in range(GROUP):
                hd = c * GROUP + g
                o_ref[rows, hd * HEAD_DIM:(hd + 1) * HEAD_DIM] = (
                    oc[g * CHUNK:(g + 1) * CHUNK] * inv[g]).astype(BF16)


def _dsa_body(q_ref, iq_ref, iw_ref, kb_ref, vb_ref, ikbd_ref, *rest, **static):
    head = lambda ref: (lambda ks, c: ref[ks, :, c * HEAD_DIM:(c + 1) * HEAD_DIM])
    _dsa_core(q_ref, iq_ref, iw_ref, head(kb_ref), head(vb_ref), lambda ks, c: ikbd_ref[ks, c],
              kb_ref.shape[0], *rest[-6:], **static)


def _dsa_cached_body(q_ref, iq_ref, iw_ref, kn_ref, vn_ref, ikn_ref, pk_ref, pv_ref, pik_ref,
                     o_ref, k_all, v_all, ikbd_all, *scratch, past_len, **static):
    s_keys = static['s_keys']
    new_end = past_len + CHUNK
    for ks in range(2):
        new = slice(ks * CHUNK, (ks + 1) * CHUNK)
        for c in range(N_KV_HEADS):
            for dst, past, fresh in ((k_all, pk_ref, kn_ref), (v_all, pv_ref, vn_ref)):
                dst[ks, c, :past_len, :] = past[ks, pl.ds(c, past_len, stride=N_KV_HEADS), :].astype(BF16)
                dst[ks, c, past_len:new_end, :] = fresh[new, c * HEAD_DIM:(c + 1) * HEAD_DIM]
                dst[ks, c, new_end:, :] = jnp.zeros((s_keys - new_end, HEAD_DIM), BF16)
        ik = jnp.concatenate([pik_ref[ks].astype(BF16), ikn_ref[new, :].astype(BF16),
                              jnp.zeros((s_keys - new_end, IDX_DIM), BF16)], axis=0)
        z = jnp.zeros_like(ik)
        ikbd_all[ks, :s_keys, :] = jnp.concatenate([ik, z], axis=1)
        ikbd_all[ks, s_keys:, :] = jnp.concatenate([z, ik], axis=1)
    _dsa_core(q_ref, iq_ref, iw_ref, lambda ks, c: k_all[ks, c], lambda ks, c: v_all[ks, c],
              lambda ks, c: ikbd_all[ks], 2, o_ref, *scratch, **static)


def _dsa_cached_call(q, iq, iw, kb, vb, ik, pk, pv, pik, topk):
    nb, past_len = pik.shape[0], pik.shape[1]
    l_keys = past_len + CHUNK
    s_keys = -(-l_keys // LANES) * LANES
    body = functools.partial(_dsa_cached_body, past_len=past_len, s_keys=s_keys, s_chunk=s_keys, blk0=0,
                             fixed_valid=l_keys, topk=topk)
    qrow = lambda w: pl.BlockSpec((QROWS, w), lambda n: (n, 0))
    pair = lambda a: pl.BlockSpec((2,) + a.shape[1:], lambda n: (n, 0, 0))
    mask_buf = pltpu.VMEM((QROWS, s_keys), F32)
    heads = pltpu.VMEM((2, N_KV_HEADS, s_keys, HEAD_DIM), BF16)
    return pl.pallas_call(
        body,
        grid=(nb // 2,),
        in_specs=[qrow(ATT_WIDTH), qrow(IDX_WIDTH), qrow(LANES), qrow(KV_WIDTH), qrow(KV_WIDTH), qrow(IDX_DIM),
                  pair(pk), pair(pv), pair(pik)],
        out_specs=qrow(ATT_WIDTH),
        out_shape=jax.ShapeDtypeStruct((nb * CHUNK, ATT_WIDTH), BF16),
        scratch_shapes=[heads, heads, pltpu.VMEM((2, 2 * s_keys, LANES), BF16),
                        mask_buf, mask_buf, mask_buf, mask_buf, pltpu.VMEM((QROWS, LANES), F32)],
        compiler_params=_cparams("parallel"),
        name="dsa_cached",
    )(q, iq, iw, kb, vb, ik, pk, pv, pik)


def _dsa_call(q, iq, iw, kb, vb, ikbd, o_prev, keysets, blk0, n_blk, blks_per_seq, s_keys, s_chunk,
              fixed_valid, topk):
    nb = kb.shape[0]
    blocks_per_step = QROWS // CHUNK
    steps_per_outer = blks_per_seq * keysets // blocks_per_step
    qrow = lambda w: pl.BlockSpec(
        (QROWS, w), lambda n, j: (n * steps_per_outer + blk0 // blocks_per_step + j, 0))
    n_chunks = s_keys // s_chunk
    body = functools.partial(_dsa_body, s_keys=s_keys, s_chunk=s_chunk, blk0=blk0,
                             fixed_valid=fixed_valid, topk=topk)
    args = [q, iq, iw, kb, vb, ikbd]
    in_specs = [qrow(ATT_WIDTH), qrow(IDX_WIDTH), qrow(LANES),
                pl.BlockSpec((keysets, s_keys, KV_WIDTH), lambda n, j: (n, 0, 0)),
                pl.BlockSpec((keysets, s_keys, KV_WIDTH), lambda n, j: (n, 0, 0)),
                pl.BlockSpec((keysets, n_chunks, 2 * s_chunk, LANES), lambda n, j: (n, 0, 0, 0))]
    aliases = {}
    if o_prev is not None:
        aliases = {len(args): 0}
        args.append(o_prev)
        in_specs.append(pl.BlockSpec(memory_space=pl.ANY))
    mask_buf = pltpu.VMEM((QROWS, s_keys), F32)
    return pl.pallas_call(
        body,
        grid=(nb // keysets, n_blk * keysets // blocks_per_step),
        in_specs=in_specs,
        out_specs=qrow(ATT_WIDTH),
        out_shape=jax.ShapeDtypeStruct((nb * blks_per_seq * CHUNK, ATT_WIDTH), BF16),
        scratch_shapes=[mask_buf, mask_buf, mask_buf, mask_buf, pltpu.VMEM((QROWS, LANES), F32)],
        input_output_aliases=aliases,
        compiler_params=_cparams("parallel", "arbitrary"),
        name="dsa",
    )(*args)


def _indexer_key_blocks(ik, s_chunk):
    nb, s, _ = ik.shape
    ikb = ik.astype(BF16).reshape(nb, s // s_chunk, s_chunk, IDX_DIM)
    z = jnp.zeros_like(ikb)
    return jnp.concatenate([jnp.concatenate([ikb, z], axis=-1), jnp.concatenate([z, ikb], axis=-1)], axis=-2)


def _s5prep_body(are_ref, aim_ref, ldt_ref, bre_ref, bim_ref, abre_ref, abim_ref, bbre_ref, bbim_ref):
    a_re, a_im = are_ref[...], aim_ref[...]
    dt = jnp.exp(ldt_ref[...])
    mag = jnp.exp(dt * a_re)
    ab_re = mag * jnp.cos(dt * a_im)
    ab_im = mag * jnp.sin(dt * a_im)
    den = a_re * a_re + a_im * a_im
    f_re = ((ab_re - 1.0) * a_re + ab_im * a_im) / den
    f_im = (ab_im * a_re - (ab_re - 1.0) * a_im) / den
    abre_ref[...] = ab_re
    abim_ref[...] = ab_im
    for c in range(S5_CH):
        b_re, b_im = bre_ref[c], bim_ref[c]
        bbre_ref[c] = f_re * b_re - f_im * b_im
        bbim_ref[c] = f_re * b_im + f_im * b_re


def _s5prep_call(a_re, a_im, log_dt, b_re, b_im):
    rows = S5_LANES // LANES
    flat = lambda a: a.reshape(rows, LANES)
    ldt = jnp.broadcast_to(log_dt[:, None], (S5_GROUPS, S5_STATE))
    chan_major = lambda b: jnp.transpose(b, (2, 0, 1)).reshape(S5_CH, rows, LANES)
    small = jax.ShapeDtypeStruct((rows, LANES), F32)
    big = jax.ShapeDtypeStruct((S5_CH, rows, LANES), F32)
    return pl.pallas_call(
        _s5prep_body,
        out_shape=(small, small, big, big),
        name="s5prep",
    )(flat(a_re), flat(a_im), flat(ldt), chan_major(b_re), chan_major(b_im))


def _gelu_tanh(x):
    return 0.5 * x * (1.0 + jnp.tanh(math.sqrt(2.0 / math.pi) * (x + 0.044715 * (x * x * x))))


def _s5_body(u_ref, x0re_ref, x0im_ref, abre_ref, abim_ref, bre_ref, bim_ref, cre_ref, cim_ref,
             dskip_ref, wglu_ref, bglu_ref, ob_ref, sre_ref, sim_ref, utm, otm, xre, xim, st_re, st_im,
             *, tc, nb, lane_w):
    step = pl.program_id(0)

    @pl.when(step == 0)
    def _():
        st_re[...] = x0re_ref[...]
        st_im[...] = x0im_ref[...]

    for n in range(nb):
        for k in range(S5_SLABS):
            utm[k, pl.ds(n, tc, stride=nb), :] = u_ref[n, :, k * LANES:(k + 1) * LANES]

    for k in range(S5_SLABS):
        us = utm[k].astype(BF16)
        sl = slice(k * S5_SLAB_STATE, (k + 1) * S5_SLAB_STATE)
        xre[:, :, sl] = jnp.dot(us, bre_ref[k], preferred_element_type=F32).reshape(tc, nb, S5_SLAB_STATE)
        xim[:, :, sl] = jnp.dot(us, bim_ref[k], preferred_element_type=F32).reshape(tc, nb, S5_SLAB_STATE)

    for lc in range(S5_LANES // lane_w):
        sl = slice(lc * lane_w, (lc + 1) * lane_w)
        a_r = jnp.broadcast_to(abre_ref[:, sl], (nb, lane_w))
        a_i = jnp.broadcast_to(abim_ref[:, sl], (nb, lane_w))

        def scan_step(t, carry, sl=sl, a_r=a_r, a_i=a_i):
            s_r, s_i = carry
            n_r = a_r * s_r - a_i * s_i + xre[t, :, sl]
            n_i = a_r * s_i + a_i * s_r + xim[t, :, sl]
            xre[t, :, sl] = n_r
            xim[t, :, sl] = n_i
            return n_r, n_i

        s_r, s_i = lax.fori_loop(0, tc, scan_step, (st_re[:, sl], st_im[:, sl]), unroll=8)
        st_re[:, sl] = s_r
        st_im[:, sl] = s_i

    ys = []
    for k in range(S5_SLABS):
        sl = slice(k * S5_SLAB_STATE, (k + 1) * S5_SLAB_STATE)
        xr = xre[:, :, sl].reshape(tc * nb, S5_SLAB_STATE).astype(BF16)
        xi = xim[:, :, sl].reshape(tc * nb, S5_SLAB_STATE).astype(BF16)
        ys.append(jnp.dot(xr, cre_ref[k], preferred_element_type=F32)
                  - jnp.dot(xi, cim_ref[k], preferred_element_type=F32)
                  + dskip_ref[:, k * LANES:(k + 1) * LANES] * utm[k])
    yb = _gelu_tanh(jnp.concatenate(ys, axis=1))
    gate = jax.nn.sigmoid(jnp.dot(yb.astype(BF16), wglu_ref[...], preferred_element_type=F32) + bglu_ref[...])
    o = yb * gate
    for k in range(S5_SLABS):
        otm[k] = o[:, k * LANES:(k + 1) * LANES]
    for n in range(nb):
        for k in range(S5_SLABS):
            ob_ref[n, :, k * LANES:(k + 1) * LANES] = otm[k, pl.ds(n, tc, stride=nb), :].astype(BF16)

    @pl.when(step == pl.num_programs(0) - 1)
    def _():
        sre_ref[...] = st_re[...]
        sim_ref[...] = st_im[...]


def _s5_call(u, x0_re, x0_im, ab_re, ab_im, b_re_bd, b_im_bd, c_re_bd, c_im_bd, d_skip, w_glu, b_glu, tc):
    nb, t_len, _ = u.shape
    lane_w = 1024 if nb <= 8 else 512
    body = functools.partial(_s5_body, tc=tc, nb=nb, lane_w=lane_w)
    state = jax.ShapeDtypeStruct((nb, S5_LANES), F32)
    seq_spec = pl.BlockSpec((nb, tc, S5_WIDTH), lambda i: (0, i, 0))
    slabs = pltpu.VMEM((S5_SLABS, tc * nb, LANES), F32)
    return pl.pallas_call(
        body,
        grid=(t_len // tc,),
        in_specs=[seq_spec,
                  _const_spec((nb, S5_LANES)), _const_spec((nb, S5_LANES)),
                  _const_spec((1, S5_LANES)), _const_spec((1, S5_LANES)),
                  _const_spec(b_re_bd.shape), _const_spec(b_im_bd.shape),
                  _const_spec(c_re_bd.shape), _const_spec(c_im_bd.shape),
                  _const_spec((1, S5_WIDTH)), _const_spec((S5_WIDTH, S5_WIDTH)), _const_spec((1, S5_WIDTH))],
        out_specs=(seq_spec, _const_spec((nb, S5_LANES)), _const_spec((nb, S5_LANES))),
        out_shape=(jax.ShapeDtypeStruct((nb, t_len, S5_WIDTH), BF16), state, state),
        scratch_shapes=[slabs, slabs,
                        pltpu.VMEM((tc, nb, S5_LANES), F32), pltpu.VMEM((tc, nb, S5_LANES), F32),
                        pltpu.VMEM((nb, S5_LANES), F32), pltpu.VMEM((nb, S5_LANES), F32)],
        compiler_params=_cparams("arbitrary"),
        name="s5",
    )(u, x0_re, x0_im, ab_re, ab_im, b_re_bd, b_im_bd, c_re_bd, c_im_bd, d_skip, w_glu, b_glu)


def _block_diag_slabs(w):
    g, r, c = w.shape
    w = w.reshape(S5_SLABS, S5_SLAB_GROUPS, r, c)
    eye = jnp.eye(S5_SLAB_GROUPS, dtype=w.dtype)
    bd = w[:, :, :, None, :] * eye[None, :, None, :, None]
    return bd.reshape(S5_SLABS, S5_SLAB_GROUPS * r, S5_SLAB_GROUPS * c)


def _merge_body(oa_ref, ob_ref, ga_ref, gb_ref, x_ref, wa_ref, wb_ref, wo_ref, g_ref, x1_ref, hf_ref):
    pa = jnp.dot(oa_ref[...], wa_ref[...], preferred_element_type=F32)
    pb = jnp.dot(ob_ref[...], wb_ref[...], preferred_element_type=F32)
    merged = ga_ref[...] * pa + gb_ref[...] * pb
    x1 = x_ref[...] + jnp.dot(merged.astype(BF16), wo_ref[...], preferred_element_type=F32)
    x1_ref[...] = x1
    hf = (x1 * lax.rsqrt(jnp.mean(x1 * x1, axis=-1, keepdims=True) + EPS)) * g_ref[...]
    hf_ref[...] = hf.astype(BF16)


def _merge_call(oa, ob, gates, x2d, w_a, w_b, w_o, g_ffn, tm):
    m = x2d.shape[0]
    row = lambda w: pl.BlockSpec((tm, w), lambda i: (i, 0))
    return pl.pallas_call(
        _merge_body,
        grid=(m // tm,),
        in_specs=[row(ATT_WIDTH), row(S5_WIDTH),
                  pl.BlockSpec((tm, D_MODEL), lambda i: (i, 0)), pl.BlockSpec((tm, D_MODEL), lambda i: (i, 1)),
                  row(D_MODEL), _const_spec(w_a.shape), _const_spec(w_b.shape), _const_spec(w_o.shape),
                  _const_spec((1, D_MODEL))],
        out_specs=(row(D_MODEL), row(D_MODEL)),
        out_shape=(jax.ShapeDtypeStruct((m, D_MODEL), F32), jax.ShapeDtypeStruct((m, D_MODEL), BF16)),
        compiler_params=_cparams("parallel"),
        name="merge",
    )(oa, ob, gates, gates, x2d, w_a, w_b, w_o, g_ffn)


FFN_OUT_CHUNK = 512


def _ffn_body(hf_ref, x1_hbm, wg_ref, wu_ref, wd_ref, g_ref, y_ref, x1_sem, *, final_norm, tm):
    i, f = pl.program_id(0), pl.program_id(1)
    x1_copy = pltpu.make_async_copy(x1_hbm.at[pl.ds(i * tm, tm), :], y_ref, x1_sem)

    @pl.when(f == 0)
    def _():
        x1_copy.start()

    hf = hf_ref[...]
    a = jax.nn.silu(jnp.dot(hf, wg_ref[...], preferred_element_type=F32)) * jnp.dot(
        hf, wu_ref[...], preferred_element_type=F32)
    ab = a.astype(BF16)

    @pl.when(f == 0)
    def _():
        x1_copy.wait()

    for c0 in range(0, D_MODEL, FFN_OUT_CHUNK):
        cols = slice(c0, c0 + FFN_OUT_CHUNK)
        y_ref[:, cols] += jnp.dot(ab, wd_ref[:, cols], preferred_element_type=F32)

    if final_norm:
        @pl.when(f == pl.num_programs(1) - 1)
        def _():
            x2 = y_ref[...]
            y_ref[...] = (x2 * lax.rsqrt(jnp.mean(x2 * x2, axis=-1, keepdims=True) + EPS)) * g_ref[...]


def _ffn_call(hf, x1, w_gate, w_up, w_down, g_final, tm, tf, final_norm):
    m = hf.shape[0]
    return pl.pallas_call(
        functools.partial(_ffn_body, final_norm=final_norm, tm=tm),
        grid=(m // tm, D_FF // tf),
        in_specs=[pl.BlockSpec((tm, D_MODEL), lambda i, f: (i, 0)),
                  pl.BlockSpec(memory_space=pl.ANY),
                  pl.BlockSpec((D_MODEL, tf), lambda i, f: (0, f)),
                  pl.BlockSpec((D_MODEL, tf), lambda i, f: (0, f)),
                  pl.BlockSpec((tf, D_MODEL), lambda i, f: (f, 0)),
                  _const_spec((1, D_MODEL))],
        out_specs=pl.BlockSpec((tm, D_MODEL), lambda i, f: (i, 0)),
        out_shape=jax.ShapeDtypeStruct((m, D_MODEL), F32),
        scratch_shapes=[pltpu.SemaphoreType.DMA(())],
        compiler_params=_cparams("parallel", "arbitrary"),
        name="ffn",
    )(hf, x1, w_gate, w_up, w_down, g_final)


def _rope_tables(pos, dim):
    half = dim // 2
    inv = 1.0 / (ROPE_THETA ** (jnp.arange(half, dtype=F32) * (2.0 / dim)))
    ang = pos[:, None] * inv[None, :]
    cos, sin = jnp.cos(ang), jnp.sin(ang)
    reps = LANES // dim
    return (jnp.tile(jnp.concatenate([cos, cos], axis=-1), (1, reps)),
            jnp.tile(jnp.concatenate([-sin, sin], axis=-1), (1, reps)))


def _pack_w_in(w_in):
    o = IN_OFFS
    seg = lambda i: w_in[:, o[i]:o[i + 1]]
    pad = lambda w: jnp.pad(w, ((0, 0), (0, LANES - w.shape[1])))
    w_pack = jnp.concatenate([seg(0), seg(1), seg(2), seg(3), pad(seg(4)), pad(seg(5)), seg(6)], axis=1)
    return w_pack.astype(BF16), seg(7).astype(BF16)


def _layer(x, pos, past, ssm0, lw, tm_proj):
    nb, t_len, _ = x.shape
    m = nb * t_len
    x2d = x.reshape(m, D_MODEL)

    cq, sq = _rope_tables(pos, HEAD_DIM)
    ci, si = _rope_tables(pos, IDX_DIM)
    tabs = (cq, sq, ci, si)
    if t_len < tm_proj:
        tabs = tuple(jnp.tile(t, (tm_proj // t_len, 1)) for t in tabs)

    h, q, k, v, kb, vb, iq, ik, iw, u = _proj_call(x2d, lw['g_mix'], lw['w_pack'], tabs, tm_proj)
    gates = _gates_call(h, lw['w_gl'], tm_proj)

    if past is None:
        s_chunk = DSA_GROUP_BLOCKS * CHUNK
        kb_all = kb.reshape(nb, t_len, KV_WIDTH)
        vb_all = vb.reshape(nb, t_len, KV_WIDTH)
        ikbd = _indexer_key_blocks(ik.reshape(nb, t_len, IDX_DIM), s_chunk)
        n_blk = t_len // CHUNK
        o_a = jnp.zeros((m, ATT_WIDTH), BF16)
        for blk0 in range(0, n_blk, DSA_GROUP_BLOCKS):
            o_a = _dsa_call(q, iq, iw, kb_all, vb_all, ikbd, o_a, 1, blk0, DSA_GROUP_BLOCKS, n_blk,
                            (blk0 + DSA_GROUP_BLOCKS) * CHUNK, s_chunk, None, min(TOPK_MAX, t_len // 4))
    else:
        pk, pv, pik = past
        assert t_len == CHUNK, "the cached stream is one query block per sequence"
        l_keys = pk.shape[1] + t_len
        o_a = _dsa_cached_call(q, iq, iw, kb, vb, ik, pk.reshape(nb, -1, HEAD_DIM), pv.reshape(nb, -1, HEAD_DIM),
                               pik, min(TOPK_MAX, l_keys // 4))

    x0_re, x0_im = ssm0
    o_b, s_re, s_im = _s5_call(
        u.reshape(nb, t_len, S5_WIDTH), x0_re.reshape(nb, S5_LANES), x0_im.reshape(nb, S5_LANES),
        lw['ab_re'], lw['ab_im'], lw['b_re_bd'], lw['b_im_bd'], lw['c_re_bd'], lw['c_im_bd'],
        lw['d_skip'], lw['w_glu'], lw['b_glu'], min(t_len, S5_STEP_ROWS // nb))

    x1, hf = _merge_call(o_a, o_b.reshape(m, S5_WIDTH), gates, x2d, lw['w_proj_a'], lw['w_proj_b'],
                         lw['w_out'], lw['g_ffn'], 256)
    caches = (k.reshape(nb, t_len, N_KV_HEADS, HEAD_DIM), v.reshape(nb, t_len, N_KV_HEADS, HEAD_DIM),
              ik.reshape(nb, t_len, IDX_DIM), s_re.reshape(nb, S5_GROUPS, S5_STATE),
              s_im.reshape(nb, S5_GROUPS, S5_STATE))
    return x1, hf, caches


def kernel(x_prompt, x_sample, cache_k, cache_v, cache_idx_k, state_ssm_re, state_ssm_im,
           g_mix, w_in, a_re, a_im, log_dt, b_re, b_im, c_re, c_im, d_skip, w_glu, b_glu,
           w_proj_a, w_proj_b, w_out, g_ffn, w_gate, w_up, w_down, g_final):
    depth = w_in.shape[0]
    t_p, t_s = x_prompt.shape[1], x_sample.shape[1]
    past_len = cache_k.shape[2]
    pos_p = jnp.arange(t_p, dtype=F32)
    pos_s = past_len + jnp.arange(t_s, dtype=F32)
    nb_p, nb_s = x_prompt.shape[0], x_sample.shape[0]
    g_fin = g_final.reshape(1, D_MODEL)

    hp, hs = x_prompt, x_sample
    outs_p, outs_s = [], []
    for l in range(depth):
        w_pack, w_gl = _pack_w_in(w_in[l])
        ab_re, ab_im, bb_re, bb_im = _s5prep_call(a_re[l], a_im[l], log_dt[l], b_re[l], b_im[l])
        per_group = lambda bb: jnp.transpose(bb.reshape(S5_CH, S5_GROUPS, S5_STATE), (1, 0, 2))
        lw = {
            'g_mix': g_mix[l].reshape(1, D_MODEL), 'w_pack': w_pack, 'w_gl': w_gl,
            'ab_re': ab_re.reshape(1, S5_LANES), 'ab_im': ab_im.reshape(1, S5_LANES),
            'b_re_bd': _block_diag_slabs(per_group(bb_re)).astype(BF16),
            'b_im_bd': _block_diag_slabs(per_group(bb_im)).astype(BF16),
            'c_re_bd': _block_diag_slabs(jnp.transpose(c_re[l], (0, 2, 1))).astype(BF16),
            'c_im_bd': _block_diag_slabs(jnp.transpose(c_im[l], (0, 2, 1))).astype(BF16),
            'd_skip': d_skip[l].reshape(1, S5_WIDTH), 'w_glu': w_glu[l].astype(BF16),
            'b_glu': b_glu[l].reshape(1, S5_WIDTH),
            'w_proj_a': w_proj_a[l].astype(BF16), 'w_proj_b': w_proj_b[l].astype(BF16),
            'w_out': w_out[l].astype(BF16), 'g_ffn': g_ffn[l].reshape(1, D_MODEL),
        }
        wg, wu, wd = w_gate[l].astype(BF16), w_up[l].astype(BF16), w_down[l].astype(BF16)
        last = l == depth - 1

        zeros = jnp.zeros((nb_p, S5_GROUPS, S5_STATE), F32)
        x1p, hfp, cp = _layer(hp, pos_p, None, (zeros, zeros), lw, 512)
        x1s, hfs, cs = _layer(hs, pos_s, (cache_k[l], cache_v[l], cache_idx_k[l]),
                              (state_ssm_re[l], state_ssm_im[l]), lw, 512)
        yp = _ffn_call(hfp, x1p, wg, wu, wd, g_fin, 1024, 512, last)
        ys = _ffn_call(hfs, x1s, wg, wu, wd, g_fin, 1024, 512, last)
        hp = yp.reshape(nb_p, t_p, D_MODEL)
        hs = ys.reshape(nb_s, t_s, D_MODEL)
        outs_p.append(cp)
        outs_s.append(cs)

    stack = lambda outs, i: jnp.stack([o[i] for o in outs])
    return (hp, hs,
            stack(outs_p, 0), stack(outs_p, 1), stack(outs_p, 2), stack(outs_p, 3), stack(outs_p, 4),
            stack(outs_s, 0), stack(outs_s, 1), stack(outs_s, 2), stack(outs_s, 3), stack(outs_s, 4))
```

```python
import functools
import math

import jax
import jax.numpy as jnp
import numpy as np
from jax import lax
from jax.experimental import pallas as pl
from jax.experimental.pallas import tpu as pltpu

F32 = jnp.float32
BF16 = jnp.bfloat16

D_MODEL = 2048
CHUNK = 64
HEAD_DIM = 128
N_HEADS = 8
N_KV_HEADS = 2
GROUP = N_HEADS // N_KV_HEADS
ATT_WIDTH = N_HEADS * HEAD_DIM
KV_WIDTH = N_KV_HEADS * HEAD_DIM
IDX_HEADS = 16
IDX_DIM = 64
IDX_WIDTH = IDX_HEADS * IDX_DIM
TOPK_MAX = 256
S5_CH = 16
S5_WIDTH = D_MODEL // 2
S5_GROUPS = S5_WIDTH // S5_CH
S5_STATE = 64
S5_LANES = S5_GROUPS * S5_STATE
D_FF = 5632
ROPE_THETA = 10000.0
EPS = 1e-6
IN_SIZES = (ATT_WIDTH, KV_WIDTH, KV_WIDTH, IDX_WIDTH, IDX_DIM, IDX_HEADS, S5_WIDTH, 2 * D_MODEL)
IN_OFFS = tuple(int(s) for s in np.cumsum((0,) + IN_SIZES))

LANES = 128
S5_SLAB_GROUPS = LANES // S5_CH
S5_SLABS = S5_GROUPS // S5_SLAB_GROUPS
S5_SLAB_STATE = S5_SLAB_GROUPS * S5_STATE
INT_MIN = -2 ** 31
DSA_GROUP_BLOCKS = 4
QROWS = 2 * CHUNK
S5_STEP_ROWS = 512
SEARCH_PART = 128

VMEM_LIMIT = 56 * 2 ** 20


def _cparams(*sem):
    return pltpu.CompilerParams(dimension_semantics=sem, vmem_limit_bytes=VMEM_LIMIT)


def _const_spec(shape):
    nd = len(shape)
    return pl.BlockSpec(shape, lambda *_: (0,) * nd, pipeline_mode=pl.Buffered(1))


_PQ = 0
_PK = _PQ + ATT_WIDTH
_PV = _PK + KV_WIDTH
_PIQ = _PV + KV_WIDTH
_PEND = _PIQ + IDX_WIDTH
IW_SCALE = (IDX_DIM ** -0.5) * (IDX_HEADS ** -0.5)


def _rope128(z, cos, sin):
    return z * cos + pltpu.roll(z, HEAD_DIM // 2, 1) * sin


def _rope64(z, cos, sin, low_half):
    partner = jnp.where(low_half, pltpu.roll(z, LANES - IDX_DIM // 2, 1), pltpu.roll(z, IDX_DIM // 2, 1))
    return z * cos + partner * sin


def _proj_body(x_ref, g_ref, wa_ref, wi_ref, wu_ref, cq_ref, sq_ref, ci_ref, si_ref,
               h_ref, q_ref, k_ref, v_ref, kb_ref, vb_ref, iq_ref, ik_ref, iw_ref, u_ref):
    x = x_ref[...]
    tm = x.shape[0]
    h = (x * lax.rsqrt(jnp.mean(x * x, axis=-1, keepdims=True) + EPS)) * g_ref[...]
    hb = h.astype(BF16)
    h_ref[...] = hb

    def proj(lo, hi):
        return jnp.dot(hb, wa_ref[:, lo:hi], preferred_element_type=F32)

    cq, sq, ci, si = cq_ref[...], sq_ref[...], ci_ref[...], si_ref[...]
    low_half = (lax.broadcasted_iota(jnp.int32, cq.shape, 1) & (IDX_DIM - 1)) < (IDX_DIM // 2)

    zq = proj(_PQ, _PK)
    for hd in range(N_HEADS):
        sl = slice(hd * HEAD_DIM, (hd + 1) * HEAD_DIM)
        q_ref[:, sl] = _rope128(zq[:, sl], cq, sq).astype(BF16)
    zk = proj(_PK, _PV)
    zv = proj(_PV, _PIQ)
    vb_ref[...] = zv.astype(BF16)
    for hd in range(N_KV_HEADS):
        sl = slice(hd * HEAD_DIM, (hd + 1) * HEAD_DIM)
        head_rows = pl.ds(hd, tm, stride=N_KV_HEADS)
        r = _rope128(zk[:, sl], cq, sq)
        k_ref[head_rows, :] = r
        kb_ref[:, sl] = r.astype(BF16)
        v_ref[head_rows, :] = zv[:, sl]
    ziq = proj(_PIQ, _PEND)
    for p in range(IDX_WIDTH // LANES):
        sl = slice(p * LANES, (p + 1) * LANES)
        iq_ref[:, sl] = _rope64(ziq[:, sl], ci, si, low_half).astype(BF16)
    z2 = jnp.dot(hb, wi_ref[...], preferred_element_type=F32)
    ik_ref[...] = _rope64(z2[:, :LANES], ci, si, low_half)[:, :IDX_DIM]
    iw_ref[...] = z2[:, LANES:] * IW_SCALE
    u_ref[...] = jnp.dot(hb, wu_ref[...], preferred_element_type=F32)


def _proj_call(x2d, g_mix, w_att, w_idx, w_u, tabs, tm):
    m = x2d.shape[0]
    n_tiles = m // tm
    tab_tiles = tabs[0].shape[0] // tm
    row = lambda w: pl.BlockSpec((tm, w), lambda i: (i, 0))
    tab_spec = pl.BlockSpec((tm, LANES), lambda i: (i % tab_tiles, 0))
    kv_cache = jax.ShapeDtypeStruct((m * N_KV_HEADS, HEAD_DIM), F32)
    kv_cache_spec = pl.BlockSpec((tm * N_KV_HEADS, HEAD_DIM), lambda i: (i, 0))
    out_shape = (
        jax.ShapeDtypeStruct((m, D_MODEL), BF16),
        jax.ShapeDtypeStruct((m, ATT_WIDTH), BF16),
        kv_cache,
        kv_cache,
        jax.ShapeDtypeStruct((m, KV_WIDTH), BF16),
        jax.ShapeDtypeStruct((m, KV_WIDTH), BF16),
        jax.ShapeDtypeStruct((m, IDX_WIDTH), BF16),
        jax.ShapeDtypeStruct((m, IDX_DIM), F32),
        jax.ShapeDtypeStruct((m, LANES), F32),
        jax.ShapeDtypeStruct((m, S5_WIDTH), F32),
    )
    out_specs = (row(D_MODEL), row(ATT_WIDTH), kv_cache_spec, kv_cache_spec, row(KV_WIDTH), row(KV_WIDTH),
                 row(IDX_WIDTH), row(IDX_DIM), row(LANES), row(S5_WIDTH))
    return pl.pallas_call(
        _proj_body,
        grid=(n_tiles,),
        in_specs=[row(D_MODEL), _const_spec((1, D_MODEL)),
                  _const_spec(w_att.shape), _const_spec(w_idx.shape), _const_spec(w_u.shape),
                  tab_spec, tab_spec, tab_spec, tab_spec],
        out_specs=out_specs,
        out_shape=out_shape,
        compiler_params=_cparams("parallel"),
        name="proj",
    )(x2d, g_mix, w_att, w_idx, w_u, *tabs)


def _gates_body(h_ref, w_ref, o_ref):
    o_ref[...] = jax.nn.sigmoid(jnp.dot(h_ref[...], w_ref[...], preferred_element_type=F32))


def _gates_call(h, w_gl, tm):
    m, n = h.shape[0], w_gl.shape[1]
    return pl.pallas_call(
        _gates_body,
        grid=(m // tm,),
        in_specs=[pl.BlockSpec((tm, D_MODEL), lambda i: (i, 0)), _const_spec(w_gl.shape)],
        out_specs=pl.BlockSpec((tm, n), lambda i: (i, 0)),
        out_shape=jax.ShapeDtypeStruct((m, n), F32),
        compiler_params=_cparams("parallel"),
        name="gates",
    )(h, w_gl)


LOG2_E = math.log2(math.e)
NEG_INF_KEY = INT_MIN + 0x7FFFFF
F32_MAX = float(np.finfo(np.float32).max)


def _key_to_f32(key):
    return lax.bitcast_convert_type(key ^ ((key >> 31) & jnp.int32(0x7FFFFFFF)), F32)


def _row_count(mask):
    return jnp.sum(jnp.where(mask, 1.0, 0.0), axis=1, keepdims=True)


def _resolve_threshold_ties(score, t_lo, t_next, excess, kf, s_keys, bias_ref, cand_ref, pick_ref, rem_ref):
    tied_row = excess > 0.0
    ge = score >= t_lo
    above = score >= t_next
    cand_ref[...] = jnp.where(ge & jnp.logical_not(above) & tied_row, 1.0, 0.0)
    pick_ref[...] = jnp.zeros_like(score)
    rem0 = jnp.where(tied_row, kf - _row_count(above), 0.0)
    rem_ref[...] = jnp.broadcast_to(rem0, rem_ref.shape)
    col = lax.broadcasted_iota(jnp.int32, score.shape, 1)

    def take_next_value(_):
        cand = cand_ref[...] > 0.5
        rem = rem_ref[:, :1]
        top = jnp.max(jnp.where(cand, score, -jnp.inf), axis=1, keepdims=True)
        eq = cand & (score == top)
        last = jnp.zeros((score.shape[0], 1), jnp.int32)
        for b in range(int(s_keys).bit_length() - 1, -1, -1):
            nxt = last + jnp.int32(1 << b)
            last = jnp.where(_row_count(eq & (col < nxt)) < rem, nxt, last)
        take = eq & (col <= last) & (rem > 0.0)
        pick_ref[...] = jnp.where(take, 1.0, pick_ref[...])
        cand_ref[...] = jnp.where(eq, 0.0, cand_ref[...])
        rem = jnp.where(top > -jnp.inf, rem - _row_count(take), 0.0)
        rem_ref[...] = jnp.broadcast_to(rem, rem_ref.shape)
        return jnp.max(rem) > 0.0

    lax.while_loop(lambda go: go, take_next_value, jnp.max(rem0) > 0.0)
    chosen = above | (pick_ref[...] > 0.5)
    bias_ref[...] = jnp.where(tied_row, jnp.where(chosen, 0.0, -jnp.inf), bias_ref[...])


def _dsa_core(q_ref, iq_ref, iw_ref, get_k, get_v, get_ikbd, n_keysets,
              o_ref, score_ref, bias_ref, cand_ref, pick_ref, rem_ref, *,
              s_keys, s_chunk, blk0, fixed_valid, topk):
    qrows = q_ref.shape[0]
    halves = [(slice(hf * CHUNK, (hf + 1) * CHUNK), hf % n_keysets) for hf in range(qrows // CHUNK)]

    pairs = IDX_WIDTH // LANES
    for hf, (rows, ks) in enumerate(halves):
        if fixed_valid is None:
            valid = (blk0 + len(halves) * pl.program_id(1) + hf + 1) * CHUNK
        else:
            valid = fixed_valid
        iq = iq_ref[rows, :]
        lhs = jnp.concatenate([iq[:, p * LANES:(p + 1) * LANES] for p in range(pairs)], axis=0)
        iw = iw_ref[rows, :]
        for c in range(s_keys // s_chunk):
            logits = lax.dot_general(lhs, get_ikbd(ks, c), (((1,), (1,)), ((), ())), preferred_element_type=F32)
            acc = jnp.zeros((CHUNK, s_chunk), F32)
            for p in range(pairs):
                lp = logits[p * CHUNK:(p + 1) * CHUNK]
                acc = acc + jnp.maximum(lp[:, :s_chunk], 0.0) * iw[:, 2 * p:2 * p + 1]
                acc = acc + jnp.maximum(lp[:, s_chunk:], 0.0) * iw[:, 2 * p + 1:2 * p + 2]
            col = c * s_chunk + lax.broadcasted_iota(jnp.int32, acc.shape, 1)
            score_ref[rows, c * s_chunk:(c + 1) * s_chunk] = jnp.where(col < valid, acc, -jnp.inf)

    score = score_ref[...]
    if s_keys <= topk:
        bias_ref[...] = jnp.where(score > -jnp.inf, 0.0, -jnp.inf)
    else:
        kf = float(topk)
        score_t = score.T

        def count_ge(t):
            ind = jnp.where(score_t >= t, 1.0, 0.0).reshape(s_keys // SEARCH_PART, SEARCH_PART, qrows)
            return jnp.sum(jnp.sum(ind, axis=0), axis=0, keepdims=True)

        thr = jnp.full((1, qrows), INT_MIN, jnp.int32)
        for b in range(31, -1, -1):
            cand = thr + jnp.int32(INT_MIN if b == 31 else 1 << b)
            thr = jnp.where(count_ge(_key_to_f32(cand)) >= kf, cand, thr)
        thr = jnp.maximum(thr, jnp.int32(NEG_INF_KEY))
        t_lo = jnp.maximum(_key_to_f32(thr), -F32_MAX)
        excess = count_ge(t_lo) - kf
        stats = jnp.concatenate([t_lo, _key_to_f32(thr + 1), excess, jnp.zeros((qrows - 3, qrows), F32)], axis=0).T
        t_lo_col = stats[:, 0:1]
        bias_ref[...] = jnp.where(score >= t_lo_col, 0.0, -jnp.inf)

        @pl.when(jnp.max(excess) > 0.0)
        def _():
            _resolve_threshold_ties(score, t_lo_col, stats[:, 1:2], stats[:, 2:3], kf, s_keys,
                                    bias_ref, cand_ref, pick_ref, rem_ref)

    scale = HEAD_DIM ** -0.5
    for rows, ks in halves:
        bias = bias_ref[rows, :]
        q = q_ref[rows, :]
        for c in range(N_KV_HEADS):
            kc = get_k(ks, c)
            vc = get_v(ks, c)
            qc = jnp.concatenate(
                [q[:, (c * GROUP + g) * HEAD_DIM:(c * GROUP + g + 1) * HEAD_DIM] for g in range(GROUP)], axis=0)
            logits = lax.dot_general(qc, kc, (((1,), (1,)), ((), ())), preferred_element_type=F32)
            es, inv = [], []
            for g in range(GROUP):
                lg = logits[g * CHUNK:(g + 1) * CHUNK] + bias
                e = jnp.exp2((lg - jnp.max(lg, axis=1, keepdims=True)) * (scale * LOG2_E))
                inv.append(1.0 / jnp.sum(e, axis=1, keepdims=True))
                es.append(e.astype(BF16))
            oc = jnp.dot(jnp.concatenate(es, axis=0), vc, preferred_element_type=F32)
            for g in range(GROUP):
                hd = c * GROUP + g
                o_ref[rows, hd * HEAD_DIM:(hd + 1) * HEAD_DIM] = (
                    oc[g * CHUNK:(g + 1) * CHUNK] * inv[g]).astype(BF16)


def _dsa_body(q_ref, iq_ref, iw_ref, kb_ref, vb_ref, ikbd_ref, *rest, **static):
    head = lambda ref: (lambda ks, c: ref[ks, :, c * HEAD_DIM:(c + 1) * HEAD_DIM])
    _dsa_core(q_ref, iq_ref, iw_ref, head(kb_ref), head(vb_ref), lambda ks, c: ikbd_ref[ks, c],
              kb_ref.shape[0], *rest[-6:], **static)


def _dsa_cached_body(q_ref, iq_ref, iw_ref, kn_ref, vn_ref, ikn_ref, pk_ref, pv_ref, pik_ref,
                     o_ref, k_all, v_all, ikbd_all, *scratch, past_len, **static):
    s_keys = static['s_keys']
    new_end = past_len + CHUNK
    for ks in range(2):
        new = slice(ks * CHUNK, (ks + 1) * CHUNK)
        for c in range(N_KV_HEADS):
            for dst, past, fresh in ((k_all, pk_ref, kn_ref), (v_all, pv_ref, vn_ref)):
                dst[ks, c, :past_len, :] = past[ks, pl.ds(c, past_len, stride=N_KV_HEADS), :].astype(BF16)
                dst[ks, c, past_len:new_end, :] = fresh[new, c * HEAD_DIM:(c + 1) * HEAD_DIM]
                dst[ks, c, new_end:, :] = jnp.zeros((s_keys - new_end, HEAD_DIM), BF16)
        ik = jnp.concatenate([pik_ref[ks].astype(BF16), ikn_ref[new, :].astype(BF16),
                              jnp.zeros((s_keys - new_end, IDX_DIM), BF16)], axis=0)
        z = jnp.zeros_like(ik)
        ikbd_all[ks, :s_keys, :] = jnp.concatenate([ik, z], axis=1)
        ikbd_all[ks, s_keys:, :] = jnp.concatenate([z, ik], axis=1)
    _dsa_core(q_ref, iq_ref, iw_ref, lambda ks, c: k_all[ks, c], lambda ks, c: v_all[ks, c],
              lambda ks, c: ikbd_all[ks], 2, o_ref, *scratch, **static)


def _dsa_cached_call(q, iq, iw, kb, vb, ik, pk, pv, pik, topk):
    nb, past_len = pik.shape[0], pik.shape[1]
    l_keys = past_len + CHUNK
    s_keys = -(-l_keys // LANES) * LANES
    body = functools.partial(_dsa_cached_body, past_len=past_len, s_keys=s_keys, s_chunk=s_keys, blk0=0,
                             fixed_valid=l_keys, topk=topk)
    qrow = lambda w: pl.BlockSpec((QROWS, w), lambda n: (n, 0))
    pair = lambda a: pl.BlockSpec((2,) + a.shape[1:], lambda n: (n, 0, 0))
    mask_buf = pltpu.VMEM((QROWS, s_keys), F32)
    heads = pltpu.VMEM((2, N_KV_HEADS, s_keys, HEAD_DIM), BF16)
    return pl.pallas_call(
        body,
        grid=(nb // 2,),
        in_specs=[qrow(ATT_WIDTH), qrow(IDX_WIDTH), qrow(LANES), qrow(KV_WIDTH), qrow(KV_WIDTH), qrow(IDX_DIM),
                  pair(pk), pair(pv), pair(pik)],
        out_specs=qrow(ATT_WIDTH),
        out_shape=jax.ShapeDtypeStruct((nb * CHUNK, ATT_WIDTH), BF16),
        scratch_shapes=[heads, heads, pltpu.VMEM((2, 2 * s_keys, LANES), BF16),
                        mask_buf, mask_buf, mask_buf, mask_buf, pltpu.VMEM((QROWS, LANES), F32)],
        compiler_params=_cparams("parallel"),
        name="dsa_cached",
    )(q, iq, iw, kb, vb, ik, pk, pv, pik)


def _dsa_call(q, iq, iw, kb, vb, ikbd, o_prev, qrows, blk0, n_blk, blks_per_seq, s_keys, s_chunk, topk):
    nb = kb.shape[0]
    blocks_per_step = qrows // CHUNK
    steps_per_seq = blks_per_seq // blocks_per_step
    qrow = lambda w: pl.BlockSpec(
        (qrows, w), lambda n, j: (n * steps_per_seq + blk0 // blocks_per_step + j, 0))
    n_chunks = s_keys // s_chunk
    body = functools.partial(_dsa_body, s_keys=s_keys, s_chunk=s_chunk, blk0=blk0, fixed_valid=None, topk=topk)
    mask_buf = pltpu.VMEM((qrows, s_keys), F32)
    return pl.pallas_call(
        body,
        grid=(nb, n_blk // blocks_per_step),
        in_specs=[qrow(ATT_WIDTH), qrow(IDX_WIDTH), qrow(LANES),
                  pl.BlockSpec((1, s_keys, KV_WIDTH), lambda n, j: (n, 0, 0)),
                  pl.BlockSpec((1, s_keys, KV_WIDTH), lambda n, j: (n, 0, 0)),
                  pl.BlockSpec((1, n_chunks, 2 * s_chunk, LANES), lambda n, j: (n, 0, 0, 0)),
                  pl.BlockSpec(memory_space=pl.ANY)],
        out_specs=qrow(ATT_WIDTH),
        out_shape=jax.ShapeDtypeStruct((nb * blks_per_seq * CHUNK, ATT_WIDTH), BF16),
        scratch_shapes=[mask_buf, mask_buf, mask_buf, mask_buf, pltpu.VMEM((qrows, LANES), F32)],
        input_output_aliases={6: 0},
        compiler_params=_cparams("parallel", "arbitrary"),
        name="dsa",
    )(q, iq, iw, kb, vb, ikbd, o_prev)


def _indexer_key_blocks(ik, s_chunk):
    nb, s, _ = ik.shape
    ikb = ik.astype(BF16).reshape(nb, s // s_chunk, s_chunk, IDX_DIM)
    z = jnp.zeros_like(ikb)
    return jnp.concatenate([jnp.concatenate([ikb, z], axis=-1), jnp.concatenate([z, ikb], axis=-1)], axis=-2)


def _s5prep_body(are_ref, aim_ref, ldt_ref, bre_ref, bim_ref, abre_ref, abim_ref, bbre_ref, bbim_ref):
    a_re, a_im = are_ref[...], aim_ref[...]
    dt = jnp.exp(ldt_ref[...])
    mag = jnp.exp(dt * a_re)
    ab_re = mag * jnp.cos(dt * a_im)
    ab_im = mag * jnp.sin(dt * a_im)
    den = a_re * a_re + a_im * a_im
    f_re = ((ab_re - 1.0) * a_re + ab_im * a_im) / den
    f_im = (ab_im * a_re - (ab_re - 1.0) * a_im) / den
    abre_ref[...] = ab_re
    abim_ref[...] = ab_im
    for c in range(S5_CH):
        b_re, b_im = bre_ref[c], bim_ref[c]
        bbre_ref[c] = f_re * b_re - f_im * b_im
        bbim_ref[c] = f_re * b_im + f_im * b_re


def _s5prep_call(a_re, a_im, log_dt, b_re, b_im):
    rows = S5_LANES // LANES
    flat = lambda a: a.reshape(rows, LANES)
    ldt = jnp.broadcast_to(log_dt[:, None], (S5_GROUPS, S5_STATE))
    chan_major = lambda b: jnp.transpose(b, (2, 0, 1)).reshape(S5_CH, rows, LANES)
    small = jax.ShapeDtypeStruct((rows, LANES), F32)
    big = jax.ShapeDtypeStruct((S5_CH, rows, LANES), F32)
    return pl.pallas_call(
        _s5prep_body,
        out_shape=(small, small, big, big),
        name="s5prep",
    )(flat(a_re), flat(a_im), flat(ldt), chan_major(b_re), chan_major(b_im))


def _gelu_tanh(x):
    return 0.5 * x * (1.0 + jnp.tanh(math.sqrt(2.0 / math.pi) * (x + 0.044715 * (x * x * x))))


def _s5_body(u_ref, x0re_ref, x0im_ref, abre_ref, abim_ref, bre_ref, bim_ref, cre_ref, cim_ref,
             dskip_ref, wglu_ref, bglu_ref, ob_ref, sre_ref, sim_ref, utm, otm, xre, xim, st_re, st_im,
             *, tc, nb, lane_w):
    step = pl.program_id(0)

    @pl.when(step == 0)
    def _():
        st_re[...] = x0re_ref[...]
        st_im[...] = x0im_ref[...]

    for n in range(nb):
        for k in range(S5_SLABS):
            utm[k, pl.ds(n, tc, stride=nb), :] = u_ref[n, :, k * LANES:(k + 1) * LANES]

    for k in range(S5_SLABS):
        us = utm[k].astype(BF16)
        sl = slice(k * S5_SLAB_STATE, (k + 1) * S5_SLAB_STATE)
        xre[:, :, sl] = jnp.dot(us, bre_ref[k], preferred_element_type=F32).reshape(tc, nb, S5_SLAB_STATE)
        xim[:, :, sl] = jnp.dot(us, bim_ref[k], preferred_element_type=F32).reshape(tc, nb, S5_SLAB_STATE)

    for lc in range(S5_LANES // lane_w):
        sl = slice(lc * lane_w, (lc + 1) * lane_w)
        a_r = jnp.broadcast_to(abre_ref[:, sl], (nb, lane_w))
        a_i = jnp.broadcast_to(abim_ref[:, sl], (nb, lane_w))

        def scan_step(t, carry, sl=sl, a_r=a_r, a_i=a_i):
            s_r, s_i = carry
            n_r = a_r * s_r - a_i * s_i + xre[t, :, sl]
            n_i = a_r * s_i + a_i * s_r + xim[t, :, sl]
            xre[t, :, sl] = n_r
            xim[t, :, sl] = n_i
            return n_r, n_i

        s_r, s_i = lax.fori_loop(0, tc, scan_step, (st_re[:, sl], st_im[:, sl]), unroll=8)
        st_re[:, sl] = s_r
        st_im[:, sl] = s_i

    ys = []
    for k in range(S5_SLABS):
        sl = slice(k * S5_SLAB_STATE, (k + 1) * S5_SLAB_STATE)
        xr = xre[:, :, sl].reshape(tc * nb, S5_SLAB_STATE).astype(BF16)
        xi = xim[:, :, sl].reshape(tc * nb, S5_SLAB_STATE).astype(BF16)
        ys.append(jnp.dot(xr, cre_ref[k], preferred_element_type=F32)
                  - jnp.dot(xi, cim_ref[k], preferred_element_type=F32)
                  + dskip_ref[:, k * LANES:(k + 1) * LANES] * utm[k])
    yb = _gelu_tanh(jnp.concatenate(ys, axis=1))
    gate = jax.nn.sigmoid(jnp.dot(yb.astype(BF16), wglu_ref[...], preferred_element_type=F32) + bglu_ref[...])
    o = yb * gate
    for k in range(S5_SLABS):
        otm[k] = o[:, k * LANES:(k + 1) * LANES]
    for n in range(nb):
        for k in range(S5_SLABS):
            ob_ref[n, :, k * LANES:(k + 1) * LANES] = otm[k, pl.ds(n, tc, stride=nb), :].astype(BF16)

    @pl.when(step == pl.num_programs(0) - 1)
    def _():
        sre_ref[...] = st_re[...]
        sim_ref[...] = st_im[...]


def _s5_call(u, x0_re, x0_im, ab_re, ab_im, b_re_bd, b_im_bd, c_re_bd, c_im_bd, d_skip, w_glu, b_glu, tc):
    nb, t_len, _ = u.shape
    lane_w = 1024 if nb <= 8 else 512
    body = functools.partial(_s5_body, tc=tc, nb=nb, lane_w=lane_w)
    state = jax.ShapeDtypeStruct((nb, S5_LANES), F32)
    seq_spec = pl.BlockSpec((nb, tc, S5_WIDTH), lambda i: (0, i, 0))
    slabs = pltpu.VMEM((S5_SLABS, tc * nb, LANES), F32)
    return pl.pallas_call(
        body,
        grid=(t_len // tc,),
        in_specs=[seq_spec,
                  _const_spec((nb, S5_LANES)), _const_spec((nb, S5_LANES)),
                  _const_spec((1, S5_LANES)), _const_spec((1, S5_LANES)),
                  _const_spec(b_re_bd.shape), _const_spec(b_im_bd.shape),
                  _const_spec(c_re_bd.shape), _const_spec(c_im_bd.shape),
                  _const_spec((1, S5_WIDTH)), _const_spec((S5_WIDTH, S5_WIDTH)), _const_spec((1, S5_WIDTH))],
        out_specs=(seq_spec, _const_spec((nb, S5_LANES)), _const_spec((nb, S5_LANES))),
        out_shape=(jax.ShapeDtypeStruct((nb, t_len, S5_WIDTH), BF16), state, state),
        scratch_shapes=[slabs, slabs,
                        pltpu.VMEM((tc, nb, S5_LANES), F32), pltpu.VMEM((tc, nb, S5_LANES), F32),
                        pltpu.VMEM((nb, S5_LANES), F32), pltpu.VMEM((nb, S5_LANES), F32)],
        compiler_params=_cparams("arbitrary"),
        name="s5",
    )(u, x0_re, x0_im, ab_re, ab_im, b_re_bd, b_im_bd, c_re_bd, c_im_bd, d_skip, w_glu, b_glu)


def _block_diag_slabs(w):
    g, r, c = w.shape
    w = w.reshape(S5_SLABS, S5_SLAB_GROUPS, r, c)
    eye = jnp.eye(S5_SLAB_GROUPS, dtype=w.dtype)
    bd = w[:, :, :, None, :] * eye[None, :, None, :, None]
    return bd.reshape(S5_SLABS, S5_SLAB_GROUPS * r, S5_SLAB_GROUPS * c)


def _merge_body(oa_ref, ob_ref, ga_ref, gb_ref, x_ref, wa_ref, wb_ref, wo_ref, g_ref, x1_ref, hf_ref):
    pa = jnp.dot(oa_ref[...], wa_ref[...], preferred_element_type=F32)
    pb = jnp.dot(ob_ref[...], wb_ref[...], preferred_element_type=F32)
    merged = ga_ref[...] * pa + gb_ref[...] * pb
    x1 = x_ref[...] + jnp.dot(merged.astype(BF16), wo_ref[...], preferred_element_type=F32)
    x1_ref[...] = x1
    hf = (x1 * lax.rsqrt(jnp.mean(x1 * x1, axis=-1, keepdims=True) + EPS)) * g_ref[...]
    hf_ref[...] = hf.astype(BF16)


def _merge_call(oa, ob, gates, x2d, w_a, w_b, w_o, g_ffn, tm):
    m = x2d.shape[0]
    row = lambda w: pl.BlockSpec((tm, w), lambda i: (i, 0))
    return pl.pallas_call(
        _merge_body,
        grid=(m // tm,),
        in_specs=[row(ATT_WIDTH), row(S5_WIDTH),
                  pl.BlockSpec((tm, D_MODEL), lambda i: (i, 0)), pl.BlockSpec((tm, D_MODEL), lambda i: (i, 1)),
                  row(D_MODEL), _const_spec(w_a.shape), _const_spec(w_b.shape), _const_spec(w_o.shape),
                  _const_spec((1, D_MODEL))],
        out_specs=(row(D_MODEL), row(D_MODEL)),
        out_shape=(jax.ShapeDtypeStruct((m, D_MODEL), F32), jax.ShapeDtypeStruct((m, D_MODEL), BF16)),
        compiler_params=_cparams("parallel"),
        name="merge",
    )(oa, ob, gates, gates, x2d, w_a, w_b, w_o, g_ffn)


FFN_OUT_CHUNK = 512


def _ffn_body(hf_ref, x1_hbm, wg_ref, wu_ref, wd_ref, g_ref, y_ref, x1_sem, *, final_norm, tm):
    i, f = pl.program_id(0), pl.program_id(1)
    x1_copy = pltpu.make_async_copy(x1_hbm.at[pl.ds(i * tm, tm), :], y_ref, x1_sem)

    @pl.when(f == 0)
    def _():
        x1_copy.start()

    hf = hf_ref[...]
    a = jax.nn.silu(jnp.dot(hf, wg_ref[...], preferred_element_type=F32)) * jnp.dot(
        hf, wu_ref[...], preferred_element_type=F32)
    ab = a.astype(BF16)

    @pl.when(f == 0)
    def _():
        x1_copy.wait()

    for c0 in range(0, D_MODEL, FFN_OUT_CHUNK):
        cols = slice(c0, c0 + FFN_OUT_CHUNK)
        y_ref[:, cols] += jnp.dot(ab, wd_ref[:, cols], preferred_element_type=F32)

    if final_norm:
        @pl.when(f == pl.num_programs(1) - 1)
        def _():
            x2 = y_ref[...]
            y_ref[...] = (x2 * lax.rsqrt(jnp.mean(x2 * x2, axis=-1, keepdims=True) + EPS)) * g_ref[...]


def _ffn_call(hf, x1, w_gate, w_up, w_down, g_final, tm, tf, final_norm):
    m = hf.shape[0]
    return pl.pallas_call(
        functools.partial(_ffn_body, final_norm=final_norm, tm=tm),
        grid=(m // tm, D_FF // tf),
        in_specs=[pl.BlockSpec((tm, D_MODEL), lambda i, f: (i, 0)),
                  pl.BlockSpec(memory_space=pl.ANY),
                  pl.BlockSpec((D_MODEL, tf), lambda i, f: (0, f)),
                  pl.BlockSpec((D_MODEL, tf), lambda i, f: (0, f)),
                  pl.BlockSpec((tf, D_MODEL), lambda i, f: (f, 0)),
                  _const_spec((1, D_MODEL))],
        out_specs=pl.BlockSpec((tm, D_MODEL), lambda i, f: (i, 0)),
        out_shape=jax.ShapeDtypeStruct((m, D_MODEL), F32),
        scratch_shapes=[pltpu.SemaphoreType.DMA(())],
        compiler_params=_cparams("parallel", "arbitrary"),
        name="ffn",
    )(hf, x1, w_gate, w_up, w_down, g_final)


def _rope_tables(pos, dim):
    half = dim // 2
    inv = 1.0 / (ROPE_THETA ** (jnp.arange(half, dtype=F32) * (2.0 / dim)))
    ang = pos[:, None] * inv[None, :]
    cos, sin = jnp.cos(ang), jnp.sin(ang)
    reps = LANES // dim
    return (jnp.tile(jnp.concatenate([cos, cos], axis=-1), (1, reps)),
            jnp.tile(jnp.concatenate([-sin, sin], axis=-1), (1, reps)))


def _split_w_in(w_in):
    o = IN_OFFS
    seg = lambda i, j: w_in[:, o[i]:o[j]].astype(BF16)
    pad = lambda w: jnp.pad(w, ((0, 0), (0, LANES - w.shape[1])))
    return seg(0, 4), jnp.concatenate([pad(seg(4, 5)), pad(seg(5, 6))], axis=1), seg(6, 7), seg(7, 8)


def _layer(x, pos, past, ssm0, lw, tm_proj):
    nb, t_len, _ = x.shape
    m = nb * t_len
    x2d = x.reshape(m, D_MODEL)

    cq, sq = _rope_tables(pos, HEAD_DIM)
    ci, si = _rope_tables(pos, IDX_DIM)
    tabs = (cq, sq, ci, si)
    if t_len < tm_proj:
        tabs = tuple(jnp.tile(t, (tm_proj // t_len, 1)) for t in tabs)

    h, q, k, v, kb, vb, iq, ik, iw, u = _proj_call(
        x2d, lw['g_mix'], lw['w_att'], lw['w_idx'], lw['w_u'], tabs, tm_proj)
    gates = _gates_call(h, lw['w_gl'], tm_proj)

    if past is None:
        s_chunk = DSA_GROUP_BLOCKS * CHUNK
        kb_all = kb.reshape(nb, t_len, KV_WIDTH)
        vb_all = vb.reshape(nb, t_len, KV_WIDTH)
        ikbd = _indexer_key_blocks(ik.reshape(nb, t_len, IDX_DIM), s_chunk)
        n_blk = t_len // CHUNK
        o_a = jnp.zeros((m, ATT_WIDTH), BF16)
        for blk0 in range(0, n_blk, DSA_GROUP_BLOCKS):
            o_a = _dsa_call(q, iq, iw, kb_all, vb_all, ikbd, o_a, DSA_GROUP_BLOCKS * CHUNK, blk0, DSA_GROUP_BLOCKS,
                            n_blk, (blk0 + DSA_GROUP_BLOCKS) * CHUNK, s_chunk, min(TOPK_MAX, t_len // 4))
    else:
        pk, pv, pik = past
        assert t_len == CHUNK, "the cached stream is one query block per sequence"
        l_keys = pk.shape[1] + t_len
        o_a = _dsa_cached_call(q, iq, iw, kb, vb, ik, pk.reshape(nb, -1, HEAD_DIM), pv.reshape(nb, -1, HEAD_DIM),
                               pik, min(TOPK_MAX, l_keys // 4))

    x0_re, x0_im = ssm0
    o_b, s_re, s_im = _s5_call(
        u.reshape(nb, t_len, S5_WIDTH), x0_re.reshape(nb, S5_LANES), x0_im.reshape(nb, S5_LANES),
        lw['ab_re'], lw['ab_im'], lw['b_re_bd'], lw['b_im_bd'], lw['c_re_bd'], lw['c_im_bd'],
        lw['d_skip'], lw['w_glu'], lw['b_glu'], min(t_len, S5_STEP_ROWS // nb))

    x1, hf = _merge_call(o_a, o_b.reshape(m, S5_WIDTH), gates, x2d, lw['w_proj_a'], lw['w_proj_b'],
                         lw['w_out'], lw['g_ffn'], 256)
    caches = (k.reshape(nb, t_len, N_KV_HEADS, HEAD_DIM), v.reshape(nb, t_len, N_KV_HEADS, HEAD_DIM),
              ik.reshape(nb, t_len, IDX_DIM), s_re.reshape(nb, S5_GROUPS, S5_STATE),
              s_im.reshape(nb, S5_GROUPS, S5_STATE))
    return x1, hf, caches


def kernel(x_prompt, x_sample, cache_k, cache_v, cache_idx_k, state_ssm_re, state_ssm_im,
           g_mix, w_in, a_re, a_im, log_dt, b_re, b_im, c_re, c_im, d_skip, w_glu, b_glu,
           w_proj_a, w_proj_b, w_out, g_ffn, w_gate, w_up, w_down, g_final):
    depth = w_in.shape[0]
    t_p, t_s = x_prompt.shape[1], x_sample.shape[1]
    past_len = cache_k.shape[2]
    pos_p = jnp.arange(t_p, dtype=F32)
    pos_s = past_len + jnp.arange(t_s, dtype=F32)
    nb_p, nb_s = x_prompt.shape[0], x_sample.shape[0]
    g_fin = g_final.reshape(1, D_MODEL)

    hp, hs = x_prompt, x_sample
    outs_p, outs_s = [], []
    for l in range(depth):
        w_att, w_idx, w_u, w_gl = _split_w_in(w_in[l])
        ab_re, ab_im, bb_re, bb_im = _s5prep_call(a_re[l], a_im[l], log_dt[l], b_re[l], b_im[l])
        per_group = lambda bb: jnp.transpose(bb.reshape(S5_CH, S5_GROUPS, S5_STATE), (1, 0, 2))
        lw = {
            'g_mix': g_mix[l].reshape(1, D_MODEL), 'w_att': w_att, 'w_idx': w_idx, 'w_u': w_u, 'w_gl': w_gl,
            'ab_re': ab_re.reshape(1, S5_LANES), 'ab_im': ab_im.reshape(1, S5_LANES),
            'b_re_bd': _block_diag_slabs(per_group(bb_re)).astype(BF16),
            'b_im_bd': _block_diag_slabs(per_group(bb_im)).astype(BF16),
            'c_re_bd': _block_diag_slabs(jnp.transpose(c_re[l], (0, 2, 1))).astype(BF16),
            'c_im_bd': _block_diag_slabs(jnp.transpose(c_im[l], (0, 2, 1))).astype(BF16),
            'd_skip': d_skip[l].reshape(1, S5_WIDTH), 'w_glu': w_glu[l].astype(BF16),
            'b_glu': b_glu[l].reshape(1, S5_WIDTH),
            'w_proj_a': w_proj_a[l].astype(BF16), 'w_proj_b': w_proj_b[l].astype(BF16),
            'w_out': w_out[l].astype(BF16), 'g_ffn': g_ffn[l].reshape(1, D_MODEL),
        }
        wg, wu, wd = w_gate[l].astype(BF16), w_up[l].astype(BF16), w_down[l].astype(BF16)
        last = l == depth - 1

        zeros = jnp.zeros((nb_p, S5_GROUPS, S5_STATE), F32)
        x1p, hfp, cp = _layer(hp, pos_p, None, (zeros, zeros), lw, 512)
        x1s, hfs, cs = _layer(hs, pos_s, (cache_k[l], cache_v[l], cache_idx_k[l]),
                              (state_ssm_re[l], state_ssm_im[l]), lw, 512)
        yp = _ffn_call(hfp, x1p, wg, wu, wd, g_fin, 1024, 512, last)
        ys = _ffn_call(hfs, x1s, wg, wu, wd, g_fin, 1024, 512, last)
        hp = yp.reshape(nb_p, t_p, D_MODEL)
        hs = ys.reshape(nb_s, t_s, D_MODEL)
        outs_p.append(cp)
        outs_s.append(cs)

    stack = lambda outs, i: jnp.stack([o[i] for o in outs])
    return (hp, hs,
            stack(outs_p, 0), stack(outs_p, 1), stack(outs_p, 2), stack(outs_p, 3), stack(outs_p, 4),
            stack(outs_s, 0), stack(outs_s, 1), stack(outs_s, 2), stack(outs_s, 3), stack(outs_s, 4))
```

```python
import functools
import math

import jax
import jax.numpy as jnp
import numpy as np
from jax import lax
from jax.experimental import pallas as pl
from jax.experimental.pallas import tpu as pltpu

F32 = jnp.float32
BF16 = jnp.bfloat16

D_MODEL = 2048
CHUNK = 64
HEAD_DIM = 128
N_HEADS = 8
N_KV_HEADS = 2
GROUP = N_HEADS // N_KV_HEADS
ATT_WIDTH = N_HEADS * HEAD_DIM
KV_WIDTH = N_KV_HEADS * HEAD_DIM
IDX_HEADS = 16
IDX_DIM = 64
IDX_WIDTH = IDX_HEADS * IDX_DIM
TOPK_MAX = 256
S5_CH = 16
S5_WIDTH = D_MODEL // 2
S5_GROUPS = S5_WIDTH // S5_CH
S5_STATE = 64
S5_LANES = S5_GROUPS * S5_STATE
D_FF = 5632
ROPE_THETA = 10000.0
EPS = 1e-6
IN_SIZES = (ATT_WIDTH, KV_WIDTH, KV_WIDTH, IDX_WIDTH, IDX_DIM, IDX_HEADS, S5_WIDTH, 2 * D_MODEL)
IN_OFFS = tuple(int(s) for s in np.cumsum((0,) + IN_SIZES))

LANES = 128
S5_SLAB_GROUPS = LANES // S5_CH
S5_SLABS = S5_GROUPS // S5_SLAB_GROUPS
S5_SLAB_STATE = S5_SLAB_GROUPS * S5_STATE
INT_MIN = -2 ** 31
DSA_GROUP_BLOCKS = 4
QROWS = 2 * CHUNK
DSA_WIDE_STEP_MAX_KEYS = 1280
S5_STEP_ROWS = 512
SEARCH_PART = 128

VMEM_LIMIT = 56 * 2 ** 20


def _cparams(*sem):
    return pltpu.CompilerParams(dimension_semantics=sem, vmem_limit_bytes=VMEM_LIMIT)


def _const_spec(shape):
    nd = len(shape)
    return pl.BlockSpec(shape, lambda *_: (0,) * nd, pipeline_mode=pl.Buffered(1))


_PQ = 0
_PK = _PQ + ATT_WIDTH
_PV = _PK + KV_WIDTH
_PIQ = _PV + KV_WIDTH
_PEND = _PIQ + IDX_WIDTH
IW_SCALE = (IDX_DIM ** -0.5) * (IDX_HEADS ** -0.5)


def _rope128(z, cos, sin):
    return z * cos + pltpu.roll(z, HEAD_DIM // 2, 1) * sin


def _rope64(z, cos, sin, low_half):
    partner = jnp.where(low_half, pltpu.roll(z, LANES - IDX_DIM // 2, 1), pltpu.roll(z, IDX_DIM // 2, 1))
    return z * cos + partner * sin


def _proj_body(x_ref, g_ref, wa_ref, wi_ref, wu_ref, cq_ref, sq_ref, ci_ref, si_ref,
               h_ref, q_ref, k_ref, v_ref, kb_ref, vb_ref, iq_ref, ik_ref, iw_ref, u_ref):
    x = x_ref[...]
    tm = x.shape[0]
    h = (x * lax.rsqrt(jnp.mean(x * x, axis=-1, keepdims=True) + EPS)) * g_ref[...]
    hb = h.astype(BF16)
    h_ref[...] = hb

    def proj(lo, hi):
        return jnp.dot(hb, wa_ref[:, lo:hi], preferred_element_type=F32)

    cq, sq, ci, si = cq_ref[...], sq_ref[...], ci_ref[...], si_ref[...]
    low_half = (lax.broadcasted_iota(jnp.int32, cq.shape, 1) & (IDX_DIM - 1)) < (IDX_DIM // 2)

    zq = proj(_PQ, _PK)
    for hd in range(N_HEADS):
        sl = slice(hd * HEAD_DIM, (hd + 1) * HEAD_DIM)
        q_ref[:, sl] = _rope128(zq[:, sl], cq, sq).astype(BF16)
    zk = proj(_PK, _PV)
    zv = proj(_PV, _PIQ)
    vb_ref[...] = zv.astype(BF16)
    for hd in range(N_KV_HEADS):
        sl = slice(hd * HEAD_DIM, (hd + 1) * HEAD_DIM)
        head_rows = pl.ds(hd, tm, stride=N_KV_HEADS)
        r = _rope128(zk[:, sl], cq, sq)
        k_ref[head_rows, :] = r
        kb_ref[:, sl] = r.astype(BF16)
        v_ref[head_rows, :] = zv[:, sl]
    ziq = proj(_PIQ, _PEND)
    for p in range(IDX_WIDTH // LANES):
        sl = slice(p * LANES, (p + 1) * LANES)
        iq_ref[:, sl] = _rope64(ziq[:, sl], ci, si, low_half).astype(BF16)
    z2 = jnp.dot(hb, wi_ref[...], preferred_element_type=F32)
    ik_ref[...] = _rope64(z2[:, :LANES], ci, si, low_half)[:, :IDX_DIM]
    iw_ref[...] = z2[:, LANES:] * IW_SCALE
    u_ref[...] = jnp.dot(hb, wu_ref[...], preferred_element_type=F32)


def _proj_call(x2d, g_mix, w_att, w_idx, w_u, tabs, tm):
    m = x2d.shape[0]
    n_tiles = m // tm
    tab_tiles = tabs[0].shape[0] // tm
    row = lambda w: pl.BlockSpec((tm, w), lambda i: (i, 0))
    tab_spec = pl.BlockSpec((tm, LANES), lambda i: (i % tab_tiles, 0))
    kv_cache = jax.ShapeDtypeStruct((m * N_KV_HEADS, HEAD_DIM), F32)
    kv_cache_spec = pl.BlockSpec((tm * N_KV_HEADS, HEAD_DIM), lambda i: (i, 0))
    out_shape = (
        jax.ShapeDtypeStruct((m, D_MODEL), BF16),
        jax.ShapeDtypeStruct((m, ATT_WIDTH), BF16),
        kv_cache,
        kv_cache,
        jax.ShapeDtypeStruct((m, KV_WIDTH), BF16),
        jax.ShapeDtypeStruct((m, KV_WIDTH), BF16),
        jax.ShapeDtypeStruct((m, IDX_WIDTH), BF16),
        jax.ShapeDtypeStruct((m, IDX_DIM), F32),
        jax.ShapeDtypeStruct((m, LANES), F32),
        jax.ShapeDtypeStruct((m, S5_WIDTH), F32),
    )
    out_specs = (row(D_MODEL), row(ATT_WIDTH), kv_cache_spec, kv_cache_spec, row(KV_WIDTH), row(KV_WIDTH),
                 row(IDX_WIDTH), row(IDX_DIM), row(LANES), row(S5_WIDTH))
    return pl.pallas_call(
        _proj_body,
        grid=(n_tiles,),
        in_specs=[row(D_MODEL), _const_spec((1, D_MODEL)),
                  _const_spec(w_att.shape), _const_spec(w_idx.shape), _const_spec(w_u.shape),
                  tab_spec, tab_spec, tab_spec, tab_spec],
        out_specs=out_specs,
        out_shape=out_shape,
        compiler_params=_cparams("parallel"),
        name="proj",
    )(x2d, g_mix, w_att, w_idx, w_u, *tabs)


def _gates_body(h_ref, w_ref, o_ref):
    o_ref[...] = jax.nn.sigmoid(jnp.dot(h_ref[...], w_ref[...], preferred_element_type=F32))


def _gates_call(h, w_gl, tm):
    m, n = h.shape[0], w_gl.shape[1]
    return pl.pallas_call(
        _gates_body,
        grid=(m // tm,),
        in_specs=[pl.BlockSpec((tm, D_MODEL), lambda i: (i, 0)), _const_spec(w_gl.shape)],
        out_specs=pl.BlockSpec((tm, n), lambda i: (i, 0)),
        out_shape=jax.ShapeDtypeStruct((m, n), F32),
        compiler_params=_cparams("parallel"),
        name="gates",
    )(h, w_gl)


LOG2_E = math.log2(math.e)
NEG_INF_KEY = INT_MIN + 0x7FFFFF
F32_MAX = float(np.finfo(np.float32).max)


def _key_to_f32(key):
    return lax.bitcast_convert_type(key ^ ((key >> 31) & jnp.int32(0x7FFFFFFF)), F32)


def _row_count(mask):
    return jnp.sum(jnp.where(mask, 1.0, 0.0), axis=1, keepdims=True)


def _resolve_threshold_ties(score, t_lo, t_next, excess, kf, s_keys, bias_ref, cand_ref, pick_ref, rem_ref):
    tied_row = excess > 0.0
    ge = score >= t_lo
    above = score >= t_next
    cand_ref[...] = jnp.where(ge & jnp.logical_not(above) & tied_row, 1.0, 0.0)
    pick_ref[...] = jnp.zeros_like(score)
    rem0 = jnp.where(tied_row, kf - _row_count(above), 0.0)
    rem_ref[...] = jnp.broadcast_to(rem0, rem_ref.shape)
    col = lax.broadcasted_iota(jnp.int32, score.shape, 1)

    def take_next_value(_):
        cand = cand_ref[...] > 0.5
        rem = rem_ref[:, :1]
        top = jnp.max(jnp.where(cand, score, -jnp.inf), axis=1, keepdims=True)
        eq = cand & (score == top)
        last = jnp.zeros((score.shape[0], 1), jnp.int32)
        for b in range(int(s_keys).bit_length() - 1, -1, -1):
            nxt = last + jnp.int32(1 << b)
            last = jnp.where(_row_count(eq & (col < nxt)) < rem, nxt, last)
        take = eq & (col <= last) & (rem > 0.0)
        pick_ref[...] = jnp.where(take, 1.0, pick_ref[...])
        cand_ref[...] = jnp.where(eq, 0.0, cand_ref[...])
        rem = jnp.where(top > -jnp.inf, rem - _row_count(take), 0.0)
        rem_ref[...] = jnp.broadcast_to(rem, rem_ref.shape)
        return jnp.max(rem) > 0.0

    lax.while_loop(lambda go: go, take_next_value, jnp.max(rem0) > 0.0)
    chosen = above | (pick_ref[...] > 0.5)
    bias_ref[...] = jnp.where(tied_row, jnp.where(chosen, 0.0, -jnp.inf), bias_ref[...])


def _dsa_core(q_ref, iq_ref, iw_ref, get_k, get_v, get_ikbd, n_keysets,
              o_ref, score_ref, bias_ref, cand_ref, pick_ref, rem_ref, *,
              s_keys, s_chunk, blk0, fixed_valid, topk):
    qrows = q_ref.shape[0]
    halves = [(slice(hf * CHUNK, (hf + 1) * CHUNK), hf % n_keysets) for hf in range(qrows // CHUNK)]

    pairs = IDX_WIDTH // LANES
    for hf, (rows, ks) in enumerate(halves):
        if fixed_valid is None:
            valid = (blk0 + len(halves) * pl.program_id(1) + hf + 1) * CHUNK
        else:
            valid = fixed_valid
        iq = iq_ref[rows, :]
        lhs = jnp.concatenate([iq[:, p * LANES:(p + 1) * LANES] for p in range(pairs)], axis=0)
        iw = iw_ref[rows, :]
        for c in range(s_keys // s_chunk):
            logits = lax.dot_general(lhs, get_ikbd(ks, c), (((1,), (1,)), ((), ())), preferred_element_type=F32)
            acc = jnp.zeros((CHUNK, s_chunk), F32)
            for p in range(pairs):
                lp = logits[p * CHUNK:(p + 1) * CHUNK]
                acc = acc + jnp.maximum(lp[:, :s_chunk], 0.0) * iw[:, 2 * p:2 * p + 1]
                acc = acc + jnp.maximum(lp[:, s_chunk:], 0.0) * iw[:, 2 * p + 1:2 * p + 2]
            col = c * s_chunk + lax.broadcasted_iota(jnp.int32, acc.shape, 1)
            score_ref[rows, c * s_chunk:(c + 1) * s_chunk] = jnp.where(col < valid, acc, -jnp.inf)

    score = score_ref[...]
    if s_keys <= topk:
        bias_ref[...] = jnp.where(score > -jnp.inf, 0.0, -jnp.inf)
    else:
        kf = float(topk)
        score_t = score.T

        def count_ge(t):
            ind = jnp.where(score_t >= t, 1.0, 0.0).reshape(s_keys // SEARCH_PART, SEARCH_PART, qrows)
            return jnp.sum(jnp.sum(ind, axis=0), axis=0, keepdims=True)

        thr = jnp.full((1, qrows), INT_MIN, jnp.int32)
        for b in range(31, -1, -1):
            cand = thr + jnp.int32(INT_MIN if b == 31 else 1 << b)
            thr = jnp.where(count_ge(_key_to_f32(cand)) >= kf, cand, thr)
        thr = jnp.maximum(thr, jnp.int32(NEG_INF_KEY))
        t_lo = jnp.maximum(_key_to_f32(thr), -F32_MAX)
        excess = count_ge(t_lo) - kf
        stats = jnp.concatenate([t_lo, _key_to_f32(thr + 1), excess, jnp.zeros((qrows - 3, qrows), F32)], axis=0).T
        t_lo_col = stats[:, 0:1]
        bias_ref[...] = jnp.where(score >= t_lo_col, 0.0, -jnp.inf)

        @pl.when(jnp.max(excess) > 0.0)
        def _():
            _resolve_threshold_ties(score, t_lo_col, stats[:, 1:2], stats[:, 2:3], kf, s_keys,
                                    bias_ref, cand_ref, pick_ref, rem_ref)

    scale = HEAD_DIM ** -0.5
    for rows, ks in halves:
        bias = bias_ref[rows, :]
        q = q_ref[rows, :]
        for c in range(N_KV_HEADS):
            kc = get_k(ks, c)
            vc = get_v(ks, c)
            qc = jnp.concatenate(
                [q[:, (c * GROUP + g) * HEAD_DIM:(c * GROUP + g + 1) * HEAD_DIM] for g in range(GROUP)], axis=0)
            logits = lax.dot_general(qc, kc, (((1,), (1,)), ((), ())), preferred_element_type=F32)
            es, inv = [], []
            for g in range(GROUP):
                lg = logits[g * CHUNK:(g + 1) * CHUNK] + bias
                e = jnp.exp2((lg - jnp.max(lg, axis=1, keepdims=True)) * (scale * LOG2_E))
                inv.append(1.0 / jnp.sum(e, axis=1, keepdims=True))
                es.append(e.astype(BF16))
            oc = jnp.dot(jnp.concatenate(es, axis=0), vc, preferred_element_type=F32)
            for g in range(GROUP):
                hd = c * GROUP + g
                o_ref[rows, hd * HEAD_DIM:(hd + 1) * HEAD_DIM] = (
                    oc[g * CHUNK:(g + 1) * CHUNK] * inv[g]).astype(BF16)


def _dsa_body(q_ref, iq_ref, iw_ref, kb_ref, vb_ref, ikbd_ref, *rest, **static):
    head = lambda ref: (lambda ks, c: ref[ks, :, c * HEAD_DIM:(c + 1) * HEAD_DIM])
    _dsa_core(q_ref, iq_ref, iw_ref, head(kb_ref), head(vb_ref), lambda ks, c: ikbd_ref[ks, c],
              kb_ref.shape[0], *rest[-6:], **static)


def _dsa_cached_body(q_ref, iq_ref, iw_ref, kn_ref, vn_ref, ikn_ref, pk_ref, pv_ref, pik_ref,
                     o_ref, k_all, v_all, ikbd_all, *scratch, past_len, **static):
    s_keys = static['s_keys']
    new_end = past_len + CHUNK
    for ks in range(2):
        new = slice(ks * CHUNK, (ks + 1) * CHUNK)
        for c in range(N_KV_HEADS):
            for dst, past, fresh in ((k_all, pk_ref, kn_ref), (v_all, pv_ref, vn_ref)):
                dst[ks, c, :past_len, :] = past[ks, pl.ds(c, past_len, stride=N_KV_HEADS), :].astype(BF16)
                dst[ks, c, past_len:new_end, :] = fresh[new, c * HEAD_DIM:(c + 1) * HEAD_DIM]
                dst[ks, c, new_end:, :] = jnp.zeros((s_keys - new_end, HEAD_DIM), BF16)
        ik = jnp.concatenate([pik_ref[ks].astype(BF16), ikn_ref[new, :].astype(BF16),
                              jnp.zeros((s_keys - new_end, IDX_DIM), BF16)], axis=0)
        z = jnp.zeros_like(ik)
        ikbd_all[ks, :s_keys, :] = jnp.concatenate([ik, z], axis=1)
        ikbd_all[ks, s_keys:, :] = jnp.concatenate([z, ik], axis=1)
    _dsa_core(q_ref, iq_ref, iw_ref, lambda ks, c: k_all[ks, c], lambda ks, c: v_all[ks, c],
              lambda ks, c: ikbd_all[ks], 2, o_ref, *scratch, **static)


def _dsa_cached_call(q, iq, iw, kb, vb, ik, pk, pv, pik, topk):
    nb, past_len = pik.shape[0], pik.shape[1]
    l_keys = past_len + CHUNK
    s_keys = -(-l_keys // LANES) * LANES
    body = functools.partial(_dsa_cached_body, past_len=past_len, s_keys=s_keys, s_chunk=s_keys, blk0=0,
                             fixed_valid=l_keys, topk=topk)
    qrow = lambda w: pl.BlockSpec((QROWS, w), lambda n: (n, 0))
    pair = lambda a: pl.BlockSpec((2,) + a.shape[1:], lambda n: (n, 0, 0))
    mask_buf = pltpu.VMEM((QROWS, s_keys), F32)
    heads = pltpu.VMEM((2, N_KV_HEADS, s_keys, HEAD_DIM), BF16)
    return pl.pallas_call(
        body,
        grid=(nb // 2,),
        in_specs=[qrow(ATT_WIDTH), qrow(IDX_WIDTH), qrow(LANES), qrow(KV_WIDTH), qrow(KV_WIDTH), qrow(IDX_DIM),
                  pair(pk), pair(pv), pair(pik)],
        out_specs=qrow(ATT_WIDTH),
        out_shape=jax.ShapeDtypeStruct((nb * CHUNK, ATT_WIDTH), BF16),
        scratch_shapes=[heads, heads, pltpu.VMEM((2, 2 * s_keys, LANES), BF16),
                        mask_buf, mask_buf, mask_buf, mask_buf, pltpu.VMEM((QROWS, LANES), F32)],
        compiler_params=_cparams("parallel"),
        name="dsa_cached",
    )(q, iq, iw, kb, vb, ik, pk, pv, pik)


def _dsa_call(q, iq, iw, kb, vb, ikbd, o_prev, qrows, blk0, n_blk, blks_per_seq, s_keys, s_chunk, topk):
    nb = kb.shape[0]
    blocks_per_step = qrows // CHUNK
    steps_per_seq = blks_per_seq // blocks_per_step
    qrow = lambda w: pl.BlockSpec(
        (qrows, w), lambda n, j: (n * steps_per_seq + blk0 // blocks_per_step + j, 0))
    n_chunks = s_keys // s_chunk
    body = functools.partial(_dsa_body, s_keys=s_keys, s_chunk=s_chunk, blk0=blk0, fixed_valid=None, topk=topk)
    mask_buf = pltpu.VMEM((qrows, s_keys), F32)
    return pl.pallas_call(
        body,
        grid=(nb, n_blk // blocks_per_step),
        in_specs=[qrow(ATT_WIDTH), qrow(IDX_WIDTH), qrow(LANES),
                  pl.BlockSpec((1, s_keys, KV_WIDTH), lambda n, j: (n, 0, 0)),
                  pl.BlockSpec((1, s_keys, KV_WIDTH), lambda n, j: (n, 0, 0)),
                  pl.BlockSpec((1, n_chunks, 2 * s_chunk, LANES), lambda n, j: (n, 0, 0, 0)),
                  pl.BlockSpec(memory_space=pl.ANY)],
        out_specs=qrow(ATT_WIDTH),
        out_shape=jax.ShapeDtypeStruct((nb * blks_per_seq * CHUNK, ATT_WIDTH), BF16),
        scratch_shapes=[mask_buf, mask_buf, mask_buf, mask_buf, pltpu.VMEM((qrows, LANES), F32)],
        input_output_aliases={6: 0},
        compiler_params=_cparams("parallel", "arbitrary"),
        name="dsa",
    )(q, iq, iw, kb, vb, ikbd, o_prev)


def _indexer_key_blocks(ik, s_chunk):
    nb, s, _ = ik.shape
    ikb = ik.astype(BF16).reshape(nb, s // s_chunk, s_chunk, IDX_DIM)
    z = jnp.zeros_like(ikb)
    return jnp.concatenate([jnp.concatenate([ikb, z], axis=-1), jnp.concatenate([z, ikb], axis=-1)], axis=-2)


def _s5prep_body(are_ref, aim_ref, ldt_ref, bre_ref, bim_ref, abre_ref, abim_ref, bbre_ref, bbim_ref):
    a_re, a_im = are_ref[...], aim_ref[...]
    dt = jnp.exp(ldt_ref[...])
    mag = jnp.exp(dt * a_re)
    ab_re = mag * jnp.cos(dt * a_im)
    ab_im = mag * jnp.sin(dt * a_im)
    den = a_re * a_re + a_im * a_im
    f_re = ((ab_re - 1.0) * a_re + ab_im * a_im) / den
    f_im = (ab_im * a_re - (ab_re - 1.0) * a_im) / den
    abre_ref[...] = ab_re
    abim_ref[...] = ab_im
    for c in range(S5_CH):
        b_re, b_im = bre_ref[c], bim_ref[c]
        bbre_ref[c] = f_re * b_re - f_im * b_im
        bbim_ref[c] = f_re * b_im + f_im * b_re


def _s5prep_call(a_re, a_im, log_dt, b_re, b_im):
    rows = S5_LANES // LANES
    flat = lambda a: a.reshape(rows, LANES)
    ldt = jnp.broadcast_to(log_dt[:, None], (S5_GROUPS, S5_STATE))
    chan_major = lambda b: jnp.transpose(b, (2, 0, 1)).reshape(S5_CH, rows, LANES)
    small = jax.ShapeDtypeStruct((rows, LANES), F32)
    big = jax.ShapeDtypeStruct((S5_CH, rows, LANES), F32)
    return pl.pallas_call(
        _s5prep_body,
        out_shape=(small, small, big, big),
        name="s5prep",
    )(flat(a_re), flat(a_im), flat(ldt), chan_major(b_re), chan_major(b_im))


def _gelu_tanh(x):
    return 0.5 * x * (1.0 + jnp.tanh(math.sqrt(2.0 / math.pi) * (x + 0.044715 * (x * x * x))))


def _s5_body(u_ref, x0re_ref, x0im_ref, abre_ref, abim_ref, bre_ref, bim_ref, cre_ref, cim_ref,
             dskip_ref, wglu_ref, bglu_ref, ob_ref, sre_ref, sim_ref, utm, otm, xre, xim, st_re, st_im,
             *, tc, nb, lane_w):
    step = pl.program_id(0)

    @pl.when(step == 0)
    def _():
        st_re[...] = x0re_ref[...]
        st_im[...] = x0im_ref[...]

    for n in range(nb):
        for k in range(S5_SLABS):
            utm[k, pl.ds(n, tc, stride=nb), :] = u_ref[n, :, k * LANES:(k + 1) * LANES]

    for k in range(S5_SLABS):
        us = utm[k].astype(BF16)
        sl = slice(k * S5_SLAB_STATE, (k + 1) * S5_SLAB_STATE)
        xre[:, :, sl] = jnp.dot(us, bre_ref[k], preferred_element_type=F32).reshape(tc, nb, S5_SLAB_STATE)
        xim[:, :, sl] = jnp.dot(us, bim_ref[k], preferred_element_type=F32).reshape(tc, nb, S5_SLAB_STATE)

    for lc in range(S5_LANES // lane_w):
        sl = slice(lc * lane_w, (lc + 1) * lane_w)
        a_r = jnp.broadcast_to(abre_ref[:, sl], (nb, lane_w))
        a_i = jnp.broadcast_to(abim_ref[:, sl], (nb, lane_w))

        s_r, s_i = st_re[:, sl], st_im[:, sl]
        for t in range(tc):
            s_r, s_i = (a_r * s_r - a_i * s_i + xre[t, :, sl],
                        a_r * s_i + a_i * s_r + xim[t, :, sl])
            xre[t, :, sl] = s_r
            xim[t, :, sl] = s_i
        st_re[:, sl] = s_r
        st_im[:, sl] = s_i

    ys = []
    for k in range(S5_SLABS):
        sl = slice(k * S5_SLAB_STATE, (k + 1) * S5_SLAB_STATE)
        xr = xre[:, :, sl].reshape(tc * nb, S5_SLAB_STATE).astype(BF16)
        xi = xim[:, :, sl].reshape(tc * nb, S5_SLAB_STATE).astype(BF16)
        ys.append(jnp.dot(xr, cre_ref[k], preferred_element_type=F32)
                  - jnp.dot(xi, cim_ref[k], preferred_element_type=F32)
                  + dskip_ref[:, k * LANES:(k + 1) * LANES] * utm[k])
    yb = _gelu_tanh(jnp.concatenate(ys, axis=1))
    gate = jax.nn.sigmoid(jnp.dot(yb.astype(BF16), wglu_ref[...], preferred_element_type=F32) + bglu_ref[...])
    o = yb * gate
    for k in range(S5_SLABS):
        otm[k] = o[:, k * LANES:(k + 1) * LANES]
    for n in range(nb):
        for k in range(S5_SLABS):
            ob_ref[n, :, k * LANES:(k + 1) * LANES] = otm[k, pl.ds(n, tc, stride=nb), :].astype(BF16)

    @pl.when(step == pl.num_programs(0) - 1)
    def _():
        sre_ref[...] = st_re[...]
        sim_ref[...] = st_im[...]


def _s5_call(u, x0_re, x0_im, ab_re, ab_im, b_re_bd, b_im_bd, c_re_bd, c_im_bd, d_skip, w_glu, b_glu, tc):
    nb, t_len, _ = u.shape
    lane_w = 1024 if nb <= 8 else 512
    body = functools.partial(_s5_body, tc=tc, nb=nb, lane_w=lane_w)
    state = jax.ShapeDtypeStruct((nb, S5_LANES), F32)
    seq_spec = pl.BlockSpec((nb, tc, S5_WIDTH), lambda i: (0, i, 0))
    slabs = pltpu.VMEM((S5_SLABS, tc * nb, LANES), F32)
    return pl.pallas_call(
        body,
        grid=(t_len // tc,),
        in_specs=[seq_spec,
                  _const_spec((nb, S5_LANES)), _const_spec((nb, S5_LANES)),
                  _const_spec((1, S5_LANES)), _const_spec((1, S5_LANES)),
                  _const_spec(b_re_bd.shape), _const_spec(b_im_bd.shape),
                  _const_spec(c_re_bd.shape), _const_spec(c_im_bd.shape),
                  _const_spec((1, S5_WIDTH)), _const_spec((S5_WIDTH, S5_WIDTH)), _const_spec((1, S5_WIDTH))],
        out_specs=(seq_spec, _const_spec((nb, S5_LANES)), _const_spec((nb, S5_LANES))),
        out_shape=(jax.ShapeDtypeStruct((nb, t_len, S5_WIDTH), BF16), state, state),
        scratch_shapes=[slabs, slabs,
                        pltpu.VMEM((tc, nb, S5_LANES), F32), pltpu.VMEM((tc, nb, S5_LANES), F32),
                        pltpu.VMEM((nb, S5_LANES), F32), pltpu.VMEM((nb, S5_LANES), F32)],
        compiler_params=_cparams("arbitrary"),
        name="s5",
    )(u, x0_re, x0_im, ab_re, ab_im, b_re_bd, b_im_bd, c_re_bd, c_im_bd, d_skip, w_glu, b_glu)


def _block_diag_slabs(w):
    g, r, c = w.shape
    w = w.reshape(S5_SLABS, S5_SLAB_GROUPS, r, c)
    eye = jnp.eye(S5_SLAB_GROUPS, dtype=w.dtype)
    bd = w[:, :, :, None, :] * eye[None, :, None, :, None]
    return bd.reshape(S5_SLABS, S5_SLAB_GROUPS * r, S5_SLAB_GROUPS * c)


def _merge_body(oa_ref, ob_ref, ga_ref, gb_ref, x_ref, wa_ref, wb_ref, wo_ref, g_ref, x1_ref, hf_ref):
    pa = jnp.dot(oa_ref[...], wa_ref[...], preferred_element_type=F32)
    pb = jnp.dot(ob_ref[...], wb_ref[...], preferred_element_type=F32)
    merged = ga_ref[...] * pa + gb_ref[...] * pb
    x1 = x_ref[...] + jnp.dot(merged.astype(BF16), wo_ref[...], preferred_element_type=F32)
    x1_ref[...] = x1
    hf = (x1 * lax.rsqrt(jnp.mean(x1 * x1, axis=-1, keepdims=True) + EPS)) * g_ref[...]
    hf_ref[...] = hf.astype(BF16)


def _merge_call(oa, ob, gates, x2d, w_a, w_b, w_o, g_ffn, tm):
    m = x2d.shape[0]
    row = lambda w: pl.BlockSpec((tm, w), lambda i: (i, 0))
    return pl.pallas_call(
        _merge_body,
        grid=(m // tm,),
        in_specs=[row(ATT_WIDTH), row(S5_WIDTH),
                  pl.BlockSpec((tm, D_MODEL), lambda i: (i, 0)), pl.BlockSpec((tm, D_MODEL), lambda i: (i, 1)),
                  row(D_MODEL), _const_spec(w_a.shape), _const_spec(w_b.shape), _const_spec(w_o.shape),
                  _const_spec((1, D_MODEL))],
        out_specs=(row(D_MODEL), row(D_MODEL)),
        out_shape=(jax.ShapeDtypeStruct((m, D_MODEL), F32), jax.ShapeDtypeStruct((m, D_MODEL), BF16)),
        compiler_params=_cparams("parallel"),
        name="merge",
    )(oa, ob, gates, gates, x2d, w_a, w_b, w_o, g_ffn)


FFN_OUT_CHUNK = 512


def _ffn_body(hf_ref, x1_hbm, wg_ref, wu_ref, wd_ref, g_ref, y_ref, x1_sem, *, final_norm, tm):
    i, f = pl.program_id(0), pl.program_id(1)
    x1_copy = pltpu.make_async_copy(x1_hbm.at[pl.ds(i * tm, tm), :], y_ref, x1_sem)

    @pl.when(f == 0)
    def _():
        x1_copy.start()

    hf = hf_ref[...]
    a = jax.nn.silu(jnp.dot(hf, wg_ref[...], preferred_element_type=F32)) * jnp.dot(
        hf, wu_ref[...], preferred_element_type=F32)
    ab = a.astype(BF16)

    @pl.when(f == 0)
    def _():
        x1_copy.wait()

    for c0 in range(0, D_MODEL, FFN_OUT_CHUNK):
        cols = slice(c0, c0 + FFN_OUT_CHUNK)
        y_ref[:, cols] += jnp.dot(ab, wd_ref[:, cols], preferred_element_type=F32)

    if final_norm:
        @pl.when(f == pl.num_programs(1) - 1)
        def _():
            x2 = y_ref[...]
            y_ref[...] = (x2 * lax.rsqrt(jnp.mean(x2 * x2, axis=-1, keepdims=True) + EPS)) * g_ref[...]


def _ffn_call(hf, x1, w_gate, w_up, w_down, g_final, tm, tf, final_norm):
    m = hf.shape[0]
    return pl.pallas_call(
        functools.partial(_ffn_body, final_norm=final_norm, tm=tm),
        grid=(m // tm, D_FF // tf),
        in_specs=[pl.BlockSpec((tm, D_MODEL), lambda i, f: (i, 0)),
                  pl.BlockSpec(memory_space=pl.ANY),
                  pl.BlockSpec((D_MODEL, tf), lambda i, f: (0, f)),
                  pl.BlockSpec((D_MODEL, tf), lambda i, f: (0, f)),
                  pl.BlockSpec((tf, D_MODEL), lambda i, f: (f, 0)),
                  _const_spec((1, D_MODEL))],
        out_specs=pl.BlockSpec((tm, D_MODEL), lambda i, f: (i, 0)),
        out_shape=jax.ShapeDtypeStruct((m, D_MODEL), F32),
        scratch_shapes=[pltpu.SemaphoreType.DMA(())],
        compiler_params=_cparams("parallel", "arbitrary"),
        name="ffn",
    )(hf, x1, w_gate, w_up, w_down, g_final)


def _rope_tables(pos, dim):
    half = dim // 2
    inv = 1.0 / (ROPE_THETA ** (jnp.arange(half, dtype=F32) * (2.0 / dim)))
    ang = pos[:, None] * inv[None, :]
    cos, sin = jnp.cos(ang), jnp.sin(ang)
    reps = LANES // dim
    return (jnp.tile(jnp.concatenate([cos, cos], axis=-1), (1, reps)),
            jnp.tile(jnp.concatenate([-sin, sin], axis=-1), (1, reps)))


def _split_w_in(w_in):
    o = IN_OFFS
    seg = lambda i, j: w_in[:, o[i]:o[j]].astype(BF16)
    pad = lambda w: jnp.pad(w, ((0, 0), (0, LANES - w.shape[1])))
    return seg(0, 4), jnp.concatenate([pad(seg(4, 5)), pad(seg(5, 6))], axis=1), seg(6, 7), seg(7, 8)


def _layer(x, pos, past, ssm0, lw, tm_proj):
    nb, t_len, _ = x.shape
    m = nb * t_len
    x2d = x.reshape(m, D_MODEL)

    cq, sq = _rope_tables(pos, HEAD_DIM)
    ci, si = _rope_tables(pos, IDX_DIM)
    tabs = (cq, sq, ci, si)
    if t_len < tm_proj:
        tabs = tuple(jnp.tile(t, (tm_proj // t_len, 1)) for t in tabs)

    h, q, k, v, kb, vb, iq, ik, iw, u = _proj_call(
        x2d, lw['g_mix'], lw['w_att'], lw['w_idx'], lw['w_u'], tabs, tm_proj)
    gates = _gates_call(h, lw['w_gl'], tm_proj)

    if past is None:
        s_chunk = DSA_GROUP_BLOCKS * CHUNK
        kb_all = kb.reshape(nb, t_len, KV_WIDTH)
        vb_all = vb.reshape(nb, t_len, KV_WIDTH)
        ikbd = _indexer_key_blocks(ik.reshape(nb, t_len, IDX_DIM), s_chunk)
        n_blk = t_len // CHUNK
        o_a = jnp.zeros((m, ATT_WIDTH), BF16)
        for blk0 in range(0, n_blk, DSA_GROUP_BLOCKS):
            s_keys = (blk0 + DSA_GROUP_BLOCKS) * CHUNK
            qrows = DSA_GROUP_BLOCKS * CHUNK if s_keys <= DSA_WIDE_STEP_MAX_KEYS else QROWS
            o_a = _dsa_call(q, iq, iw, kb_all, vb_all, ikbd, o_a, qrows, blk0, DSA_GROUP_BLOCKS,
                            n_blk, s_keys, s_chunk, min(TOPK_MAX, t_len // 4))
    else:
        pk, pv, pik = past
        assert t_len == CHUNK, "the cached stream is one query block per sequence"
        l_keys = pk.shape[1] + t_len
        o_a = _dsa_cached_call(q, iq, iw, kb, vb, ik, pk.reshape(nb, -1, HEAD_DIM), pv.reshape(nb, -1, HEAD_DIM),
                               pik, min(TOPK_MAX, l_keys // 4))

    x0_re, x0_im = ssm0
    o_b, s_re, s_im = _s5_call(
        u.reshape(nb, t_len, S5_WIDTH), x0_re.reshape(nb, S5_LANES), x0_im.reshape(nb, S5_LANES),
        lw['ab_re'], lw['ab_im'], lw['b_re_bd'], lw['b_im_bd'], lw['c_re_bd'], lw['c_im_bd'],
        lw['d_skip'], lw['w_glu'], lw['b_glu'], min(t_len, S5_STEP_ROWS // nb))

    x1, hf = _merge_call(o_a, o_b.reshape(m, S5_WIDTH), gates, x2d, lw['w_proj_a'], lw['w_proj_b'],
                         lw['w_out'], lw['g_ffn'], 256)
    caches = (k.reshape(nb, t_len, N_KV_HEADS, HEAD_DIM), v.reshape(nb, t_len, N_KV_HEADS, HEAD_DIM),
              ik.reshape(nb, t_len, IDX_DIM), s_re.reshape(nb, S5_GROUPS, S5_STATE),
              s_im.reshape(nb, S5_GROUPS, S5_STATE))
    return x1, hf, caches


def kernel(x_prompt, x_sample, cache_k, cache_v, cache_idx_k, state_ssm_re, state_ssm_im,
           g_mix, w_in, a_re, a_im, log_dt, b_re, b_im, c_re, c_im, d_skip, w_glu, b_glu,
           w_proj_a, w_proj_b, w_out, g_ffn, w_gate, w_up, w_down, g_final):
    depth = w_in.shape[0]
    t_p, t_s = x_prompt.shape[1], x_sample.shape[1]
    past_len = cache_k.shape[2]
    pos_p = jnp.arange(t_p, dtype=F32)
    pos_s = past_len + jnp.arange(t_s, dtype=F32)
    nb_p, nb_s = x_prompt.shape[0], x_sample.shape[0]
    g_fin = g_final.reshape(1, D_MODEL)

    hp, hs = x_prompt, x_sample
    outs_p, outs_s = [], []
    for l in range(depth):
        w_att, w_idx, w_u, w_gl = _split_w_in(w_in[l])
        ab_re, ab_im, bb_re, bb_im = _s5prep_call(a_re[l], a_im[l], log_dt[l], b_re[l], b_im[l])
        per_group = lambda bb: jnp.transpose(bb.reshape(S5_CH, S5_GROUPS, S5_STATE), (1, 0, 2))
        lw = {
            'g_mix': g_mix[l].reshape(1, D_MODEL), 'w_att': w_att, 'w_idx': w_idx, 'w_u': w_u, 'w_gl': w_gl,
            'ab_re': ab_re.reshape(1, S5_LANES), 'ab_im': ab_im.reshape(1, S5_LANES),
            'b_re_bd': _block_diag_slabs(per_group(bb_re)).astype(BF16),
            'b_im_bd': _block_diag_slabs(per_group(bb_im)).astype(BF16),
            'c_re_bd': _block_diag_slabs(jnp.transpose(c_re[l], (0, 2, 1))).astype(BF16),
            'c_im_bd': _block_diag_slabs(jnp.transpose(c_im[l], (0, 2, 1))).astype(BF16),
            'd_skip': d_skip[l].reshape(1, S5_WIDTH), 'w_glu': w_glu[l].astype(BF16),
            'b_glu': b_glu[l].reshape(1, S5_WIDTH),
            'w_proj_a': w_proj_a[l].astype(BF16), 'w_proj_b': w_proj_b[l].astype(BF16),
            'w_out': w_out[l].astype(BF16), 'g_ffn': g_ffn[l].reshape(1, D_MODEL),
        }
        wg, wu, wd = w_gate[l].astype(BF16), w_up[l].astype(BF16), w_down[l].astype(BF16)
        last = l == depth - 1

        zeros = jnp.zeros((nb_p, S5_GROUPS, S5_STATE), F32)
        x1p, hfp, cp = _layer(hp, pos_p, None, (zeros, zeros), lw, 512)
        x1s, hfs, cs = _layer(hs, pos_s, (cache_k[l], cache_v[l], cache_idx_k[l]),
                              (state_ssm_re[l], state_ssm_im[l]), lw, 512)
        yp = _ffn_call(hfp, x1p, wg, wu, wd, g_fin, 1024, 512, last)
        ys = _ffn_call(hfs, x1s, wg, wu, wd, g_fin, 1024, 512, last)
        hp = yp.reshape(nb_p, t_p, D_MODEL)
        hs = ys.reshape(nb_s, t_s, D_MODEL)
        outs_p.append(cp)
        outs_s.append(cs)

    stack = lambda outs, i: jnp.stack([o[i] for o in outs])
    return (hp, hs,
            stack(outs_p, 0), stack(outs_p, 1), stack(outs_p, 2), stack(outs_p, 3), stack(outs_p, 4),
            stack(outs_s, 0), stack(outs_s, 1), stack(outs_s, 2), stack(outs_s, 3), stack(outs_s, 4))
```

```python
import functools
import math

import jax
import jax.numpy as jnp
import numpy as np
from jax import lax
from jax.experimental import pallas as pl
from jax.experimental.pallas import tpu as pltpu

F32 = jnp.float32
BF16 = jnp.bfloat16

D_MODEL = 2048
CHUNK = 64
HEAD_DIM = 128
N_HEADS = 8
N_KV_HEADS = 2
GROUP = N_HEADS // N_KV_HEADS
ATT_WIDTH = N_HEADS * HEAD_DIM
KV_WIDTH = N_KV_HEADS * HEAD_DIM
IDX_HEADS = 16
IDX_DIM = 64
IDX_WIDTH = IDX_HEADS * IDX_DIM
TOPK_MAX = 256
S5_CH = 16
S5_WIDTH = D_MODEL // 2
S5_GROUPS = S5_WIDTH // S5_CH
S5_STATE = 64
S5_LANES = S5_GROUPS * S5_STATE
D_FF = 5632
ROPE_THETA = 10000.0
EPS = 1e-6
IN_SIZES = (ATT_WIDTH, KV_WIDTH, KV_WIDTH, IDX_WIDTH, IDX_DIM, IDX_HEADS, S5_WIDTH, 2 * D_MODEL)
IN_OFFS = tuple(int(s) for s in np.cumsum((0,) + IN_SIZES))

LANES = 128
S5_SLAB_GROUPS = LANES // S5_CH
S5_SLABS = S5_GROUPS // S5_SLAB_GROUPS
S5_SLAB_STATE = S5_SLAB_GROUPS * S5_STATE
INT_MIN = -2 ** 31
DSA_GROUP_BLOCKS = 4
QROWS = 2 * CHUNK
DSA_WIDE_STEP_MAX_KEYS = 1280
S5_STEP_ROWS = 512
SEARCH_PART = 128

VMEM_LIMIT = 56 * 2 ** 20
PROJ_ROWS = 512
MERGE_ROWS = 256
FFN_ROWS = 1024
FFN_FF_TILE = 512


def _cparams(*sem):
    return pltpu.CompilerParams(dimension_semantics=sem, vmem_limit_bytes=VMEM_LIMIT)


def _const_spec(shape):
    nd = len(shape)
    return pl.BlockSpec(shape, lambda *_: (0,) * nd, pipeline_mode=pl.Buffered(1))


_PQ = 0
_PK = _PQ + ATT_WIDTH
_PV = _PK + KV_WIDTH
_PIQ = _PV + KV_WIDTH
_PEND = _PIQ + IDX_WIDTH
IW_SCALE = (IDX_DIM ** -0.5) * (IDX_HEADS ** -0.5)


def _rope128(z, cos, sin):
    return z * cos + pltpu.roll(z, HEAD_DIM // 2, 1) * sin


def _rope64(z, cos, sin, low_half):
    partner = jnp.where(low_half, pltpu.roll(z, LANES - IDX_DIM // 2, 1), pltpu.roll(z, IDX_DIM // 2, 1))
    return z * cos + partner * sin


def _proj_body(x_ref, g_ref, wa_ref, wi_ref, wu_ref, cq_ref, sq_ref, ci_ref, si_ref,
               h_ref, q_ref, k_ref, v_ref, kb_ref, vb_ref, iq_ref, ik_ref, iw_ref, u_ref):
    x = x_ref[...]
    tm = x.shape[0]
    h = (x * lax.rsqrt(jnp.mean(x * x, axis=-1, keepdims=True) + EPS)) * g_ref[...]
    hb = h.astype(BF16)
    h_ref[...] = hb

    def proj(lo, hi):
        return jnp.dot(hb, wa_ref[:, lo:hi], preferred_element_type=F32)

    cq, sq, ci, si = cq_ref[...], sq_ref[...], ci_ref[...], si_ref[...]
    low_half = (lax.broadcasted_iota(jnp.int32, cq.shape, 1) & (IDX_DIM - 1)) < (IDX_DIM // 2)

    zq = proj(_PQ, _PK)
    for hd in range(N_HEADS):
        sl = slice(hd * HEAD_DIM, (hd + 1) * HEAD_DIM)
        q_ref[:, sl] = _rope128(zq[:, sl], cq, sq).astype(BF16)
    zk = proj(_PK, _PV)
    zv = proj(_PV, _PIQ)
    vb_ref[...] = zv.astype(BF16)
    for hd in range(N_KV_HEADS):
        sl = slice(hd * HEAD_DIM, (hd + 1) * HEAD_DIM)
        head_rows = pl.ds(hd, tm, stride=N_KV_HEADS)
        r = _rope128(zk[:, sl], cq, sq)
        k_ref[head_rows, :] = r
        kb_ref[:, sl] = r.astype(BF16)
        v_ref[head_rows, :] = zv[:, sl]
    ziq = proj(_PIQ, _PEND)
    for p in range(IDX_WIDTH // LANES):
        sl = slice(p * LANES, (p + 1) * LANES)
        iq_ref[:, sl] = _rope64(ziq[:, sl], ci, si, low_half).astype(BF16)
    z2 = jnp.dot(hb, wi_ref[...], preferred_element_type=F32)
    ik_ref[...] = _rope64(z2[:, :LANES], ci, si, low_half)[:, :IDX_DIM]
    iw_ref[...] = z2[:, LANES:] * IW_SCALE
    u_ref[...] = jnp.dot(hb, wu_ref[...], preferred_element_type=F32)


def _proj_call(x2d, g_mix, w_att, w_idx, w_u, tabs, tm):
    m = x2d.shape[0]
    n_tiles = m // tm
    tab_tiles = tabs[0].shape[0] // tm
    row = lambda w: pl.BlockSpec((tm, w), lambda i: (i, 0))
    tab_spec = pl.BlockSpec((tm, LANES), lambda i: (i % tab_tiles, 0))
    kv_cache = jax.ShapeDtypeStruct((m * N_KV_HEADS, HEAD_DIM), F32)
    kv_cache_spec = pl.BlockSpec((tm * N_KV_HEADS, HEAD_DIM), lambda i: (i, 0))
    out_shape = (
        jax.ShapeDtypeStruct((m, D_MODEL), BF16),
        jax.ShapeDtypeStruct((m, ATT_WIDTH), BF16),
        kv_cache,
        kv_cache,
        jax.ShapeDtypeStruct((m, KV_WIDTH), BF16),
        jax.ShapeDtypeStruct((m, KV_WIDTH), BF16),
        jax.ShapeDtypeStruct((m, IDX_WIDTH), BF16),
        jax.ShapeDtypeStruct((m, IDX_DIM), F32),
        jax.ShapeDtypeStruct((m, LANES), F32),
        jax.ShapeDtypeStruct((m, S5_WIDTH), F32),
    )
    out_specs = (row(D_MODEL), row(ATT_WIDTH), kv_cache_spec, kv_cache_spec, row(KV_WIDTH), row(KV_WIDTH),
                 row(IDX_WIDTH), row(IDX_DIM), row(LANES), row(S5_WIDTH))
    return pl.pallas_call(
        _proj_body,
        grid=(n_tiles,),
        in_specs=[row(D_MODEL), _const_spec((1, D_MODEL)),
                  _const_spec(w_att.shape), _const_spec(w_idx.shape), _const_spec(w_u.shape),
                  tab_spec, tab_spec, tab_spec, tab_spec],
        out_specs=out_specs,
        out_shape=out_shape,
        compiler_params=_cparams("parallel"),
        name="proj",
    )(x2d, g_mix, w_att, w_idx, w_u, *tabs)


def _gates_body(h_ref, w_ref, o_ref):
    o_ref[...] = jax.nn.sigmoid(jnp.dot(h_ref[...], w_ref[...], preferred_element_type=F32))


def _gates_call(h, w_gl, tm):
    m, n = h.shape[0], w_gl.shape[1]
    return pl.pallas_call(
        _gates_body,
        grid=(m // tm,),
        in_specs=[pl.BlockSpec((tm, D_MODEL), lambda i: (i, 0)), _const_spec(w_gl.shape)],
        out_specs=pl.BlockSpec((tm, n), lambda i: (i, 0)),
        out_shape=jax.ShapeDtypeStruct((m, n), F32),
        compiler_params=_cparams("parallel"),
        name="gates",
    )(h, w_gl)


LOG2_E = math.log2(math.e)
NEG_INF_KEY = INT_MIN + 0x7FFFFF
F32_MAX = float(np.finfo(np.float32).max)


def _key_to_f32(key):
    return lax.bitcast_convert_type(key ^ ((key >> 31) & jnp.int32(0x7FFFFFFF)), F32)


def _row_count(mask):
    return jnp.sum(jnp.where(mask, 1.0, 0.0), axis=1, keepdims=True)


def _resolve_threshold_ties(score, t_lo, t_next, excess, kf, s_keys, bias_ref, cand_ref, pick_ref, rem_ref):
    tied_row = excess > 0.0
    ge = score >= t_lo
    above = score >= t_next
    cand_ref[...] = jnp.where(ge & jnp.logical_not(above) & tied_row, 1.0, 0.0)
    pick_ref[...] = jnp.zeros_like(score)
    rem0 = jnp.where(tied_row, kf - _row_count(above), 0.0)
    rem_ref[...] = jnp.broadcast_to(rem0, rem_ref.shape)
    col = lax.broadcasted_iota(jnp.int32, score.shape, 1)

    def take_next_value(_):
        cand = cand_ref[...] > 0.5
        rem = rem_ref[:, :1]
        top = jnp.max(jnp.where(cand, score, -jnp.inf), axis=1, keepdims=True)
        eq = cand & (score == top)
        last = jnp.zeros((score.shape[0], 1), jnp.int32)
        for b in range(int(s_keys).bit_length() - 1, -1, -1):
            nxt = last + jnp.int32(1 << b)
            last = jnp.where(_row_count(eq & (col < nxt)) < rem, nxt, last)
        take = eq & (col <= last) & (rem > 0.0)
        pick_ref[...] = jnp.where(take, 1.0, pick_ref[...])
        cand_ref[...] = jnp.where(eq, 0.0, cand_ref[...])
        rem = jnp.where(top > -jnp.inf, rem - _row_count(take), 0.0)
        rem_ref[...] = jnp.broadcast_to(rem, rem_ref.shape)
        return jnp.max(rem) > 0.0

    lax.while_loop(lambda go: go, take_next_value, jnp.max(rem0) > 0.0)
    chosen = above | (pick_ref[...] > 0.5)
    bias_ref[...] = jnp.where(tied_row, jnp.where(chosen, 0.0, -jnp.inf), bias_ref[...])


def _dsa_core(q_ref, iq_ref, iw_ref, get_k, get_v, get_ikbd, n_keysets,
              o_ref, score_ref, bias_ref, cand_ref, pick_ref, rem_ref, *,
              s_keys, s_chunk, blk0, fixed_valid, topk):
    qrows = q_ref.shape[0]
    halves = [(slice(hf * CHUNK, (hf + 1) * CHUNK), hf % n_keysets) for hf in range(qrows // CHUNK)]

    pairs = IDX_WIDTH // LANES
    for hf, (rows, ks) in enumerate(halves):
        if fixed_valid is None:
            valid = (blk0 + len(halves) * pl.program_id(1) + hf + 1) * CHUNK
        else:
            valid = fixed_valid
        iq = iq_ref[rows, :]
        lhs = jnp.concatenate([iq[:, p * LANES:(p + 1) * LANES] for p in range(pairs)], axis=0)
        iw = iw_ref[rows, :]
        for c in range(s_keys // s_chunk):
            logits = lax.dot_general(lhs, get_ikbd(ks, c), (((1,), (1,)), ((), ())), preferred_element_type=F32)
            acc = jnp.zeros((CHUNK, s_chunk), F32)
            for p in range(pairs):
                lp = logits[p * CHUNK:(p + 1) * CHUNK]
                acc = acc + jnp.maximum(lp[:, :s_chunk], 0.0) * iw[:, 2 * p:2 * p + 1]
                acc = acc + jnp.maximum(lp[:, s_chunk:], 0.0) * iw[:, 2 * p + 1:2 * p + 2]
            col = c * s_chunk + lax.broadcasted_iota(jnp.int32, acc.shape, 1)
            score_ref[rows, c * s_chunk:(c + 1) * s_chunk] = jnp.where(col < valid, acc, -jnp.inf)

    score = score_ref[...]
    if s_keys <= topk:
        bias_ref[...] = jnp.where(score > -jnp.inf, 0.0, -jnp.inf)
    else:
        kf = float(topk)
        score_t = score.T

        def count_ge(t):
            ind = jnp.where(score_t >= t, 1.0, 0.0).reshape(s_keys // SEARCH_PART, SEARCH_PART, qrows)
            return jnp.sum(jnp.sum(ind, axis=0), axis=0, keepdims=True)

        thr = jnp.full((1, qrows), INT_MIN, jnp.int32)
        for b in range(31, -1, -1):
            cand = thr + jnp.int32(INT_MIN if b == 31 else 1 << b)
            thr = jnp.where(count_ge(_key_to_f32(cand)) >= kf, cand, thr)
        thr = jnp.maximum(thr, jnp.int32(NEG_INF_KEY))
        t_lo = jnp.maximum(_key_to_f32(thr), -F32_MAX)
        excess = count_ge(t_lo) - kf
        stats = jnp.concatenate([t_lo, _key_to_f32(thr + 1), excess, jnp.zeros((qrows - 3, qrows), F32)], axis=0).T
        t_lo_col = stats[:, 0:1]
        bias_ref[...] = jnp.where(score >= t_lo_col, 0.0, -jnp.inf)

        @pl.when(jnp.max(excess) > 0.0)
        def _():
            _resolve_threshold_ties(score, t_lo_col, stats[:, 1:2], stats[:, 2:3], kf, s_keys,
                                    bias_ref, cand_ref, pick_ref, rem_ref)

    scale = HEAD_DIM ** -0.5
    for rows, ks in halves:
        bias = bias_ref[rows, :]
        q = q_ref[rows, :]
        for c in range(N_KV_HEADS):
            kc = get_k(ks, c)
            vc = get_v(ks, c)
            qc = jnp.concatenate(
                [q[:, (c * GROUP + g) * HEAD_DIM:(c * GROUP + g + 1) * HEAD_DIM] for g in range(GROUP)], axis=0)
            logits = lax.dot_general(qc, kc, (((1,), (1,)), ((), ())), preferred_element_type=F32)
            es, inv = [], []
            for g in range(GROUP):
                lg = logits[g * CHUNK:(g + 1) * CHUNK] + bias
                e = jnp.exp2((lg - jnp.max(lg, axis=1, keepdims=True)) * (scale * LOG2_E))
                inv.append(1.0 / jnp.sum(e, axis=1, keepdims=True))
                es.append(e.astype(BF16))
            oc = jnp.dot(jnp.concatenate(es, axis=0), vc, preferred_element_type=F32)
            for g in range(GROUP):
                hd = c * GROUP + g
                o_ref[rows, hd * HEAD_DIM:(hd + 1) * HEAD_DIM] = (
                    oc[g * CHUNK:(g + 1) * CHUNK] * inv[g]).astype(BF16)


def _dsa_body(q_ref, iq_ref, iw_ref, kb_ref, vb_ref, ikbd_ref, *rest, **static):
    head = lambda ref: (lambda ks, c: ref[ks, :, c * HEAD_DIM:(c + 1) * HEAD_DIM])
    _dsa_core(q_ref, iq_ref, iw_ref, head(kb_ref), head(vb_ref), lambda ks, c: ikbd_ref[ks, c],
              kb_ref.shape[0], *rest[-6:], **static)


def _dsa_cached_body(q_ref, iq_ref, iw_ref, kn_ref, vn_ref, ikn_ref, pk_ref, pv_ref, pik_ref,
                     o_ref, k_all, v_all, ikbd_all, *scratch, past_len, **static):
    s_keys = static['s_keys']
    new_end = past_len + CHUNK
    for ks in range(2):
        new = slice(ks * CHUNK, (ks + 1) * CHUNK)
        for c in range(N_KV_HEADS):
            for dst, past, fresh in ((k_all, pk_ref, kn_ref), (v_all, pv_ref, vn_ref)):
                dst[ks, c, :past_len, :] = past[ks, pl.ds(c, past_len, stride=N_KV_HEADS), :].astype(BF16)
                dst[ks, c, past_len:new_end, :] = fresh[new, c * HEAD_DIM:(c + 1) * HEAD_DIM]
                dst[ks, c, new_end:, :] = jnp.zeros((s_keys - new_end, HEAD_DIM), BF16)
        ik = jnp.concatenate([pik_ref[ks].astype(BF16), ikn_ref[new, :].astype(BF16),
                              jnp.zeros((s_keys - new_end, IDX_DIM), BF16)], axis=0)
        z = jnp.zeros_like(ik)
        ikbd_all[ks, :s_keys, :] = jnp.concatenate([ik, z], axis=1)
        ikbd_all[ks, s_keys:, :] = jnp.concatenate([z, ik], axis=1)
    _dsa_core(q_ref, iq_ref, iw_ref, lambda ks, c: k_all[ks, c], lambda ks, c: v_all[ks, c],
              lambda ks, c: ikbd_all[ks], 2, o_ref, *scratch, **static)


def _dsa_cached_call(q, iq, iw, kb, vb, ik, pk, pv, pik, topk):
    nb, past_len = pik.shape[0], pik.shape[1]
    l_keys = past_len + CHUNK
    s_keys = -(-l_keys // LANES) * LANES
    body = functools.partial(_dsa_cached_body, past_len=past_len, s_keys=s_keys, s_chunk=s_keys, blk0=0,
                             fixed_valid=l_keys, topk=topk)
    qrow = lambda w: pl.BlockSpec((QROWS, w), lambda n: (n, 0))
    pair = lambda a: pl.BlockSpec((2,) + a.shape[1:], lambda n: (n, 0, 0))
    mask_buf = pltpu.VMEM((QROWS, s_keys), F32)
    heads = pltpu.VMEM((2, N_KV_HEADS, s_keys, HEAD_DIM), BF16)
    return pl.pallas_call(
        body,
        grid=(nb // 2,),
        in_specs=[qrow(ATT_WIDTH), qrow(IDX_WIDTH), qrow(LANES), qrow(KV_WIDTH), qrow(KV_WIDTH), qrow(IDX_DIM),
                  pair(pk), pair(pv), pair(pik)],
        out_specs=qrow(ATT_WIDTH),
        out_shape=jax.ShapeDtypeStruct((nb * CHUNK, ATT_WIDTH), BF16),
        scratch_shapes=[heads, heads, pltpu.VMEM((2, 2 * s_keys, LANES), BF16),
                        mask_buf, mask_buf, mask_buf, mask_buf, pltpu.VMEM((QROWS, LANES), F32)],
        compiler_params=_cparams("parallel"),
        name="dsa_cached",
    )(q, iq, iw, kb, vb, ik, pk, pv, pik)


def _dsa_call(q, iq, iw, kb, vb, ikbd, o_prev, qrows, blk0, n_blk, blks_per_seq, s_keys, s_chunk, topk):
    nb = kb.shape[0]
    blocks_per_step = qrows // CHUNK
    steps_per_seq = blks_per_seq // blocks_per_step
    qrow = lambda w: pl.BlockSpec(
        (qrows, w), lambda n, j: (n * steps_per_seq + blk0 // blocks_per_step + j, 0))
    n_chunks = s_keys // s_chunk
    body = functools.partial(_dsa_body, s_keys=s_keys, s_chunk=s_chunk, blk0=blk0, fixed_valid=None, topk=topk)
    mask_buf = pltpu.VMEM((qrows, s_keys), F32)
    return pl.pallas_call(
        body,
        grid=(nb, n_blk // blocks_per_step),
        in_specs=[qrow(ATT_WIDTH), qrow(IDX_WIDTH), qrow(LANES),
                  pl.BlockSpec((1, s_keys, KV_WIDTH), lambda n, j: (n, 0, 0)),
                  pl.BlockSpec((1, s_keys, KV_WIDTH), lambda n, j: (n, 0, 0)),
                  pl.BlockSpec((1, n_chunks, 2 * s_chunk, LANES), lambda n, j: (n, 0, 0, 0)),
                  pl.BlockSpec(memory_space=pl.ANY)],
        out_specs=qrow(ATT_WIDTH),
        out_shape=jax.ShapeDtypeStruct((nb * blks_per_seq * CHUNK, ATT_WIDTH), BF16),
        scratch_shapes=[mask_buf, mask_buf, mask_buf, mask_buf, pltpu.VMEM((qrows, LANES), F32)],
        input_output_aliases={6: 0},
        compiler_params=_cparams("parallel", "arbitrary"),
        name="dsa",
    )(q, iq, iw, kb, vb, ikbd, o_prev)


def _indexer_key_blocks(ik, s_chunk):
    nb, s, _ = ik.shape
    ikb = ik.astype(BF16).reshape(nb, s // s_chunk, s_chunk, IDX_DIM)
    z = jnp.zeros_like(ikb)
    return jnp.concatenate([jnp.concatenate([ikb, z], axis=-1), jnp.concatenate([z, ikb], axis=-1)], axis=-2)


def _s5prep_body(are_ref, aim_ref, ldt_ref, bre_ref, bim_ref, abre_ref, abim_ref, bbre_ref, bbim_ref):
    a_re, a_im = are_ref[...], aim_ref[...]
    dt = jnp.exp(ldt_ref[...])
    mag = jnp.exp(dt * a_re)
    ab_re = mag * jnp.cos(dt * a_im)
    ab_im = mag * jnp.sin(dt * a_im)
    den = a_re * a_re + a_im * a_im
    f_re = ((ab_re - 1.0) * a_re + ab_im * a_im) / den
    f_im = (ab_im * a_re - (ab_re - 1.0) * a_im) / den
    abre_ref[...] = ab_re
    abim_ref[...] = ab_im
    for c in range(S5_CH):
        b_re, b_im = bre_ref[c], bim_ref[c]
        bbre_ref[c] = f_re * b_re - f_im * b_im
        bbim_ref[c] = f_re * b_im + f_im * b_re


def _s5prep_call(a_re, a_im, log_dt, b_re, b_im):
    rows = S5_LANES // LANES
    flat = lambda a: a.reshape(rows, LANES)
    ldt = jnp.broadcast_to(log_dt[:, None], (S5_GROUPS, S5_STATE))
    chan_major = lambda b: jnp.transpose(b, (2, 0, 1)).reshape(S5_CH, rows, LANES)
    small = jax.ShapeDtypeStruct((rows, LANES), F32)
    big = jax.ShapeDtypeStruct((S5_CH, rows, LANES), F32)
    return pl.pallas_call(
        _s5prep_body,
        out_shape=(small, small, big, big),
        name="s5prep",
    )(flat(a_re), flat(a_im), flat(ldt), chan_major(b_re), chan_major(b_im))


def _gelu_tanh(x):
    return 0.5 * x * (1.0 + jnp.tanh(math.sqrt(2.0 / math.pi) * (x + 0.044715 * (x * x * x))))


def _s5_body(u_ref, x0re_ref, x0im_ref, abre_ref, abim_ref, bre_ref, bim_ref, cre_ref, cim_ref,
             dskip_ref, wglu_ref, bglu_ref, ob_ref, sre_ref, sim_ref, utm, otm, xre, xim, st_re, st_im,
             *, tc, nb):
    step = pl.program_id(0)

    @pl.when(step == 0)
    def _():
        st_re[...] = x0re_ref[...]
        st_im[...] = x0im_ref[...]

    for n in range(nb):
        for k in range(S5_SLABS):
            utm[k, pl.ds(n, tc, stride=nb), :] = u_ref[n, :, k * LANES:(k + 1) * LANES]

    def project_in(k):
        us = utm[k].astype(BF16)
        sl = slice(k * S5_SLAB_STATE, (k + 1) * S5_SLAB_STATE)
        xre[:, :, sl] = jnp.dot(us, bre_ref[k], preferred_element_type=F32).reshape(tc, nb, S5_SLAB_STATE)
        xim[:, :, sl] = jnp.dot(us, bim_ref[k], preferred_element_type=F32).reshape(tc, nb, S5_SLAB_STATE)

    def scan(k):
        sl = slice(k * S5_SLAB_STATE, (k + 1) * S5_SLAB_STATE)
        a_r = jnp.broadcast_to(abre_ref[:, sl], (nb, S5_SLAB_STATE))
        a_i = jnp.broadcast_to(abim_ref[:, sl], (nb, S5_SLAB_STATE))
        s_r, s_i = st_re[:, sl], st_im[:, sl]
        for t in range(tc):
            s_r, s_i = (a_r * s_r - a_i * s_i + xre[t, :, sl],
                        a_r * s_i + a_i * s_r + xim[t, :, sl])
            xre[t, :, sl] = s_r
            xim[t, :, sl] = s_i
        st_re[:, sl] = s_r
        st_im[:, sl] = s_i

    def project_out(k):
        sl = slice(k * S5_SLAB_STATE, (k + 1) * S5_SLAB_STATE)
        xr = xre[:, :, sl].reshape(tc * nb, S5_SLAB_STATE).astype(BF16)
        xi = xim[:, :, sl].reshape(tc * nb, S5_SLAB_STATE).astype(BF16)
        return (jnp.dot(xr, cre_ref[k], preferred_element_type=F32)
                - jnp.dot(xi, cim_ref[k], preferred_element_type=F32)
                + dskip_ref[:, k * LANES:(k + 1) * LANES] * utm[k])

    project_in(0)
    ys = []
    for k in range(S5_SLABS):
        if k + 1 < S5_SLABS:
            project_in(k + 1)
        scan(k)
        ys.append(project_out(k))
    yb = _gelu_tanh(jnp.concatenate(ys, axis=1))
    gate = jax.nn.sigmoid(jnp.dot(yb.astype(BF16), wglu_ref[...], preferred_element_type=F32) + bglu_ref[...])
    o = yb * gate
    for k in range(S5_SLABS):
        otm[k] = o[:, k * LANES:(k + 1) * LANES]
    for n in range(nb):
        for k in range(S5_SLABS):
            ob_ref[n, :, k * LANES:(k + 1) * LANES] = otm[k, pl.ds(n, tc, stride=nb), :].astype(BF16)

    @pl.when(step == pl.num_programs(0) - 1)
    def _():
        sre_ref[...] = st_re[...]
        sim_ref[...] = st_im[...]


def _s5_call(u, x0_re, x0_im, ab_re, ab_im, b_re_bd, b_im_bd, c_re_bd, c_im_bd, d_skip, w_glu, b_glu, tc):
    nb, t_len, _ = u.shape
    body = functools.partial(_s5_body, tc=tc, nb=nb)
    state = jax.ShapeDtypeStruct((nb, S5_LANES), F32)
    seq_spec = pl.BlockSpec((nb, tc, S5_WIDTH), lambda i: (0, i, 0))
    slabs = pltpu.VMEM((S5_SLABS, tc * nb, LANES), F32)
    return pl.pallas_call(
        body,
        grid=(t_len // tc,),
        in_specs=[seq_spec,
                  _const_spec((nb, S5_LANES)), _const_spec((nb, S5_LANES)),
                  _const_spec((1, S5_LANES)), _const_spec((1, S5_LANES)),
                  _const_spec(b_re_bd.shape), _const_spec(b_im_bd.shape),
                  _const_spec(c_re_bd.shape), _const_spec(c_im_bd.shape),
                  _const_spec((1, S5_WIDTH)), _const_spec((S5_WIDTH, S5_WIDTH)), _const_spec((1, S5_WIDTH))],
        out_specs=(seq_spec, _const_spec((nb, S5_LANES)), _const_spec((nb, S5_LANES))),
        out_shape=(jax.ShapeDtypeStruct((nb, t_len, S5_WIDTH), BF16), state, state),
        scratch_shapes=[slabs, slabs,
                        pltpu.VMEM((tc, nb, S5_LANES), F32), pltpu.VMEM((tc, nb, S5_LANES), F32),
                        pltpu.VMEM((nb, S5_LANES), F32), pltpu.VMEM((nb, S5_LANES), F32)],
        compiler_params=_cparams("arbitrary"),
        name="s5",
    )(u, x0_re, x0_im, ab_re, ab_im, b_re_bd, b_im_bd, c_re_bd, c_im_bd, d_skip, w_glu, b_glu)


def _block_diag_slabs(w):
    g, r, c = w.shape
    w = w.reshape(S5_SLABS, S5_SLAB_GROUPS, r, c)
    eye = jnp.eye(S5_SLAB_GROUPS, dtype=w.dtype)
    bd = w[:, :, :, None, :] * eye[None, :, None, :, None]
    return bd.reshape(S5_SLABS, S5_SLAB_GROUPS * r, S5_SLAB_GROUPS * c)


def _merge_body(oa_ref, ob_ref, ga_ref, gb_ref, x_ref, wa_ref, wb_ref, wo_ref, g_ref, x1_ref, hf_ref):
    pa = jnp.dot(oa_ref[...], wa_ref[...], preferred_element_type=F32)
    pb = jnp.dot(ob_ref[...], wb_ref[...], preferred_element_type=F32)
    merged = ga_ref[...] * pa + gb_ref[...] * pb
    x1 = x_ref[...] + jnp.dot(merged.astype(BF16), wo_ref[...], preferred_element_type=F32)
    x1_ref[...] = x1
    hf = (x1 * lax.rsqrt(jnp.mean(x1 * x1, axis=-1, keepdims=True) + EPS)) * g_ref[...]
    hf_ref[...] = hf.astype(BF16)


def _merge_call(oa, ob, gates, x2d, w_a, w_b, w_o, g_ffn, tm):
    m = x2d.shape[0]
    row = lambda w: pl.BlockSpec((tm, w), lambda i: (i, 0))
    return pl.pallas_call(
        _merge_body,
        grid=(m // tm,),
        in_specs=[row(ATT_WIDTH), row(S5_WIDTH),
                  pl.BlockSpec((tm, D_MODEL), lambda i: (i, 0)), pl.BlockSpec((tm, D_MODEL), lambda i: (i, 1)),
                  row(D_MODEL), _const_spec(w_a.shape), _const_spec(w_b.shape), _const_spec(w_o.shape),
                  _const_spec((1, D_MODEL))],
        out_specs=(row(D_MODEL), row(D_MODEL)),
        out_shape=(jax.ShapeDtypeStruct((m, D_MODEL), F32), jax.ShapeDtypeStruct((m, D_MODEL), BF16)),
        compiler_params=_cparams("parallel"),
        name="merge",
    )(oa, ob, gates, gates, x2d, w_a, w_b, w_o, g_ffn)


FFN_OUT_CHUNK = 512


def _ffn_body(hf_ref, x1_hbm, wgu_ref, wd_ref, g_ref, y_ref, x1_sem, *, final_norm, tm):
    i, f = pl.program_id(0), pl.program_id(1)
    x1_copy = pltpu.make_async_copy(x1_hbm.at[pl.ds(i * tm, tm), :], y_ref, x1_sem)

    @pl.when(f == 0)
    def _():
        x1_copy.start()

    tf = wd_ref.shape[0]
    gu = jnp.dot(hf_ref[...], wgu_ref[...], preferred_element_type=F32)
    ab = (jax.nn.silu(gu[:, :tf]) * gu[:, tf:]).astype(BF16)

    @pl.when(f == 0)
    def _():
        x1_copy.wait()

    for c0 in range(0, D_MODEL, FFN_OUT_CHUNK):
        cols = slice(c0, c0 + FFN_OUT_CHUNK)
        y_ref[:, cols] += jnp.dot(ab, wd_ref[:, cols], preferred_element_type=F32)

    if final_norm:
        @pl.when(f == pl.num_programs(1) - 1)
        def _():
            x2 = y_ref[...]
            y_ref[...] = (x2 * lax.rsqrt(jnp.mean(x2 * x2, axis=-1, keepdims=True) + EPS)) * g_ref[...]


def _interleave_gate_up(w_gate, w_up, tf):
    d = w_gate.shape[0]
    chunks = lambda w: w.reshape(d, D_FF // tf, tf)
    return jnp.concatenate([chunks(w_gate), chunks(w_up)], axis=2).reshape(d, 2 * D_FF)


def _ffn_call(hf, x1, w_gate_up, w_down, g_final, tm, tf, final_norm):
    m = hf.shape[0]
    return pl.pallas_call(
        functools.partial(_ffn_body, final_norm=final_norm, tm=tm),
        grid=(m // tm, D_FF // tf),
        in_specs=[pl.BlockSpec((tm, D_MODEL), lambda i, f: (i, 0)),
                  pl.BlockSpec(memory_space=pl.ANY),
                  pl.BlockSpec((D_MODEL, 2 * tf), lambda i, f: (0, f)),
                  pl.BlockSpec((tf, D_MODEL), lambda i, f: (f, 0)),
                  _const_spec((1, D_MODEL))],
        out_specs=pl.BlockSpec((tm, D_MODEL), lambda i, f: (i, 0)),
        out_shape=jax.ShapeDtypeStruct((m, D_MODEL), F32),
        scratch_shapes=[pltpu.SemaphoreType.DMA(())],
        compiler_params=_cparams("parallel", "arbitrary"),
        name="ffn",
    )(hf, x1, w_gate_up, w_down, g_final)


def _rope_tables(pos, dim):
    half = dim // 2
    inv = 1.0 / (ROPE_THETA ** (jnp.arange(half, dtype=F32) * (2.0 / dim)))
    ang = pos[:, None] * inv[None, :]
    cos, sin = jnp.cos(ang), jnp.sin(ang)
    reps = LANES // dim
    return (jnp.tile(jnp.concatenate([cos, cos], axis=-1), (1, reps)),
            jnp.tile(jnp.concatenate([-sin, sin], axis=-1), (1, reps)))


def _split_w_in(w_in):
    o = IN_OFFS
    seg = lambda i, j: w_in[:, o[i]:o[j]].astype(BF16)
    pad = lambda w: jnp.pad(w, ((0, 0), (0, LANES - w.shape[1])))
    return seg(0, 4), jnp.concatenate([pad(seg(4, 5)), pad(seg(5, 6))], axis=1), seg(6, 7), seg(7, 8)


def _layer(x, pos, past, ssm0, lw):
    nb, t_len, _ = x.shape
    m = nb * t_len
    x2d = x.reshape(m, D_MODEL)

    cq, sq = _rope_tables(pos, HEAD_DIM)
    ci, si = _rope_tables(pos, IDX_DIM)
    tabs = (cq, sq, ci, si)
    if t_len < PROJ_ROWS:
        tabs = tuple(jnp.tile(t, (PROJ_ROWS // t_len, 1)) for t in tabs)

    h, q, k, v, kb, vb, iq, ik, iw, u = _proj_call(
        x2d, lw['g_mix'], lw['w_att'], lw['w_idx'], lw['w_u'], tabs, PROJ_ROWS)
    gates = _gates_call(h, lw['w_gl'], PROJ_ROWS)

    if past is None:
        s_chunk = DSA_GROUP_BLOCKS * CHUNK
        kb_all = kb.reshape(nb, t_len, KV_WIDTH)
        vb_all = vb.reshape(nb, t_len, KV_WIDTH)
        ikbd = _indexer_key_blocks(ik.reshape(nb, t_len, IDX_DIM), s_chunk)
        n_blk = t_len // CHUNK
        o_a = jnp.zeros((m, ATT_WIDTH), BF16)
        for blk0 in range(0, n_blk, DSA_GROUP_BLOCKS):
            s_keys = (blk0 + DSA_GROUP_BLOCKS) * CHUNK
            qrows = DSA_GROUP_BLOCKS * CHUNK if s_keys <= DSA_WIDE_STEP_MAX_KEYS else QROWS
            o_a = _dsa_call(q, iq, iw, kb_all, vb_all, ikbd, o_a, qrows, blk0, DSA_GROUP_BLOCKS,
                            n_blk, s_keys, s_chunk, min(TOPK_MAX, t_len // 4))
    else:
        pk, pv, pik = past
        assert t_len == CHUNK, "the cached stream is one query block per sequence"
        l_keys = pk.shape[1] + t_len
        o_a = _dsa_cached_call(q, iq, iw, kb, vb, ik, pk.reshape(nb, -1, HEAD_DIM), pv.reshape(nb, -1, HEAD_DIM),
                               pik, min(TOPK_MAX, l_keys // 4))

    x0_re, x0_im = ssm0
    o_b, s_re, s_im = _s5_call(
        u.reshape(nb, t_len, S5_WIDTH), x0_re.reshape(nb, S5_LANES), x0_im.reshape(nb, S5_LANES),
        lw['ab_re'], lw['ab_im'], lw['b_re_bd'], lw['b_im_bd'], lw['c_re_bd'], lw['c_im_bd'],
        lw['d_skip'], lw['w_glu'], lw['b_glu'], min(t_len, S5_STEP_ROWS // nb))

    x1, hf = _merge_call(o_a, o_b.reshape(m, S5_WIDTH), gates, x2d, lw['w_proj_a'], lw['w_proj_b'],
                         lw['w_out'], lw['g_ffn'], MERGE_ROWS)
    caches = (k.reshape(nb, t_len, N_KV_HEADS, HEAD_DIM), v.reshape(nb, t_len, N_KV_HEADS, HEAD_DIM),
              ik.reshape(nb, t_len, IDX_DIM), s_re.reshape(nb, S5_GROUPS, S5_STATE),
              s_im.reshape(nb, S5_GROUPS, S5_STATE))
    return x1, hf, caches


def kernel(x_prompt, x_sample, cache_k, cache_v, cache_idx_k, state_ssm_re, state_ssm_im,
           g_mix, w_in, a_re, a_im, log_dt, b_re, b_im, c_re, c_im, d_skip, w_glu, b_glu,
           w_proj_a, w_proj_b, w_out, g_ffn, w_gate, w_up, w_down, g_final):
    depth = w_in.shape[0]
    t_p, t_s = x_prompt.shape[1], x_sample.shape[1]
    past_len = cache_k.shape[2]
    pos_p = jnp.arange(t_p, dtype=F32)
    pos_s = past_len + jnp.arange(t_s, dtype=F32)
    nb_p, nb_s = x_prompt.shape[0], x_sample.shape[0]
    g_fin = g_final.reshape(1, D_MODEL)

    hp, hs = x_prompt, x_sample
    outs_p, outs_s = [], []
    for l in range(depth):
        w_att, w_idx, w_u, w_gl = _split_w_in(w_in[l])
        ab_re, ab_im, bb_re, bb_im = _s5prep_call(a_re[l], a_im[l], log_dt[l], b_re[l], b_im[l])
        per_group = lambda bb: jnp.transpose(bb.reshape(S5_CH, S5_GROUPS, S5_STATE), (1, 0, 2))
        lw = {
            'g_mix': g_mix[l].reshape(1, D_MODEL), 'w_att': w_att, 'w_idx': w_idx, 'w_u': w_u, 'w_gl': w_gl,
            'ab_re': ab_re.reshape(1, S5_LANES), 'ab_im': ab_im.reshape(1, S5_LANES),
            'b_re_bd': _block_diag_slabs(per_group(bb_re)).astype(BF16),
            'b_im_bd': _block_diag_slabs(per_group(bb_im)).astype(BF16),
            'c_re_bd': _block_diag_slabs(jnp.transpose(c_re[l], (0, 2, 1))).astype(BF16),
            'c_im_bd': _block_diag_slabs(jnp.transpose(c_im[l], (0, 2, 1))).astype(BF16),
            'd_skip': d_skip[l].reshape(1, S5_WIDTH), 'w_glu': w_glu[l].astype(BF16),
            'b_glu': b_glu[l].reshape(1, S5_WIDTH),
            'w_proj_a': w_proj_a[l].astype(BF16), 'w_proj_b': w_proj_b[l].astype(BF16),
            'w_out': w_out[l].astype(BF16), 'g_ffn': g_ffn[l].reshape(1, D_MODEL),
        }
        wgu = _interleave_gate_up(w_gate[l].astype(BF16), w_up[l].astype(BF16), FFN_FF_TILE)
        wd = w_down[l].astype(BF16)
        last = l == depth - 1

        zeros = jnp.zeros((nb_p, S5_GROUPS, S5_STATE), F32)
        x1p, hfp, cp = _layer(hp, pos_p, None, (zeros, zeros), lw)
        x1s, hfs, cs = _layer(hs, pos_s, (cache_k[l], cache_v[l], cache_idx_k[l]),
                              (state_ssm_re[l], state_ssm_im[l]), lw)
        yp = _ffn_call(hfp, x1p, wgu, wd, g_fin, FFN_ROWS, FFN_FF_TILE, last)
        ys = _ffn_call(hfs, x1s, wgu, wd, g_fin, FFN_ROWS, FFN_FF_TILE, last)
        hp = yp.reshape(nb_p, t_p, D_MODEL)
        hs = ys.reshape(nb_s, t_s, D_MODEL)
        outs_p.append(cp)
        outs_s.append(cs)

    stack = lambda outs, i: jnp.stack([o[i] for o in outs])
    return (hp, hs,
            stack(outs_p, 0), stack(outs_p, 1), stack(outs_p, 2), stack(outs_p, 3), stack(outs_p, 4),
            stack(outs_s, 0), stack(outs_s, 1), stack(outs_s, 2), stack(outs_s, 3), stack(outs_s, 4))
```

```python
import functools
import math

import jax
import jax.numpy as jnp
import numpy as np
from jax import lax
from jax.experimental import pallas as pl
from jax.experimental.pallas import tpu as pltpu

F32 = jnp.float32
BF16 = jnp.bfloat16

D_MODEL = 2048
CHUNK = 64
HEAD_DIM = 128
N_HEADS = 8
N_KV_HEADS = 2
GROUP = N_HEADS // N_KV_HEADS
ATT_WIDTH = N_HEADS * HEAD_DIM
KV_WIDTH = N_KV_HEADS * HEAD_DIM
IDX_HEADS = 16
IDX_DIM = 64
IDX_WIDTH = IDX_HEADS * IDX_DIM
TOPK_MAX = 256
S5_CH = 16
S5_WIDTH = D_MODEL // 2
S5_GROUPS = S5_WIDTH // S5_CH
S5_STATE = 64
S5_LANES = S5_GROUPS * S5_STATE
D_FF = 5632
ROPE_THETA = 10000.0
EPS = 1e-6
IN_SIZES = (ATT_WIDTH, KV_WIDTH, KV_WIDTH, IDX_WIDTH, IDX_DIM, IDX_HEADS, S5_WIDTH, 2 * D_MODEL)
IN_OFFS = tuple(int(s) for s in np.cumsum((0,) + IN_SIZES))

LANES = 128
S5_SLAB_GROUPS = LANES // S5_CH
S5_SLABS = S5_GROUPS // S5_SLAB_GROUPS
S5_SLAB_STATE = S5_SLAB_GROUPS * S5_STATE
INT_MIN = -2 ** 31
DSA_GROUP_BLOCKS = 4
QROWS = 2 * CHUNK
DSA_WIDE_STEP_MAX_KEYS = 1280
S5_STEP_ROWS = 512
SEARCH_PART = 128

VMEM_LIMIT = 56 * 2 ** 20
PROJ_ROWS = 512
MERGE_ROWS = 256
FFN_ROWS = 1024
FFN_FF_TILE = 512


def _cparams(*sem):
    return pltpu.CompilerParams(dimension_semantics=sem, vmem_limit_bytes=VMEM_LIMIT)


def _const_spec(shape):
    nd = len(shape)
    return pl.BlockSpec(shape, lambda *_: (0,) * nd, pipeline_mode=pl.Buffered(1))


_PQ = 0
_PK = _PQ + ATT_WIDTH
_PV = _PK + KV_WIDTH
_PIQ = _PV + KV_WIDTH
_PEND = _PIQ + IDX_WIDTH
IW_SCALE = (IDX_DIM ** -0.5) * (IDX_HEADS ** -0.5)


def _rope128(z, cos, sin):
    return z * cos + pltpu.roll(z, HEAD_DIM // 2, 1) * sin


def _rope64(z, cos, sin, low_half):
    partner = jnp.where(low_half, pltpu.roll(z, LANES - IDX_DIM // 2, 1), pltpu.roll(z, IDX_DIM // 2, 1))
    return z * cos + partner * sin


def _proj_body(x_ref, g_ref, wa_ref, wi_ref, wu_ref, cq_ref, sq_ref, ci_ref, si_ref,
               h_ref, q_ref, k_ref, v_ref, kb_ref, vb_ref, iq_ref, ik_ref, iw_ref, u_ref):
    x = x_ref[...]
    tm = x.shape[0]
    h = (x * lax.rsqrt(jnp.mean(x * x, axis=-1, keepdims=True) + EPS)) * g_ref[...]
    hb = h.astype(BF16)
    h_ref[...] = hb

    def proj(lo, hi):
        return jnp.dot(hb, wa_ref[:, lo:hi], preferred_element_type=F32)

    cq, sq, ci, si = cq_ref[...], sq_ref[...], ci_ref[...], si_ref[...]
    low_half = (lax.broadcasted_iota(jnp.int32, cq.shape, 1) & (IDX_DIM - 1)) < (IDX_DIM // 2)

    zq = proj(_PQ, _PK)
    for hd in range(N_HEADS):
        sl = slice(hd * HEAD_DIM, (hd + 1) * HEAD_DIM)
        q_ref[:, sl] = _rope128(zq[:, sl], cq, sq).astype(BF16)
    zk = proj(_PK, _PV)
    zv = proj(_PV, _PIQ)
    vb_ref[...] = zv.astype(BF16)
    for hd in range(N_KV_HEADS):
        sl = slice(hd * HEAD_DIM, (hd + 1) * HEAD_DIM)
        head_rows = pl.ds(hd, tm, stride=N_KV_HEADS)
        r = _rope128(zk[:, sl], cq, sq)
        k_ref[head_rows, :] = r
        kb_ref[:, sl] = r.astype(BF16)
        v_ref[head_rows, :] = zv[:, sl]
    ziq = proj(_PIQ, _PEND)
    for p in range(IDX_WIDTH // LANES):
        sl = slice(p * LANES, (p + 1) * LANES)
        iq_ref[:, sl] = _rope64(ziq[:, sl], ci, si, low_half).astype(BF16)
    z2 = jnp.dot(hb, wi_ref[...], preferred_element_type=F32)
    ik_ref[...] = _rope64(z2[:, :LANES], ci, si, low_half)[:, :IDX_DIM]
    iw_ref[...] = z2[:, LANES:] * IW_SCALE
    u_ref[...] = jnp.dot(hb, wu_ref[...], preferred_element_type=F32)


def _proj_call(x2d, g_mix, w_att, w_idx, w_u, tabs, tm):
    m = x2d.shape[0]
    n_tiles = m // tm
    tab_tiles = tabs[0].shape[0] // tm
    row = lambda w: pl.BlockSpec((tm, w), lambda i: (i, 0))
    tab_spec = pl.BlockSpec((tm, LANES), lambda i: (i % tab_tiles, 0))
    kv_cache = jax.ShapeDtypeStruct((m * N_KV_HEADS, HEAD_DIM), F32)
    kv_cache_spec = pl.BlockSpec((tm * N_KV_HEADS, HEAD_DIM), lambda i: (i, 0))
    out_shape = (
        jax.ShapeDtypeStruct((m, D_MODEL), BF16),
        jax.ShapeDtypeStruct((m, ATT_WIDTH), BF16),
        kv_cache,
        kv_cache,
        jax.ShapeDtypeStruct((m, KV_WIDTH), BF16),
        jax.ShapeDtypeStruct((m, KV_WIDTH), BF16),
        jax.ShapeDtypeStruct((m, IDX_WIDTH), BF16),
        jax.ShapeDtypeStruct((m, IDX_DIM), F32),
        jax.ShapeDtypeStruct((m, LANES), F32),
        jax.ShapeDtypeStruct((m, S5_WIDTH), F32),
    )
    out_specs = (row(D_MODEL), row(ATT_WIDTH), kv_cache_spec, kv_cache_spec, row(KV_WIDTH), row(KV_WIDTH),
                 row(IDX_WIDTH), row(IDX_DIM), row(LANES), row(S5_WIDTH))
    return pl.pallas_call(
        _proj_body,
        grid=(n_tiles,),
        in_specs=[row(D_MODEL), _const_spec((1, D_MODEL)),
                  _const_spec(w_att.shape), _const_spec(w_idx.shape), _const_spec(w_u.shape),
                  tab_spec, tab_spec, tab_spec, tab_spec],
        out_specs=out_specs,
        out_shape=out_shape,
        compiler_params=_cparams("parallel"),
        name="proj",
    )(x2d, g_mix, w_att, w_idx, w_u, *tabs)


def _gates_body(h_ref, w_ref, o_ref):
    o_ref[...] = jax.nn.sigmoid(jnp.dot(h_ref[...], w_ref[...], preferred_element_type=F32))


def _gates_call(h, w_gl, tm):
    m, n = h.shape[0], w_gl.shape[1]
    return pl.pallas_call(
        _gates_body,
        grid=(m // tm,),
        in_specs=[pl.BlockSpec((tm, D_MODEL), lambda i: (i, 0)), _const_spec(w_gl.shape)],
        out_specs=pl.BlockSpec((tm, n), lambda i: (i, 0)),
        out_shape=jax.ShapeDtypeStruct((m, n), F32),
        compiler_params=_cparams("parallel"),
        name="gates",
    )(h, w_gl)


LOG2_E = math.log2(math.e)
NEG_INF_KEY = INT_MIN + 0x7FFFFF
F32_MAX = float(np.finfo(np.float32).max)


def _key_to_f32(key):
    return lax.bitcast_convert_type(key ^ ((key >> 31) & jnp.int32(0x7FFFFFFF)), F32)


def _row_count(mask):
    return jnp.sum(jnp.where(mask, 1.0, 0.0), axis=1, keepdims=True)


def _resolve_threshold_ties(score, t_lo, t_next, excess, kf, s_keys, bias_ref, cand_ref, pick_ref, rem_ref):
    tied_row = excess > 0.0
    ge = score >= t_lo
    above = score >= t_next
    cand_ref[...] = jnp.where(ge & jnp.logical_not(above) & tied_row, 1.0, 0.0)
    pick_ref[...] = jnp.zeros_like(score)
    rem0 = jnp.where(tied_row, kf - _row_count(above), 0.0)
    rem_ref[...] = jnp.broadcast_to(rem0, rem_ref.shape)
    col = lax.broadcasted_iota(jnp.int32, score.shape, 1)

    def take_next_value(_):
        cand = cand_ref[...] > 0.5
        rem = rem_ref[:, :1]
        top = jnp.max(jnp.where(cand, score, -jnp.inf), axis=1, keepdims=True)
        eq = cand & (score == top)
        last = jnp.zeros((score.shape[0], 1), jnp.int32)
        for b in range(int(s_keys).bit_length() - 1, -1, -1):
            nxt = last + jnp.int32(1 << b)
            last = jnp.where(_row_count(eq & (col < nxt)) < rem, nxt, last)
        take = eq & (col <= last) & (rem > 0.0)
        pick_ref[...] = jnp.where(take, 1.0, pick_ref[...])
        cand_ref[...] = jnp.where(eq, 0.0, cand_ref[...])
        rem = jnp.where(top > -jnp.inf, rem - _row_count(take), 0.0)
        rem_ref[...] = jnp.broadcast_to(rem, rem_ref.shape)
        return jnp.max(rem) > 0.0

    lax.while_loop(lambda go: go, take_next_value, jnp.max(rem0) > 0.0)
    chosen = above | (pick_ref[...] > 0.5)
    bias_ref[...] = jnp.where(tied_row, jnp.where(chosen, 0.0, -jnp.inf), bias_ref[...])


def _dsa_core(q_ref, iq_ref, iw_ref, get_k, get_v, get_ikbd, n_keysets,
              o_ref, score_ref, bias_ref, cand_ref, pick_ref, rem_ref, *,
              s_keys, s_chunk, blk0, fixed_valid, topk):
    qrows = q_ref.shape[0]
    halves = [(slice(hf * CHUNK, (hf + 1) * CHUNK), hf % n_keysets) for hf in range(qrows // CHUNK)]

    pairs = IDX_WIDTH // LANES
    for hf, (rows, ks) in enumerate(halves):
        if fixed_valid is None:
            valid = (blk0 + len(halves) * pl.program_id(1) + hf + 1) * CHUNK
        else:
            valid = fixed_valid
        iq = iq_ref[rows, :]
        lhs = jnp.concatenate([iq[:, p * LANES:(p + 1) * LANES] for p in range(pairs)], axis=0)
        iw = iw_ref[rows, :]
        for c in range(s_keys // s_chunk):
            logits = lax.dot_general(lhs, get_ikbd(ks, c), (((1,), (1,)), ((), ())), preferred_element_type=F32)
            acc = jnp.zeros((CHUNK, s_chunk), F32)
            for p in range(pairs):
                lp = logits[p * CHUNK:(p + 1) * CHUNK]
                acc = acc + jnp.maximum(lp[:, :s_chunk], 0.0) * iw[:, 2 * p:2 * p + 1]
                acc = acc + jnp.maximum(lp[:, s_chunk:], 0.0) * iw[:, 2 * p + 1:2 * p + 2]
            col = c * s_chunk + lax.broadcasted_iota(jnp.int32, acc.shape, 1)
            score_ref[rows, c * s_chunk:(c + 1) * s_chunk] = jnp.where(col < valid, acc, -jnp.inf)

    score = score_ref[...]
    if s_keys <= topk:
        bias_ref[...] = jnp.where(score > -jnp.inf, 0.0, -jnp.inf)
    else:
        kf = float(topk)
        score_t = score.T

        def count_ge(t):
            ind = jnp.where(score_t >= t, 1.0, 0.0).reshape(s_keys // SEARCH_PART, SEARCH_PART, qrows)
            return jnp.sum(jnp.sum(ind, axis=0), axis=0, keepdims=True)

        thr = jnp.full((1, qrows), INT_MIN, jnp.int32)
        for b in range(31, -1, -1):
            cand = thr + jnp.int32(INT_MIN if b == 31 else 1 << b)
            thr = jnp.where(count_ge(_key_to_f32(cand)) >= kf, cand, thr)
        thr = jnp.maximum(thr, jnp.int32(NEG_INF_KEY))
        t_lo = jnp.maximum(_key_to_f32(thr), -F32_MAX)
        excess = count_ge(t_lo) - kf
        stats = jnp.concatenate([t_lo, _key_to_f32(thr + 1), excess, jnp.zeros((qrows - 3, qrows), F32)], axis=0).T
        t_lo_col = stats[:, 0:1]
        bias_ref[...] = jnp.where(score >= t_lo_col, 0.0, -jnp.inf)

        @pl.when(jnp.max(excess) > 0.0)
        def _():
            _resolve_threshold_ties(score, t_lo_col, stats[:, 1:2], stats[:, 2:3], kf, s_keys,
                                    bias_ref, cand_ref, pick_ref, rem_ref)

    scale = HEAD_DIM ** -0.5
    for rows, ks in halves:
        bias = bias_ref[rows, :]
        q = q_ref[rows, :]
        for c in range(N_KV_HEADS):
            kc = get_k(ks, c)
            vc = get_v(ks, c)
            qc = jnp.concatenate(
                [q[:, (c * GROUP + g) * HEAD_DIM:(c * GROUP + g + 1) * HEAD_DIM] for g in range(GROUP)], axis=0)
            logits = lax.dot_general(qc, kc, (((1,), (1,)), ((), ())), preferred_element_type=F32)
            es, inv = [], []
            for g in range(GROUP):
                lg = logits[g * CHUNK:(g + 1) * CHUNK] + bias
                e = jnp.exp2((lg - jnp.max(lg, axis=1, keepdims=True)) * (scale * LOG2_E))
                inv.append(1.0 / jnp.sum(e, axis=1, keepdims=True))
                es.append(e.astype(BF16))
            oc = jnp.dot(jnp.concatenate(es, axis=0), vc, preferred_element_type=F32)
            for g in range(GROUP):
                hd = c * GROUP + g
                o_ref[rows, hd * HEAD_DIM:(hd + 1) * HEAD_DIM] = (
                    oc[g * CHUNK:(g + 1) * CHUNK] * inv[g]).astype(BF16)


def _dsa_body(q_ref, iq_ref, iw_ref, kb_ref, vb_ref, ikbd_ref, *rest, **static):
    head = lambda ref: (lambda ks, c: ref[ks, :, c * HEAD_DIM:(c + 1) * HEAD_DIM])
    _dsa_core(q_ref, iq_ref, iw_ref, head(kb_ref), head(vb_ref), lambda ks, c: ikbd_ref[ks, c],
              kb_ref.shape[0], *rest[-6:], **static)


def _dsa_cached_body(q_ref, iq_ref, iw_ref, kn_ref, vn_ref, ikn_ref, pk_ref, pv_ref, pik_ref,
                     o_ref, k_all, v_all, ikbd_all, *scratch, past_len, **static):
    s_keys = static['s_keys']
    new_end = past_len + CHUNK
    for ks in range(2):
        new = slice(ks * CHUNK, (ks + 1) * CHUNK)
        for c in range(N_KV_HEADS):
            for dst, past, fresh in ((k_all, pk_ref, kn_ref), (v_all, pv_ref, vn_ref)):
                dst[ks, c, :past_len, :] = past[ks, pl.ds(c, past_len, stride=N_KV_HEADS), :].astype(BF16)
                dst[ks, c, past_len:new_end, :] = fresh[new, c * HEAD_DIM:(c + 1) * HEAD_DIM]
                dst[ks, c, new_end:, :] = jnp.zeros((s_keys - new_end, HEAD_DIM), BF16)
        ik = jnp.concatenate([pik_ref[ks].astype(BF16), ikn_ref[new, :].astype(BF16),
                              jnp.zeros((s_keys - new_end, IDX_DIM), BF16)], axis=0)
        z = jnp.zeros_like(ik)
        ikbd_all[ks, :s_keys, :] = jnp.concatenate([ik, z], axis=1)
        ikbd_all[ks, s_keys:, :] = jnp.concatenate([z, ik], axis=1)
    _dsa_core(q_ref, iq_ref, iw_ref, lambda ks, c: k_all[ks, c], lambda ks, c: v_all[ks, c],
              lambda ks, c: ikbd_all[ks], 2, o_ref, *scratch, **static)


def _dsa_cached_call(q, iq, iw, kb, vb, ik, pk, pv, pik, topk):
    nb, past_len = pik.shape[0], pik.shape[1]
    l_keys = past_len + CHUNK
    s_keys = -(-l_keys // LANES) * LANES
    body = functools.partial(_dsa_cached_body, past_len=past_len, s_keys=s_keys, s_chunk=s_keys, blk0=0,
                             fixed_valid=l_keys, topk=topk)
    qrow = lambda w: pl.BlockSpec((QROWS, w), lambda n: (n, 0))
    pair = lambda a: pl.BlockSpec((2,) + a.shape[1:], lambda n: (n, 0, 0))
    mask_buf = pltpu.VMEM((QROWS, s_keys), F32)
    heads = pltpu.VMEM((2, N_KV_HEADS, s_keys, HEAD_DIM), BF16)
    return pl.pallas_call(
        body,
        grid=(nb // 2,),
        in_specs=[qrow(ATT_WIDTH), qrow(IDX_WIDTH), qrow(LANES), qrow(KV_WIDTH), qrow(KV_WIDTH), qrow(IDX_DIM),
                  pair(pk), pair(pv), pair(pik)],
        out_specs=qrow(ATT_WIDTH),
        out_shape=jax.ShapeDtypeStruct((nb * CHUNK, ATT_WIDTH), BF16),
        scratch_shapes=[heads, heads, pltpu.VMEM((2, 2 * s_keys, LANES), BF16),
                        mask_buf, mask_buf, mask_buf, mask_buf, pltpu.VMEM((QROWS, LANES), F32)],
        compiler_params=_cparams("parallel"),
        name="dsa_cached",
    )(q, iq, iw, kb, vb, ik, pk, pv, pik)


def _dsa_call(q, iq, iw, kb, vb, ikbd, o_prev, qrows, blk0, n_blk, blks_per_seq, s_keys, s_chunk, topk):
    nb = kb.shape[0]
    blocks_per_step = qrows // CHUNK
    steps_per_seq = blks_per_seq // blocks_per_step
    qrow = lambda w: pl.BlockSpec(
        (qrows, w), lambda n, j: (n * steps_per_seq + blk0 // blocks_per_step + j, 0))
    n_chunks = s_keys // s_chunk
    body = functools.partial(_dsa_body, s_keys=s_keys, s_chunk=s_chunk, blk0=blk0, fixed_valid=None, topk=topk)
    mask_buf = pltpu.VMEM((qrows, s_keys), F32)
    return pl.pallas_call(
        body,
        grid=(nb, n_blk // blocks_per_step),
        in_specs=[qrow(ATT_WIDTH), qrow(IDX_WIDTH), qrow(LANES),
                  pl.BlockSpec((1, s_keys, KV_WIDTH), lambda n, j: (n, 0, 0)),
                  pl.BlockSpec((1, s_keys, KV_WIDTH), lambda n, j: (n, 0, 0)),
                  pl.BlockSpec((1, n_chunks, 2 * s_chunk, LANES), lambda n, j: (n, 0, 0, 0)),
                  pl.BlockSpec(memory_space=pl.ANY)],
        out_specs=qrow(ATT_WIDTH),
        out_shape=jax.ShapeDtypeStruct((nb * blks_per_seq * CHUNK, ATT_WIDTH), BF16),
        scratch_shapes=[mask_buf, mask_buf, mask_buf, mask_buf, pltpu.VMEM((qrows, LANES), F32)],
        input_output_aliases={6: 0},
        compiler_params=_cparams("parallel", "arbitrary"),
        name="dsa",
    )(q, iq, iw, kb, vb, ikbd, o_prev)


def _indexer_key_blocks(ik, s_chunk):
    nb, s, _ = ik.shape
    ikb = ik.astype(BF16).reshape(nb, s // s_chunk, s_chunk, IDX_DIM)
    z = jnp.zeros_like(ikb)
    return jnp.concatenate([jnp.concatenate([ikb, z], axis=-1), jnp.concatenate([z, ikb], axis=-1)], axis=-2)


def _s5prep_body(are_ref, aim_ref, ldt_ref, bre_ref, bim_ref, abre_ref, abim_ref, bbre_ref, bbim_ref):
    a_re, a_im = are_ref[...], aim_ref[...]
    dt = jnp.exp(ldt_ref[...])
    mag = jnp.exp(dt * a_re)
    ab_re = mag * jnp.cos(dt * a_im)
    ab_im = mag * jnp.sin(dt * a_im)
    den = a_re * a_re + a_im * a_im
    f_re = ((ab_re - 1.0) * a_re + ab_im * a_im) / den
    f_im = (ab_im * a_re - (ab_re - 1.0) * a_im) / den
    abre_ref[...] = ab_re
    abim_ref[...] = ab_im
    for c in range(S5_CH):
        b_re, b_im = bre_ref[c], bim_ref[c]
        bbre_ref[c] = f_re * b_re - f_im * b_im
        bbim_ref[c] = f_re * b_im + f_im * b_re


def _s5prep_call(a_re, a_im, log_dt, b_re, b_im):
    rows = S5_LANES // LANES
    flat = lambda a: a.reshape(rows, LANES)
    ldt = jnp.broadcast_to(log_dt[:, None], (S5_GROUPS, S5_STATE))
    chan_major = lambda b: jnp.transpose(b, (2, 0, 1)).reshape(S5_CH, rows, LANES)
    small = jax.ShapeDtypeStruct((rows, LANES), F32)
    big = jax.ShapeDtypeStruct((S5_CH, rows, LANES), F32)
    return pl.pallas_call(
        _s5prep_body,
        out_shape=(small, small, big, big),
        name="s5prep",
    )(flat(a_re), flat(a_im), flat(ldt), chan_major(b_re), chan_major(b_im))


def _gelu_tanh(x):
    return 0.5 * x * (1.0 + jnp.tanh(math.sqrt(2.0 / math.pi) * (x + 0.044715 * (x * x * x))))


def _s5_body(u_ref, x0re_ref, x0im_ref, abre_ref, abim_ref, bre_ref, bim_ref, cre_ref, cim_ref,
             dskip_ref, wglu_ref, bglu_ref, ob_ref, sre_ref, sim_ref, utm, otm, xre, xim, st_re, st_im,
             *, tc, nb):
    step = pl.program_id(0)

    @pl.when(step == 0)
    def _():
        st_re[...] = x0re_ref[...]
        st_im[...] = x0im_ref[...]

    for n in range(nb):
        for k in range(S5_SLABS):
            utm[k, pl.ds(n, tc, stride=nb), :] = u_ref[n, :, k * LANES:(k + 1) * LANES]

    def project_in(k):
        us = utm[k].astype(BF16)
        sl = slice(k * S5_SLAB_STATE, (k + 1) * S5_SLAB_STATE)
        xre[:, :, sl] = jnp.dot(us, bre_ref[k], preferred_element_type=F32).reshape(tc, nb, S5_SLAB_STATE)
        xim[:, :, sl] = jnp.dot(us, bim_ref[k], preferred_element_type=F32).reshape(tc, nb, S5_SLAB_STATE)

    def scan(k):
        sl = slice(k * S5_SLAB_STATE, (k + 1) * S5_SLAB_STATE)
        a_r = jnp.broadcast_to(abre_ref[:, sl], (nb, S5_SLAB_STATE))
        a_i = jnp.broadcast_to(abim_ref[:, sl], (nb, S5_SLAB_STATE))
        s_r, s_i = st_re[:, sl], st_im[:, sl]
        for t in range(tc):
            s_r, s_i = (a_r * s_r - a_i * s_i + xre[t, :, sl],
                        a_r * s_i + a_i * s_r + xim[t, :, sl])
            xre[t, :, sl] = s_r
            xim[t, :, sl] = s_i
        st_re[:, sl] = s_r
        st_im[:, sl] = s_i

    def project_out(k):
        sl = slice(k * S5_SLAB_STATE, (k + 1) * S5_SLAB_STATE)
        xr = xre[:, :, sl].reshape(tc * nb, S5_SLAB_STATE).astype(BF16)
        xi = xim[:, :, sl].reshape(tc * nb, S5_SLAB_STATE).astype(BF16)
        return (jnp.dot(xr, cre_ref[k], preferred_element_type=F32)
                - jnp.dot(xi, cim_ref[k], preferred_element_type=F32)
                + dskip_ref[:, k * LANES:(k + 1) * LANES] * utm[k])

    project_in(0)
    ys = []
    for k in range(S5_SLABS):
        if k + 1 < S5_SLABS:
            project_in(k + 1)
        scan(k)
        ys.append(project_out(k))
    yb = _gelu_tanh(jnp.concatenate(ys, axis=1))
    gate = jax.nn.sigmoid(jnp.dot(yb.astype(BF16), wglu_ref[...], preferred_element_type=F32) + bglu_ref[...])
    o = yb * gate
    for k in range(S5_SLABS):
        otm[k] = o[:, k * LANES:(k + 1) * LANES]
    for n in range(nb):
        for k in range(S5_SLABS):
            ob_ref[n, :, k * LANES:(k + 1) * LANES] = otm[k, pl.ds(n, tc, stride=nb), :].astype(BF16)

    @pl.when(step == pl.num_programs(0) - 1)
    def _():
        sre_ref[...] = st_re[...]
        sim_ref[...] = st_im[...]


def _s5_call(u, x0_re, x0_im, ab_re, ab_im, b_re_bd, b_im_bd, c_re_bd, c_im_bd, d_skip, w_glu, b_glu, tc):
    nb, t_len, _ = u.shape
    body = functools.partial(_s5_body, tc=tc, nb=nb)
    state = jax.ShapeDtypeStruct((nb, S5_LANES), F32)
    seq_spec = pl.BlockSpec((nb, tc, S5_WIDTH), lambda i: (0, i, 0))
    slabs = pltpu.VMEM((S5_SLABS, tc * nb, LANES), F32)
    return pl.pallas_call(
        body,
        grid=(t_len // tc,),
        in_specs=[seq_spec,
                  _const_spec((nb, S5_LANES)), _const_spec((nb, S5_LANES)),
                  _const_spec((1, S5_LANES)), _const_spec((1, S5_LANES)),
                  _const_spec(b_re_bd.shape), _const_spec(b_im_bd.shape),
                  _const_spec(c_re_bd.shape), _const_spec(c_im_bd.shape),
                  _const_spec((1, S5_WIDTH)), _const_spec((S5_WIDTH, S5_WIDTH)), _const_spec((1, S5_WIDTH))],
        out_specs=(seq_spec, _const_spec((nb, S5_LANES)), _const_spec((nb, S5_LANES))),
        out_shape=(jax.ShapeDtypeStruct((nb, t_len, S5_WIDTH), BF16), state, state),
        scratch_shapes=[slabs, slabs,
                        pltpu.VMEM((tc, nb, S5_LANES), F32), pltpu.VMEM((tc, nb, S5_LANES), F32),
                        pltpu.VMEM((nb, S5_LANES), F32), pltpu.VMEM((nb, S5_LANES), F32)],
        compiler_params=_cparams("arbitrary"),
        name="s5",
    )(u, x0_re, x0_im, ab_re, ab_im, b_re_bd, b_im_bd, c_re_bd, c_im_bd, d_skip, w_glu, b_glu)


def _block_diag_slabs(w):
    g, r, c = w.shape
    w = w.reshape(S5_SLABS, S5_SLAB_GROUPS, r, c)
    eye = jnp.eye(S5_SLAB_GROUPS, dtype=w.dtype)
    bd = w[:, :, :, None, :] * eye[None, :, None, :, None]
    return bd.reshape(S5_SLABS, S5_SLAB_GROUPS * r, S5_SLAB_GROUPS * c)


def _merge_body(oa_ref, ob_ref, ga_ref, gb_ref, x_ref, wa_ref, wb_ref, wo_ref, g_ref, x1_ref, hf_ref):
    pa = jnp.dot(oa_ref[...], wa_ref[...], preferred_element_type=F32)
    pb = jnp.dot(ob_ref[...], wb_ref[...], preferred_element_type=F32)
    merged = ga_ref[...] * pa + gb_ref[...] * pb
    x1 = x_ref[...] + jnp.dot(merged.astype(BF16), wo_ref[...], preferred_element_type=F32)
    x1_ref[...] = x1
    hf = (x1 * lax.rsqrt(jnp.mean(x1 * x1, axis=-1, keepdims=True) + EPS)) * g_ref[...]
    hf_ref[...] = hf.astype(BF16)


def _merge_call(oa, ob, gates, x2d, w_a, w_b, w_o, g_ffn, tm):
    m = x2d.shape[0]
    row = lambda w: pl.BlockSpec((tm, w), lambda i: (i, 0))
    return pl.pallas_call(
        _merge_body,
        grid=(m // tm,),
        in_specs=[row(ATT_WIDTH), row(S5_WIDTH),
                  pl.BlockSpec((tm, D_MODEL), lambda i: (i, 0)), pl.BlockSpec((tm, D_MODEL), lambda i: (i, 1)),
                  row(D_MODEL), _const_spec(w_a.shape), _const_spec(w_b.shape), _const_spec(w_o.shape),
                  _const_spec((1, D_MODEL))],
        out_specs=(row(D_MODEL), row(D_MODEL)),
        out_shape=(jax.ShapeDtypeStruct((m, D_MODEL), F32), jax.ShapeDtypeStruct((m, D_MODEL), BF16)),
        compiler_params=_cparams("parallel"),
        name="merge",
    )(oa, ob, gates, gates, x2d, w_a, w_b, w_o, g_ffn)


FFN_OUT_CHUNK = 512


def _ffn_body(hf_ref, x1_hbm, wg_ref, wu_ref, wd_ref, g_ref, y_ref, x1_sem, *, final_norm, tm):
    i, f = pl.program_id(0), pl.program_id(1)
    x1_copy = pltpu.make_async_copy(x1_hbm.at[pl.ds(i * tm, tm), :], y_ref, x1_sem)

    @pl.when(f == 0)
    def _():
        x1_copy.start()

    hf = hf_ref[...]
    a = jax.nn.silu(jnp.dot(hf, wg_ref[...], preferred_element_type=F32)) * jnp.dot(
        hf, wu_ref[...], preferred_element_type=F32)
    ab = a.astype(BF16)

    @pl.when(f == 0)
    def _():
        x1_copy.wait()

    for c0 in range(0, D_MODEL, FFN_OUT_CHUNK):
        cols = slice(c0, c0 + FFN_OUT_CHUNK)
        y_ref[:, cols] += jnp.dot(ab, wd_ref[:, cols], preferred_element_type=F32)

    if final_norm:
        @pl.when(f == pl.num_programs(1) - 1)
        def _():
            x2 = y_ref[...]
            y_ref[...] = (x2 * lax.rsqrt(jnp.mean(x2 * x2, axis=-1, keepdims=True) + EPS)) * g_ref[...]


def _ffn_call(hf, x1, w_gate, w_up, w_down, g_final, tm, tf, final_norm):
    m = hf.shape[0]
    return pl.pallas_call(
        functools.partial(_ffn_body, final_norm=final_norm, tm=tm),
        grid=(m // tm, D_FF // tf),
        in_specs=[pl.BlockSpec((tm, D_MODEL), lambda i, f: (i, 0)),
                  pl.BlockSpec(memory_space=pl.ANY),
                  pl.BlockSpec((D_MODEL, tf), lambda i, f: (0, f)),
                  pl.BlockSpec((D_MODEL, tf), lambda i, f: (0, f)),
                  pl.BlockSpec((tf, D_MODEL), lambda i, f: (f, 0)),
                  _const_spec((1, D_MODEL))],
        out_specs=pl.BlockSpec((tm, D_MODEL), lambda i, f: (i, 0)),
        out_shape=jax.ShapeDtypeStruct((m, D_MODEL), F32),
        scratch_shapes=[pltpu.SemaphoreType.DMA(())],
        compiler_params=_cparams("parallel", "arbitrary"),
        name="ffn",
    )(hf, x1, w_gate, w_up, w_down, g_final)


def _rope_tables(pos, dim):
    half = dim // 2
    inv = 1.0 / (ROPE_THETA ** (jnp.arange(half, dtype=F32) * (2.0 / dim)))
    ang = pos[:, None] * inv[None, :]
    cos, sin = jnp.cos(ang), jnp.sin(ang)
    reps = LANES // dim
    return (jnp.tile(jnp.concatenate([cos, cos], axis=-1), (1, reps)),
            jnp.tile(jnp.concatenate([-sin, sin], axis=-1), (1, reps)))


def _split_w_in(w_in):
    o = IN_OFFS
    seg = lambda i, j: w_in[:, o[i]:o[j]].astype(BF16)
    pad = lambda w: jnp.pad(w, ((0, 0), (0, LANES - w.shape[1])))
    return seg(0, 4), jnp.concatenate([pad(seg(4, 5)), pad(seg(5, 6))], axis=1), seg(6, 7), seg(7, 8)


def _layer(x, pos, past, ssm0, lw):
    nb, t_len, _ = x.shape
    m = nb * t_len
    x2d = x.reshape(m, D_MODEL)

    cq, sq = _rope_tables(pos, HEAD_DIM)
    ci, si = _rope_tables(pos, IDX_DIM)
    tabs = (cq, sq, ci, si)
    if t_len < PROJ_ROWS:
        tabs = tuple(jnp.tile(t, (PROJ_ROWS // t_len, 1)) for t in tabs)

    h, q, k, v, kb, vb, iq, ik, iw, u = _proj_call(
        x2d, lw['g_mix'], lw['w_att'], lw['w_idx'], lw['w_u'], tabs, PROJ_ROWS)
    gates = _gates_call(h, lw['w_gl'], PROJ_ROWS)

    if past is None:
        s_chunk = DSA_GROUP_BLOCKS * CHUNK
        kb_all = kb.reshape(nb, t_len, KV_WIDTH)
        vb_all = vb.reshape(nb, t_len, KV_WIDTH)
        ikbd = _indexer_key_blocks(ik.reshape(nb, t_len, IDX_DIM), s_chunk)
        n_blk = t_len // CHUNK
        o_a = jnp.zeros((m, ATT_WIDTH), BF16)
        for blk0 in range(0, n_blk, DSA_GROUP_BLOCKS):
            s_keys = (blk0 + DSA_GROUP_BLOCKS) * CHUNK
            qrows = DSA_GROUP_BLOCKS * CHUNK if s_keys <= DSA_WIDE_STEP_MAX_KEYS else QROWS
            o_a = _dsa_call(q, iq, iw, kb_all, vb_all, ikbd, o_a, qrows, blk0, DSA_GROUP_BLOCKS,
                            n_blk, s_keys, s_chunk, min(TOPK_MAX, t_len // 4))
    else:
        pk, pv, pik = past
        assert t_len == CHUNK, "the cached stream is one query block per sequence"
        l_keys = pk.shape[1] + t_len
        o_a = _dsa_cached_call(q, iq, iw, kb, vb, ik, pk.reshape(nb, -1, HEAD_DIM), pv.reshape(nb, -1, HEAD_DIM),
                               pik, min(TOPK_MAX, l_keys // 4))

    x0_re, x0_im = ssm0
    o_b, s_re, s_im = _s5_call(
        u.reshape(nb, t_len, S5_WIDTH), x0_re.reshape(nb, S5_LANES), x0_im.reshape(nb, S5_LANES),
        lw['ab_re'], lw['ab_im'], lw['b_re_bd'], lw['b_im_bd'], lw['c_re_bd'], lw['c_im_bd'],
        lw['d_skip'], lw['w_glu'], lw['b_glu'], min(t_len, S5_STEP_ROWS // nb))

    x1, hf = _merge_call(o_a, o_b.reshape(m, S5_WIDTH), gates, x2d, lw['w_proj_a'], lw['w_proj_b'],
                         lw['w_out'], lw['g_ffn'], MERGE_ROWS)
    caches = (k.reshape(nb, t_len, N_KV_HEADS, HEAD_DIM), v.reshape(nb, t_len, N_KV_HEADS, HEAD_DIM),
              ik.reshape(nb, t_len, IDX_DIM), s_re.reshape(nb, S5_GROUPS, S5_STATE),
              s_im.reshape(nb, S5_GROUPS, S5_STATE))
    return x1, hf, caches


def kernel(x_prompt, x_sample, cache_k, cache_v, cache_idx_k, state_ssm_re, state_ssm_im,
           g_mix, w_in, a_re, a_im, log_dt, b_re, b_im, c_re, c_im, d_skip, w_glu, b_glu,
           w_proj_a, w_proj_b, w_out, g_ffn, w_gate, w_up, w_down, g_final):
    depth = w_in.shape[0]
    t_p, t_s = x_prompt.shape[1], x_sample.shape[1]
    past_len = cache_k.shape[2]
    pos_p = jnp.arange(t_p, dtype=F32)
    pos_s = past_len + jnp.arange(t_s, dtype=F32)
    nb_p, nb_s = x_prompt.shape[0], x_sample.shape[0]
    g_fin = g_final.reshape(1, D_MODEL)

    hp, hs = x_prompt, x_sample
    outs_p, outs_s = [], []
    for l in range(depth):
        w_att, w_idx, w_u, w_gl = _split_w_in(w_in[l])
        ab_re, ab_im, bb_re, bb_im = _s5prep_call(a_re[l], a_im[l], log_dt[l], b_re[l], b_im[l])
        per_group = lambda bb: jnp.transpose(bb.reshape(S5_CH, S5_GROUPS, S5_STATE), (1, 0, 2))
        lw = {
            'g_mix': g_mix[l].reshape(1, D_MODEL), 'w_att': w_att, 'w_idx': w_idx, 'w_u': w_u, 'w_gl': w_gl,
            'ab_re': ab_re.reshape(1, S5_LANES), 'ab_im': ab_im.reshape(1, S5_LANES),
            'b_re_bd': _block_diag_slabs(per_group(bb_re)).astype(BF16),
            'b_im_bd': _block_diag_slabs(per_group(bb_im)).astype(BF16),
            'c_re_bd': _block_diag_slabs(jnp.transpose(c_re[l], (0, 2, 1))).astype(BF16),
            'c_im_bd': _block_diag_slabs(jnp.transpose(c_im[l], (0, 2, 1))).astype(BF16),
            'd_skip': d_skip[l].reshape(1, S5_WIDTH), 'w_glu': w_glu[l].astype(BF16),
            'b_glu': b_glu[l].reshape(1, S5_WIDTH),
            'w_proj_a': w_proj_a[l].astype(BF16), 'w_proj_b': w_proj_b[l].astype(BF16),
            'w_out': w_out[l].astype(BF16), 'g_ffn': g_ffn[l].reshape(1, D_MODEL),
        }
        wg, wu, wd = w_gate[l].astype(BF16), w_up[l].astype(BF16), w_down[l].astype(BF16)
        last = l == depth - 1

        zeros = jnp.zeros((nb_p, S5_GROUPS, S5_STATE), F32)
        x1p, hfp, cp = _layer(hp, pos_p, None, (zeros, zeros), lw)
        x1s, hfs, cs = _layer(hs, pos_s, (cache_k[l], cache_v[l], cache_idx_k[l]),
                              (state_ssm_re[l], state_ssm_im[l]), lw)
        yp = _ffn_call(hfp, x1p, wg, wu, wd, g_fin, FFN_ROWS, FFN_FF_TILE, last)
        ys = _ffn_call(hfs, x1s, wg, wu, wd, g_fin, FFN_ROWS, FFN_FF_TILE, last)
        hp = yp.reshape(nb_p, t_p, D_MODEL)
        hs = ys.reshape(nb_s, t_s, D_MODEL)
        outs_p.append(cp)
        outs_s.append(cs)

    stack = lambda outs, i: jnp.stack([o[i] for o in outs])
    return (hp, hs,
            stack(outs_p, 0), stack(outs_p, 1), stack(outs_p, 2), stack(outs_p, 3), stack(outs_p, 4),
            stack(outs_s, 0), stack(outs_s, 1), stack(outs_s, 2), stack(outs_s, 3), stack(outs_s, 4))
```

```python
import functools
import math

import jax
import jax.numpy as jnp
import numpy as np
from jax import lax
from jax.experimental import pallas as pl
from jax.experimental.pallas import tpu as pltpu

F32 = jnp.float32
BF16 = jnp.bfloat16

D_MODEL = 2048
CHUNK = 64
HEAD_DIM = 128
N_HEADS = 8
N_KV_HEADS = 2
GROUP = N_HEADS // N_KV_HEADS
ATT_WIDTH = N_HEADS * HEAD_DIM
KV_WIDTH = N_KV_HEADS * HEAD_DIM
IDX_HEADS = 16
IDX_DIM = 64
IDX_WIDTH = IDX_HEADS * IDX_DIM
TOPK_MAX = 256
S5_CH = 16
S5_WIDTH = D_MODEL // 2
S5_GROUPS = S5_WIDTH // S5_CH
S5_STATE = 64
S5_LANES = S5_GROUPS * S5_STATE
D_FF = 5632
ROPE_THETA = 10000.0
EPS = 1e-6
IN_SIZES = (ATT_WIDTH, KV_WIDTH, KV_WIDTH, IDX_WIDTH, IDX_DIM, IDX_HEADS, S5_WIDTH, 2 * D_MODEL)
IN_OFFS = tuple(int(s) for s in np.cumsum((0,) + IN_SIZES))

LANES = 128
S5_SLAB_GROUPS = LANES // S5_CH
S5_SLABS = S5_GROUPS // S5_SLAB_GROUPS
S5_SLAB_STATE = S5_SLAB_GROUPS * S5_STATE
INT_MIN = -2 ** 31
DSA_GROUP_BLOCKS = 4
QROWS = 2 * CHUNK
DSA_WIDE_STEP_MAX_KEYS = 1280
S5_STEP_ROWS = 512
SEARCH_PART = 128

VMEM_LIMIT = 56 * 2 ** 20
PROJ_ROWS = 512
MERGE_ROWS = 256
FFN_ROWS = 1024
FFN_FF_TILE = 512
W_IN_SPLIT_ROWS = 256


def _cparams(*sem):
    return pltpu.CompilerParams(dimension_semantics=sem, vmem_limit_bytes=VMEM_LIMIT)


def _const_spec(shape):
    nd = len(shape)
    return pl.BlockSpec(shape, lambda *_: (0,) * nd, pipeline_mode=pl.Buffered(1))


_PQ = 0
_PK = _PQ + ATT_WIDTH
_PV = _PK + KV_WIDTH
_PIQ = _PV + KV_WIDTH
_PEND = _PIQ + IDX_WIDTH
IW_SCALE = (IDX_DIM ** -0.5) * (IDX_HEADS ** -0.5)


def _rope128(z, cos, sin):
    return z * cos + pltpu.roll(z, HEAD_DIM // 2, 1) * sin


def _rope64(z, cos, sin, low_half):
    partner = jnp.where(low_half, pltpu.roll(z, LANES - IDX_DIM // 2, 1), pltpu.roll(z, IDX_DIM // 2, 1))
    return z * cos + partner * sin


def _proj_body(x_ref, g_ref, wa_ref, wi_ref, wu_ref, cq_ref, sq_ref, ci_ref, si_ref,
               h_ref, q_ref, k_ref, v_ref, kb_ref, vb_ref, iq_ref, ik_ref, iw_ref, u_ref):
    x = x_ref[...]
    tm = x.shape[0]
    h = (x * lax.rsqrt(jnp.mean(x * x, axis=-1, keepdims=True) + EPS)) * g_ref[...]
    hb = h.astype(BF16)
    h_ref[...] = hb

    def proj(lo, hi):
        return jnp.dot(hb, wa_ref[:, lo:hi], preferred_element_type=F32)

    cq, sq, ci, si = cq_ref[...], sq_ref[...], ci_ref[...], si_ref[...]
    low_half = (lax.broadcasted_iota(jnp.int32, cq.shape, 1) & (IDX_DIM - 1)) < (IDX_DIM // 2)

    zq = proj(_PQ, _PK)
    for hd in range(N_HEADS):
        sl = slice(hd * HEAD_DIM, (hd + 1) * HEAD_DIM)
        q_ref[:, sl] = _rope128(zq[:, sl], cq, sq).astype(BF16)
    zk = proj(_PK, _PV)
    zv = proj(_PV, _PIQ)
    vb_ref[...] = zv.astype(BF16)
    for hd in range(N_KV_HEADS):
        sl = slice(hd * HEAD_DIM, (hd + 1) * HEAD_DIM)
        head_rows = pl.ds(hd, tm, stride=N_KV_HEADS)
        r = _rope128(zk[:, sl], cq, sq)
        k_ref[head_rows, :] = r
        kb_ref[:, sl] = r.astype(BF16)
        v_ref[head_rows, :] = zv[:, sl]
    ziq = proj(_PIQ, _PEND)
    for p in range(IDX_WIDTH // LANES):
        sl = slice(p * LANES, (p + 1) * LANES)
        iq_ref[:, sl] = _rope64(ziq[:, sl], ci, si, low_half).astype(BF16)
    z2 = jnp.dot(hb, wi_ref[...], preferred_element_type=F32)
    ik_ref[...] = _rope64(z2[:, :LANES], ci, si, low_half)[:, :IDX_DIM]
    iw_ref[...] = z2[:, LANES:] * IW_SCALE
    u_ref[...] = jnp.dot(hb, wu_ref[...], preferred_element_type=F32)


def _proj_call(x2d, g_mix, w_att, w_idx, w_u, tabs, tm):
    m = x2d.shape[0]
    n_tiles = m // tm
    tab_tiles = tabs[0].shape[0] // tm
    row = lambda w: pl.BlockSpec((tm, w), lambda i: (i, 0))
    tab_spec = pl.BlockSpec((tm, LANES), lambda i: (i % tab_tiles, 0))
    kv_cache = jax.ShapeDtypeStruct((m * N_KV_HEADS, HEAD_DIM), F32)
    kv_cache_spec = pl.BlockSpec((tm * N_KV_HEADS, HEAD_DIM), lambda i: (i, 0))
    out_shape = (
        jax.ShapeDtypeStruct((m, D_MODEL), BF16),
        jax.ShapeDtypeStruct((m, ATT_WIDTH), BF16),
        kv_cache,
        kv_cache,
        jax.ShapeDtypeStruct((m, KV_WIDTH), BF16),
        jax.ShapeDtypeStruct((m, KV_WIDTH), BF16),
        jax.ShapeDtypeStruct((m, IDX_WIDTH), BF16),
        jax.ShapeDtypeStruct((m, IDX_DIM), F32),
        jax.ShapeDtypeStruct((m, LANES), F32),
        jax.ShapeDtypeStruct((m, S5_WIDTH), F32),
    )
    out_specs = (row(D_MODEL), row(ATT_WIDTH), kv_cache_spec, kv_cache_spec, row(KV_WIDTH), row(KV_WIDTH),
                 row(IDX_WIDTH), row(IDX_DIM), row(LANES), row(S5_WIDTH))
    return pl.pallas_call(
        _proj_body,
        grid=(n_tiles,),
        in_specs=[row(D_MODEL), _const_spec((1, D_MODEL)),
                  _const_spec(w_att.shape), _const_spec(w_idx.shape), _const_spec(w_u.shape),
                  tab_spec, tab_spec, tab_spec, tab_spec],
        out_specs=out_specs,
        out_shape=out_shape,
        compiler_params=_cparams("parallel"),
        name="proj",
    )(x2d, g_mix, w_att, w_idx, w_u, *tabs)


def _gates_body(h_ref, w_ref, o_ref):
    o_ref[...] = jax.nn.sigmoid(jnp.dot(h_ref[...], w_ref[...], preferred_element_type=F32))


def _gates_call(h, w_gl, tm):
    m, n = h.shape[0], w_gl.shape[1]
    return pl.pallas_call(
        _gates_body,
        grid=(m // tm,),
        in_specs=[pl.BlockSpec((tm, D_MODEL), lambda i: (i, 0)), _const_spec(w_gl.shape)],
        out_specs=pl.BlockSpec((tm, n), lambda i: (i, 0)),
        out_shape=jax.ShapeDtypeStruct((m, n), F32),
        compiler_params=_cparams("parallel"),
        name="gates",
    )(h, w_gl)


LOG2_E = math.log2(math.e)
NEG_INF_KEY = INT_MIN + 0x7FFFFF
F32_MAX = float(np.finfo(np.float32).max)


def _key_to_f32(key):
    return lax.bitcast_convert_type(key ^ ((key >> 31) & jnp.int32(0x7FFFFFFF)), F32)


def _row_count(mask):
    return jnp.sum(jnp.where(mask, 1.0, 0.0), axis=1, keepdims=True)


def _resolve_threshold_ties(score, t_lo, t_next, excess, kf, s_keys, bias_ref, cand_ref, pick_ref, rem_ref):
    tied_row = excess > 0.0
    ge = score >= t_lo
    above = score >= t_next
    cand_ref[...] = jnp.where(ge & jnp.logical_not(above) & tied_row, 1.0, 0.0)
    pick_ref[...] = jnp.zeros_like(score)
    rem0 = jnp.where(tied_row, kf - _row_count(above), 0.0)
    rem_ref[...] = jnp.broadcast_to(rem0, rem_ref.shape)
    col = lax.broadcasted_iota(jnp.int32, score.shape, 1)

    def take_next_value(_):
        cand = cand_ref[...] > 0.5
        rem = rem_ref[:, :1]
        top = jnp.max(jnp.where(cand, score, -jnp.inf), axis=1, keepdims=True)
        eq = cand & (score == top)
        last = jnp.zeros((score.shape[0], 1), jnp.int32)
        for b in range(int(s_keys).bit_length() - 1, -1, -1):
            nxt = last + jnp.int32(1 << b)
            last = jnp.where(_row_count(eq & (col < nxt)) < rem, nxt, last)
        take = eq & (col <= last) & (rem > 0.0)
        pick_ref[...] = jnp.where(take, 1.0, pick_ref[...])
        cand_ref[...] = jnp.where(eq, 0.0, cand_ref[...])
        rem = jnp.where(top > -jnp.inf, rem - _row_count(take), 0.0)
        rem_ref[...] = jnp.broadcast_to(rem, rem_ref.shape)
        return jnp.max(rem) > 0.0

    lax.while_loop(lambda go: go, take_next_value, jnp.max(rem0) > 0.0)
    chosen = above | (pick_ref[...] > 0.5)
    bias_ref[...] = jnp.where(tied_row, jnp.where(chosen, 0.0, -jnp.inf), bias_ref[...])


def _dsa_core(q_ref, iq_ref, iw_ref, get_k, get_v, get_ikbd, n_keysets,
              o_ref, score_ref, bias_ref, cand_ref, pick_ref, rem_ref, *,
              s_keys, s_chunk, blk0, fixed_valid, topk):
    qrows = q_ref.shape[0]
    halves = [(slice(hf * CHUNK, (hf + 1) * CHUNK), hf % n_keysets) for hf in range(qrows // CHUNK)]

    pairs = IDX_WIDTH // LANES
    for hf, (rows, ks) in enumerate(halves):
        if fixed_valid is None:
            valid = (blk0 + len(halves) * pl.program_id(1) + hf + 1) * CHUNK
        else:
            valid = fixed_valid
        iq = iq_ref[rows, :]
        lhs = jnp.concatenate([iq[:, p * LANES:(p + 1) * LANES] for p in range(pairs)], axis=0)
        iw = iw_ref[rows, :]
        for c in range(s_keys // s_chunk):
            logits = lax.dot_general(lhs, get_ikbd(ks, c), (((1,), (1,)), ((), ())), preferred_element_type=F32)
            acc = jnp.zeros((CHUNK, s_chunk), F32)
            for p in range(pairs):
                lp = logits[p * CHUNK:(p + 1) * CHUNK]
                acc = acc + jnp.maximum(lp[:, :s_chunk], 0.0) * iw[:, 2 * p:2 * p + 1]
                acc = acc + jnp.maximum(lp[:, s_chunk:], 0.0) * iw[:, 2 * p + 1:2 * p + 2]
            col = c * s_chunk + lax.broadcasted_iota(jnp.int32, acc.shape, 1)
            score_ref[rows, c * s_chunk:(c + 1) * s_chunk] = jnp.where(col < valid, acc, -jnp.inf)

    score = score_ref[...]
    if s_keys <= topk:
        bias_ref[...] = jnp.where(score > -jnp.inf, 0.0, -jnp.inf)
    else:
        kf = float(topk)
        score_t = score.T

        def count_ge(t):
            ind = jnp.where(score_t >= t, 1.0, 0.0).reshape(s_keys // SEARCH_PART, SEARCH_PART, qrows)
            return jnp.sum(jnp.sum(ind, axis=0), axis=0, keepdims=True)

        thr = jnp.full((1, qrows), INT_MIN, jnp.int32)
        for b in range(31, -1, -1):
            cand = thr + jnp.int32(INT_MIN if b == 31 else 1 << b)
            thr = jnp.where(count_ge(_key_to_f32(cand)) >= kf, cand, thr)
        thr = jnp.maximum(thr, jnp.int32(NEG_INF_KEY))
        t_lo = jnp.maximum(_key_to_f32(thr), -F32_MAX)
        excess = count_ge(t_lo) - kf
        stats = jnp.concatenate([t_lo, _key_to_f32(thr + 1), excess, jnp.zeros((qrows - 3, qrows), F32)], axis=0).T
        t_lo_col = stats[:, 0:1]
        bias_ref[...] = jnp.where(score >= t_lo_col, 0.0, -jnp.inf)

        @pl.when(jnp.max(excess) > 0.0)
        def _():
            _resolve_threshold_ties(score, t_lo_col, stats[:, 1:2], stats[:, 2:3], kf, s_keys,
                                    bias_ref, cand_ref, pick_ref, rem_ref)

    scale = HEAD_DIM ** -0.5
    for rows, ks in halves:
        bias = bias_ref[rows, :]
        q = q_ref[rows, :]
        for c in range(N_KV_HEADS):
            kc = get_k(ks, c)
            vc = get_v(ks, c)
            qc = jnp.concatenate(
                [q[:, (c * GROUP + g) * HEAD_DIM:(c * GROUP + g + 1) * HEAD_DIM] for g in range(GROUP)], axis=0)
            logits = lax.dot_general(qc, kc, (((1,), (1,)), ((), ())), preferred_element_type=F32)
            es, inv = [], []
            for g in range(GROUP):
                lg = logits[g * CHUNK:(g + 1) * CHUNK] + bias
                e = jnp.exp2((lg - jnp.max(lg, axis=1, keepdims=True)) * (scale * LOG2_E))
                inv.append(1.0 / jnp.sum(e, axis=1, keepdims=True))
                es.append(e.astype(BF16))
            oc = jnp.dot(jnp.concatenate(es, axis=0), vc, preferred_element_type=F32)
            for g in range(GROUP):
                hd = c * GROUP + g
                o_ref[rows, hd * HEAD_DIM:(hd + 1) * HEAD_DIM] = (
                    oc[g * CHUNK:(g + 1) * CHUNK] * inv[g]).astype(BF16)


def _dsa_body(q_ref, iq_ref, iw_ref, kb_ref, vb_ref, ikbd_ref, *rest, **static):
    head = lambda ref: (lambda ks, c: ref[ks, :, c * HEAD_DIM:(c + 1) * HEAD_DIM])
    _dsa_core(q_ref, iq_ref, iw_ref, head(kb_ref), head(vb_ref), lambda ks, c: ikbd_ref[ks, c],
              kb_ref.shape[0], *rest[-6:], **static)


def _dsa_cached_body(q_ref, iq_ref, iw_ref, kn_ref, vn_ref, ikn_ref, pk_ref, pv_ref, pik_ref,
                     o_ref, k_all, v_all, ikbd_all, *scratch, past_len, **static):
    s_keys = static['s_keys']
    new_end = past_len + CHUNK
    for ks in range(2):
        new = slice(ks * CHUNK, (ks + 1) * CHUNK)
        for c in range(N_KV_HEADS):
            for dst, past, fresh in ((k_all, pk_ref, kn_ref), (v_all, pv_ref, vn_ref)):
                dst[ks, c, :past_len, :] = past[ks, pl.ds(c, past_len, stride=N_KV_HEADS), :].astype(BF16)
                dst[ks, c, past_len:new_end, :] = fresh[new, c * HEAD_DIM:(c + 1) * HEAD_DIM]
                dst[ks, c, new_end:, :] = jnp.zeros((s_keys - new_end, HEAD_DIM), BF16)
        ik = jnp.concatenate([pik_ref[ks].astype(BF16), ikn_ref[new, :].astype(BF16),
                              jnp.zeros((s_keys - new_end, IDX_DIM), BF16)], axis=0)
        z = jnp.zeros_like(ik)
        ikbd_all[ks, :s_keys, :] = jnp.concatenate([ik, z], axis=1)
        ikbd_all[ks, s_keys:, :] = jnp.concatenate([z, ik], axis=1)
    _dsa_core(q_ref, iq_ref, iw_ref, lambda ks, c: k_all[ks, c], lambda ks, c: v_all[ks, c],
              lambda ks, c: ikbd_all[ks], 2, o_ref, *scratch, **static)


def _dsa_cached_call(q, iq, iw, kb, vb, ik, pk, pv, pik, topk):
    nb, past_len = pik.shape[0], pik.shape[1]
    l_keys = past_len + CHUNK
    s_keys = -(-l_keys // LANES) * LANES
    body = functools.partial(_dsa_cached_body, past_len=past_len, s_keys=s_keys, s_chunk=s_keys, blk0=0,
                             fixed_valid=l_keys, topk=topk)
    qrow = lambda w: pl.BlockSpec((QROWS, w), lambda n: (n, 0))
    pair = lambda a: pl.BlockSpec((2,) + a.shape[1:], lambda n: (n, 0, 0))
    mask_buf = pltpu.VMEM((QROWS, s_keys), F32)
    heads = pltpu.VMEM((2, N_KV_HEADS, s_keys, HEAD_DIM), BF16)
    return pl.pallas_call(
        body,
        grid=(nb // 2,),
        in_specs=[qrow(ATT_WIDTH), qrow(IDX_WIDTH), qrow(LANES), qrow(KV_WIDTH), qrow(KV_WIDTH), qrow(IDX_DIM),
                  pair(pk), pair(pv), pair(pik)],
        out_specs=qrow(ATT_WIDTH),
        out_shape=jax.ShapeDtypeStruct((nb * CHUNK, ATT_WIDTH), BF16),
        scratch_shapes=[heads, heads, pltpu.VMEM((2, 2 * s_keys, LANES), BF16),
                        mask_buf, mask_buf, mask_buf, mask_buf, pltpu.VMEM((QROWS, LANES), F32)],
        compiler_params=_cparams("parallel"),
        name="dsa_cached",
    )(q, iq, iw, kb, vb, ik, pk, pv, pik)


def _dsa_call(q, iq, iw, kb, vb, ikbd, o_prev, qrows, blk0, n_blk, blks_per_seq, s_keys, s_chunk, topk):
    nb = kb.shape[0]
    blocks_per_step = qrows // CHUNK
    steps_per_seq = blks_per_seq // blocks_per_step
    qrow = lambda w: pl.BlockSpec(
        (qrows, w), lambda n, j: (n * steps_per_seq + blk0 // blocks_per_step + j, 0))
    n_chunks = s_keys // s_chunk
    body = functools.partial(_dsa_body, s_keys=s_keys, s_chunk=s_chunk, blk0=blk0, fixed_valid=None, topk=topk)
    mask_buf = pltpu.VMEM((qrows, s_keys), F32)
    return pl.pallas_call(
        body,
        grid=(nb, n_blk // blocks_per_step),
        in_specs=[qrow(ATT_WIDTH), qrow(IDX_WIDTH), qrow(LANES),
                  pl.BlockSpec((1, s_keys, KV_WIDTH), lambda n, j: (n, 0, 0)),
                  pl.BlockSpec((1, s_keys, KV_WIDTH), lambda n, j: (n, 0, 0)),
                  pl.BlockSpec((1, n_chunks, 2 * s_chunk, LANES), lambda n, j: (n, 0, 0, 0)),
                  pl.BlockSpec(memory_space=pl.ANY)],
        out_specs=qrow(ATT_WIDTH),
        out_shape=jax.ShapeDtypeStruct((nb * blks_per_seq * CHUNK, ATT_WIDTH), BF16),
        scratch_shapes=[mask_buf, mask_buf, mask_buf, mask_buf, pltpu.VMEM((qrows, LANES), F32)],
        input_output_aliases={6: 0},
        compiler_params=_cparams("parallel", "arbitrary"),
        name="dsa",
    )(q, iq, iw, kb, vb, ikbd, o_prev)


def _indexer_key_blocks(ik, s_chunk):
    nb, s, _ = ik.shape
    ikb = ik.astype(BF16).reshape(nb, s // s_chunk, s_chunk, IDX_DIM)
    z = jnp.zeros_like(ikb)
    return jnp.concatenate([jnp.concatenate([ikb, z], axis=-1), jnp.concatenate([z, ikb], axis=-1)], axis=-2)


def _s5prep_body(are_ref, aim_ref, ldt_ref, bre_ref, bim_ref, abre_ref, abim_ref, bbre_ref, bbim_ref):
    a_re, a_im = are_ref[...], aim_ref[...]
    dt = jnp.exp(ldt_ref[...])
    mag = jnp.exp(dt * a_re)
    ab_re = mag * jnp.cos(dt * a_im)
    ab_im = mag * jnp.sin(dt * a_im)
    den = a_re * a_re + a_im * a_im
    f_re = ((ab_re - 1.0) * a_re + ab_im * a_im) / den
    f_im = (ab_im * a_re - (ab_re - 1.0) * a_im) / den
    abre_ref[...] = ab_re
    abim_ref[...] = ab_im
    for c in range(S5_CH):
        b_re, b_im = bre_ref[c], bim_ref[c]
        bbre_ref[c] = f_re * b_re - f_im * b_im
        bbim_ref[c] = f_re * b_im + f_im * b_re


def _s5prep_call(a_re, a_im, log_dt, b_re, b_im):
    rows = S5_LANES // LANES
    flat = lambda a: a.reshape(rows, LANES)
    ldt = jnp.broadcast_to(log_dt[:, None], (S5_GROUPS, S5_STATE))
    chan_major = lambda b: jnp.transpose(b, (2, 0, 1)).reshape(S5_CH, rows, LANES)
    small = jax.ShapeDtypeStruct((rows, LANES), F32)
    big = jax.ShapeDtypeStruct((S5_CH, rows, LANES), F32)
    return pl.pallas_call(
        _s5prep_body,
        out_shape=(small, small, big, big),
        name="s5prep",
    )(flat(a_re), flat(a_im), flat(ldt), chan_major(b_re), chan_major(b_im))


def _gelu_tanh(x):
    return 0.5 * x * (1.0 + jnp.tanh(math.sqrt(2.0 / math.pi) * (x + 0.044715 * (x * x * x))))


def _s5_body(u_ref, x0re_ref, x0im_ref, abre_ref, abim_ref, bre_ref, bim_ref, cre_ref, cim_ref,
             dskip_ref, wglu_ref, bglu_ref, ob_ref, sre_ref, sim_ref, utm, otm, xre, xim, st_re, st_im,
             *, tc, nb):
    step = pl.program_id(0)

    @pl.when(step == 0)
    def _():
        st_re[...] = x0re_ref[...]
        st_im[...] = x0im_ref[...]

    for n in range(nb):
        for k in range(S5_SLABS):
            utm[k, pl.ds(n, tc, stride=nb), :] = u_ref[n, :, k * LANES:(k + 1) * LANES]

    def project_in(k):
        us = utm[k].astype(BF16)
        sl = slice(k * S5_SLAB_STATE, (k + 1) * S5_SLAB_STATE)
        xre[:, :, sl] = jnp.dot(us, bre_ref[k], preferred_element_type=F32).reshape(tc, nb, S5_SLAB_STATE)
        xim[:, :, sl] = jnp.dot(us, bim_ref[k], preferred_element_type=F32).reshape(tc, nb, S5_SLAB_STATE)

    def scan(k):
        sl = slice(k * S5_SLAB_STATE, (k + 1) * S5_SLAB_STATE)
        a_r = jnp.broadcast_to(abre_ref[:, sl], (nb, S5_SLAB_STATE))
        a_i = jnp.broadcast_to(abim_ref[:, sl], (nb, S5_SLAB_STATE))
        s_r, s_i = st_re[:, sl], st_im[:, sl]
        for t in range(tc):
            s_r, s_i = (a_r * s_r - a_i * s_i + xre[t, :, sl],
                        a_r * s_i + a_i * s_r + xim[t, :, sl])
            xre[t, :, sl] = s_r
            xim[t, :, sl] = s_i
        st_re[:, sl] = s_r
        st_im[:, sl] = s_i

    def project_out(k):
        sl = slice(k * S5_SLAB_STATE, (k + 1) * S5_SLAB_STATE)
        xr = xre[:, :, sl].reshape(tc * nb, S5_SLAB_STATE).astype(BF16)
        xi = xim[:, :, sl].reshape(tc * nb, S5_SLAB_STATE).astype(BF16)
        return (jnp.dot(xr, cre_ref[k], preferred_element_type=F32)
                - jnp.dot(xi, cim_ref[k], preferred_element_type=F32)
                + dskip_ref[:, k * LANES:(k + 1) * LANES] * utm[k])

    project_in(0)
    ys = []
    for k in range(S5_SLABS):
        if k + 1 < S5_SLABS:
            project_in(k + 1)
        scan(k)
        ys.append(project_out(k))
    yb = _gelu_tanh(jnp.concatenate(ys, axis=1))
    gate = jax.nn.sigmoid(jnp.dot(yb.astype(BF16), wglu_ref[...], preferred_element_type=F32) + bglu_ref[...])
    o = yb * gate
    for k in range(S5_SLABS):
        otm[k] = o[:, k * LANES:(k + 1) * LANES]
    for n in range(nb):
        for k in range(S5_SLABS):
            ob_ref[n, :, k * LANES:(k + 1) * LANES] = otm[k, pl.ds(n, tc, stride=nb), :].astype(BF16)

    @pl.when(step == pl.num_programs(0) - 1)
    def _():
        sre_ref[...] = st_re[...]
        sim_ref[...] = st_im[...]


def _s5_call(u, x0_re, x0_im, ab_re, ab_im, b_re_bd, b_im_bd, c_re_bd, c_im_bd, d_skip, w_glu, b_glu, tc):
    nb, t_len, _ = u.shape
    body = functools.partial(_s5_body, tc=tc, nb=nb)
    state = jax.ShapeDtypeStruct((nb, S5_LANES), F32)
    seq_spec = pl.BlockSpec((nb, tc, S5_WIDTH), lambda i: (0, i, 0))
    slabs = pltpu.VMEM((S5_SLABS, tc * nb, LANES), F32)
    return pl.pallas_call(
        body,
        grid=(t_len // tc,),
        in_specs=[seq_spec,
                  _const_spec((nb, S5_LANES)), _const_spec((nb, S5_LANES)),
                  _const_spec((1, S5_LANES)), _const_spec((1, S5_LANES)),
                  _const_spec(b_re_bd.shape), _const_spec(b_im_bd.shape),
                  _const_spec(c_re_bd.shape), _const_spec(c_im_bd.shape),
                  _const_spec((1, S5_WIDTH)), _const_spec((S5_WIDTH, S5_WIDTH)), _const_spec((1, S5_WIDTH))],
        out_specs=(seq_spec, _const_spec((nb, S5_LANES)), _const_spec((nb, S5_LANES))),
        out_shape=(jax.ShapeDtypeStruct((nb, t_len, S5_WIDTH), BF16), state, state),
        scratch_shapes=[slabs, slabs,
                        pltpu.VMEM((tc, nb, S5_LANES), F32), pltpu.VMEM((tc, nb, S5_LANES), F32),
                        pltpu.VMEM((nb, S5_LANES), F32), pltpu.VMEM((nb, S5_LANES), F32)],
        compiler_params=_cparams("arbitrary"),
        name="s5",
    )(u, x0_re, x0_im, ab_re, ab_im, b_re_bd, b_im_bd, c_re_bd, c_im_bd, d_skip, w_glu, b_glu)


def _block_diag_slabs(w):
    g, r, c = w.shape
    w = w.reshape(S5_SLABS, S5_SLAB_GROUPS, r, c)
    eye = jnp.eye(S5_SLAB_GROUPS, dtype=w.dtype)
    bd = w[:, :, :, None, :] * eye[None, :, None, :, None]
    return bd.reshape(S5_SLABS, S5_SLAB_GROUPS * r, S5_SLAB_GROUPS * c)


def _merge_body(oa_ref, ob_ref, ga_ref, gb_ref, x_ref, wa_ref, wb_ref, wo_ref, g_ref, x1_ref, hf_ref):
    pa = jnp.dot(oa_ref[...], wa_ref[...], preferred_element_type=F32)
    pb = jnp.dot(ob_ref[...], wb_ref[...], preferred_element_type=F32)
    merged = ga_ref[...] * pa + gb_ref[...] * pb
    x1 = x_ref[...] + jnp.dot(merged.astype(BF16), wo_ref[...], preferred_element_type=F32)
    x1_ref[...] = x1
    hf = (x1 * lax.rsqrt(jnp.mean(x1 * x1, axis=-1, keepdims=True) + EPS)) * g_ref[...]
    hf_ref[...] = hf.astype(BF16)


def _merge_call(oa, ob, gates, x2d, w_a, w_b, w_o, g_ffn, tm):
    m = x2d.shape[0]
    row = lambda w: pl.BlockSpec((tm, w), lambda i: (i, 0))
    return pl.pallas_call(
        _merge_body,
        grid=(m // tm,),
        in_specs=[row(ATT_WIDTH), row(S5_WIDTH),
                  pl.BlockSpec((tm, D_MODEL), lambda i: (i, 0)), pl.BlockSpec((tm, D_MODEL), lambda i: (i, 1)),
                  row(D_MODEL), _const_spec(w_a.shape), _const_spec(w_b.shape), _const_spec(w_o.shape),
                  _const_spec((1, D_MODEL))],
        out_specs=(row(D_MODEL), row(D_MODEL)),
        out_shape=(jax.ShapeDtypeStruct((m, D_MODEL), F32), jax.ShapeDtypeStruct((m, D_MODEL), BF16)),
        compiler_params=_cparams("parallel"),
        name="merge",
    )(oa, ob, gates, gates, x2d, w_a, w_b, w_o, g_ffn)


FFN_OUT_CHUNK = 512


def _ffn_body(hf_ref, x1_hbm, wg_ref, wu_ref, wd_ref, g_ref, y_ref, x1_sem, *, final_norm, tm):
    i, f = pl.program_id(0), pl.program_id(1)
    x1_copy = pltpu.make_async_copy(x1_hbm.at[pl.ds(i * tm, tm), :], y_ref, x1_sem)

    @pl.when(f == 0)
    def _():
        x1_copy.start()

    hf = hf_ref[...]
    a = jax.nn.silu(jnp.dot(hf, wg_ref[...], preferred_element_type=F32)) * jnp.dot(
        hf, wu_ref[...], preferred_element_type=F32)
    ab = a.astype(BF16)

    @pl.when(f == 0)
    def _():
        x1_copy.wait()

    for c0 in range(0, D_MODEL, FFN_OUT_CHUNK):
        cols = slice(c0, c0 + FFN_OUT_CHUNK)
        y_ref[:, cols] += jnp.dot(ab, wd_ref[:, cols], preferred_element_type=F32)

    if final_norm:
        @pl.when(f == pl.num_programs(1) - 1)
        def _():
            x2 = y_ref[...]
            y_ref[...] = (x2 * lax.rsqrt(jnp.mean(x2 * x2, axis=-1, keepdims=True) + EPS)) * g_ref[...]


def _ffn_call(hf, x1, w_gate, w_up, w_down, g_final, tm, tf, final_norm):
    m = hf.shape[0]
    return pl.pallas_call(
        functools.partial(_ffn_body, final_norm=final_norm, tm=tm),
        grid=(m // tm, D_FF // tf),
        in_specs=[pl.BlockSpec((tm, D_MODEL), lambda i, f: (i, 0)),
                  pl.BlockSpec(memory_space=pl.ANY),
                  pl.BlockSpec((D_MODEL, tf), lambda i, f: (0, f)),
                  pl.BlockSpec((D_MODEL, tf), lambda i, f: (0, f)),
                  pl.BlockSpec((tf, D_MODEL), lambda i, f: (f, 0)),
                  _const_spec((1, D_MODEL))],
        out_specs=pl.BlockSpec((tm, D_MODEL), lambda i, f: (i, 0)),
        out_shape=jax.ShapeDtypeStruct((m, D_MODEL), F32),
        scratch_shapes=[pltpu.SemaphoreType.DMA(())],
        compiler_params=_cparams("parallel", "arbitrary"),
        name="ffn",
    )(hf, x1, w_gate, w_up, w_down, g_final)


def _rope_tables(pos, dim):
    half = dim // 2
    inv = 1.0 / (ROPE_THETA ** (jnp.arange(half, dtype=F32) * (2.0 / dim)))
    ang = pos[:, None] * inv[None, :]
    cos, sin = jnp.cos(ang), jnp.sin(ang)
    reps = LANES // dim
    return (jnp.tile(jnp.concatenate([cos, cos], axis=-1), (1, reps)),
            jnp.tile(jnp.concatenate([-sin, sin], axis=-1), (1, reps)))


def _split_w_in_body(w_ref, att_ref, idx_ref, u_ref, gl_ref):
    o = IN_OFFS
    rows = w_ref.shape[0]
    seg = lambda i, j: w_ref[:, o[i]:o[j]].astype(BF16)
    pad = lambda w: jnp.concatenate([w, jnp.zeros((rows, LANES - w.shape[1]), BF16)], axis=1)
    att_ref[...] = seg(0, 4)
    idx_ref[...] = jnp.concatenate([pad(seg(4, 5)), pad(seg(5, 6))], axis=1)
    u_ref[...] = seg(6, 7)
    gl_ref[...] = seg(7, 8)


def _split_w_in(w_in):
    d, width = w_in.shape
    o = IN_OFFS
    widths = (o[4] - o[0], 2 * LANES, o[7] - o[6], o[8] - o[7])
    return pl.pallas_call(
        _split_w_in_body,
        grid=(d // W_IN_SPLIT_ROWS,),
        in_specs=[pl.BlockSpec((W_IN_SPLIT_ROWS, width), lambda i: (i, 0))],
        out_specs=tuple(pl.BlockSpec((W_IN_SPLIT_ROWS, w), lambda i: (i, 0)) for w in widths),
        out_shape=tuple(jax.ShapeDtypeStruct((d, w), BF16) for w in widths),
        compiler_params=_cparams("parallel"),
        name="split_w_in",
    )(w_in)


def _layer(x, pos, past, ssm0, lw):
    nb, t_len, _ = x.shape
    m = nb * t_len
    x2d = x.reshape(m, D_MODEL)

    cq, sq = _rope_tables(pos, HEAD_DIM)
    ci, si = _rope_tables(pos, IDX_DIM)
    tabs = (cq, sq, ci, si)
    if t_len < PROJ_ROWS:
        tabs = tuple(jnp.tile(t, (PROJ_ROWS // t_len, 1)) for t in tabs)

    h, q, k, v, kb, vb, iq, ik, iw, u = _proj_call(
        x2d, lw['g_mix'], lw['w_att'], lw['w_idx'], lw['w_u'], tabs, PROJ_ROWS)
    gates = _gates_call(h, lw['w_gl'], PROJ_ROWS)

    if past is None:
        s_chunk = DSA_GROUP_BLOCKS * CHUNK
        kb_all = kb.reshape(nb, t_len, KV_WIDTH)
        vb_all = vb.reshape(nb, t_len, KV_WIDTH)
        ikbd = _indexer_key_blocks(ik.reshape(nb, t_len, IDX_DIM), s_chunk)
        n_blk = t_len // CHUNK
        o_a = jnp.zeros((m, ATT_WIDTH), BF16)
        for blk0 in range(0, n_blk, DSA_GROUP_BLOCKS):
            s_keys = (blk0 + DSA_GROUP_BLOCKS) * CHUNK
            qrows = DSA_GROUP_BLOCKS * CHUNK if s_keys <= DSA_WIDE_STEP_MAX_KEYS else QROWS
            o_a = _dsa_call(q, iq, iw, kb_all, vb_all, ikbd, o_a, qrows, blk0, DSA_GROUP_BLOCKS,
                            n_blk, s_keys, s_chunk, min(TOPK_MAX, t_len // 4))
    else:
        pk, pv, pik = past
        assert t_len == CHUNK, "the cached stream is one query block per sequence"
        l_keys = pk.shape[1] + t_len
        o_a = _dsa_cached_call(q, iq, iw, kb, vb, ik, pk.reshape(nb, -1, HEAD_DIM), pv.reshape(nb, -1, HEAD_DIM),
                               pik, min(TOPK_MAX, l_keys // 4))

    x0_re, x0_im = ssm0
    o_b, s_re, s_im = _s5_call(
        u.reshape(nb, t_len, S5_WIDTH), x0_re.reshape(nb, S5_LANES), x0_im.reshape(nb, S5_LANES),
        lw['ab_re'], lw['ab_im'], lw['b_re_bd'], lw['b_im_bd'], lw['c_re_bd'], lw['c_im_bd'],
        lw['d_skip'], lw['w_glu'], lw['b_glu'], min(t_len, S5_STEP_ROWS // nb))

    x1, hf = _merge_call(o_a, o_b.reshape(m, S5_WIDTH), gates, x2d, lw['w_proj_a'], lw['w_proj_b'],
                         lw['w_out'], lw['g_ffn'], MERGE_ROWS)
    caches = (k.reshape(nb, t_len, N_KV_HEADS, HEAD_DIM), v.reshape(nb, t_len, N_KV_HEADS, HEAD_DIM),
              ik.reshape(nb, t_len, IDX_DIM), s_re.reshape(nb, S5_GROUPS, S5_STATE),
              s_im.reshape(nb, S5_GROUPS, S5_STATE))
    return x1, hf, caches


def kernel(x_prompt, x_sample, cache_k, cache_v, cache_idx_k, state_ssm_re, state_ssm_im,
           g_mix, w_in, a_re, a_im, log_dt, b_re, b_im, c_re, c_im, d_skip, w_glu, b_glu,
           w_proj_a, w_proj_b, w_out, g_ffn, w_gate, w_up, w_down, g_final):
    depth = w_in.shape[0]
    t_p, t_s = x_prompt.shape[1], x_sample.shape[1]
    past_len = cache_k.shape[2]
    pos_p = jnp.arange(t_p, dtype=F32)
    pos_s = past_len + jnp.arange(t_s, dtype=F32)
    nb_p, nb_s = x_prompt.shape[0], x_sample.shape[0]
    g_fin = g_final.reshape(1, D_MODEL)

    hp, hs = x_prompt, x_sample
    outs_p, outs_s = [], []
    for l in range(depth):
        w_att, w_idx, w_u, w_gl = _split_w_in(w_in[l])
        ab_re, ab_im, bb_re, bb_im = _s5prep_call(a_re[l], a_im[l], log_dt[l], b_re[l], b_im[l])
        per_group = lambda bb: jnp.transpose(bb.reshape(S5_CH, S5_GROUPS, S5_STATE), (1, 0, 2))
        lw = {
            'g_mix': g_mix[l].reshape(1, D_MODEL), 'w_att': w_att, 'w_idx': w_idx, 'w_u': w_u, 'w_gl': w_gl,
            'ab_re': ab_re.reshape(1, S5_LANES), 'ab_im': ab_im.reshape(1, S5_LANES),
            'b_re_bd': _block_diag_slabs(per_group(bb_re)).astype(BF16),
            'b_im_bd': _block_diag_slabs(per_group(bb_im)).astype(BF16),
            'c_re_bd': _block_diag_slabs(jnp.transpose(c_re[l], (0, 2, 1))).astype(BF16),
            'c_im_bd': _block_diag_slabs(jnp.transpose(c_im[l], (0, 2, 1))).astype(BF16),
            'd_skip': d_skip[l].reshape(1, S5_WIDTH), 'w_glu': w_glu[l].astype(BF16),
            'b_glu': b_glu[l].reshape(1, S5_WIDTH),
            'w_proj_a': w_proj_a[l].astype(BF16), 'w_proj_b': w_proj_b[l].astype(BF16),
            'w_out': w_out[l].astype(BF16), 'g_ffn': g_ffn[l].reshape(1, D_MODEL),
        }
        wg, wu, wd = w_gate[l].astype(BF16), w_up[l].astype(BF16), w_down[l].astype(BF16)
        last = l == depth - 1

        zeros = jnp.zeros((nb_p, S5_GROUPS, S5_STATE), F32)
        x1p, hfp, cp = _layer(hp, pos_p, None, (zeros, zeros), lw)
        x1s, hfs, cs = _layer(hs, pos_s, (cache_k[l], cache_v[l], cache_idx_k[l]),
                              (state_ssm_re[l], state_ssm_im[l]), lw)
        yp = _ffn_call(hfp, x1p, wg, wu, wd, g_fin, FFN_ROWS, FFN_FF_TILE, last)
        ys = _ffn_call(hfs, x1s, wg, wu, wd, g_fin, FFN_ROWS, FFN_FF_TILE, last)
        hp = yp.reshape(nb_p, t_p, D_MODEL)
        hs = ys.reshape(nb_s, t_s, D_MODEL)
        outs_p.append(cp)
        outs_s.append(cs)

    stack = lambda outs, i: jnp.stack([o[i] for o in outs])
    return (hp, hs,
            stack(outs_p, 0), stack(outs_p, 1), stack(outs_p, 2), stack(outs_p, 3), stack(outs_p, 4),
            stack(outs_s, 0), stack(outs_s, 1), stack(outs_s, 2), stack(outs_s, 3), stack(outs_s, 4))
```

```python
import functools
import math

import jax
import jax.numpy as jnp
import numpy as np
from jax import lax
from jax.experimental import pallas as pl
from jax.experimental.pallas import tpu as pltpu

F32 = jnp.float32
BF16 = jnp.bfloat16

D_MODEL = 2048
CHUNK = 64
HEAD_DIM = 128
N_HEADS = 8
N_KV_HEADS = 2
GROUP = N_HEADS // N_KV_HEADS
ATT_WIDTH = N_HEADS * HEAD_DIM
KV_WIDTH = N_KV_HEADS * HEAD_DIM
IDX_HEADS = 16
IDX_DIM = 64
IDX_WIDTH = IDX_HEADS * IDX_DIM
TOPK_MAX = 256
S5_CH = 16
S5_WIDTH = D_MODEL // 2
S5_GROUPS = S5_WIDTH // S5_CH
S5_STATE = 64
S5_LANES = S5_GROUPS * S5_STATE
D_FF = 5632
ROPE_THETA = 10000.0
EPS = 1e-6
IN_SIZES = (ATT_WIDTH, KV_WIDTH, KV_WIDTH, IDX_WIDTH, IDX_DIM, IDX_HEADS, S5_WIDTH, 2 * D_MODEL)
IN_OFFS = tuple(int(s) for s in np.cumsum((0,) + IN_SIZES))

LANES = 128
S5_SLAB_GROUPS = LANES // S5_CH
S5_SLABS = S5_GROUPS // S5_SLAB_GROUPS
S5_SLAB_STATE = S5_SLAB_GROUPS * S5_STATE
INT_MIN = -2 ** 31
DSA_GROUP_BLOCKS = 4
QROWS = 2 * CHUNK
DSA_WIDE_STEP_MAX_KEYS = 1280
S5_STEP_ROWS = 512
SEARCH_PART = 128

VMEM_LIMIT = 56 * 2 ** 20
PROJ_ROWS = 512
MERGE_ROWS = 256
FFN_ROWS = 1024
FFN_FF_TILE = 512
W_IN_SPLIT_ROWS = 256


def _cparams(*sem):
    return pltpu.CompilerParams(dimension_semantics=sem, vmem_limit_bytes=VMEM_LIMIT)


def _const_spec(shape):
    nd = len(shape)
    return pl.BlockSpec(shape, lambda *_: (0,) * nd, pipeline_mode=pl.Buffered(1))


_PQ = 0
_PK = _PQ + ATT_WIDTH
_PV = _PK + KV_WIDTH
_PIQ = _PV + KV_WIDTH
_PEND = _PIQ + IDX_WIDTH
IW_SCALE = (IDX_DIM ** -0.5) * (IDX_HEADS ** -0.5)


def _rope128(z, cos, sin):
    return z * cos + pltpu.roll(z, HEAD_DIM // 2, 1) * sin


def _rope64(z, cos, sin, low_half):
    partner = jnp.where(low_half, pltpu.roll(z, LANES - IDX_DIM // 2, 1), pltpu.roll(z, IDX_DIM // 2, 1))
    return z * cos + partner * sin


def _proj_body(x_ref, g_ref, wa_ref, wi_ref, wu_ref, cq_ref, sq_ref, ci_ref, si_ref,
               h_ref, q_ref, k_ref, v_ref, kb_ref, vb_ref, iq_ref, ik_ref, iw_ref, u_ref):
    x = x_ref[...]
    tm = x.shape[0]
    h = (x * lax.rsqrt(jnp.mean(x * x, axis=-1, keepdims=True) + EPS)) * g_ref[...]
    hb = h.astype(BF16)
    h_ref[...] = hb

    def proj(lo, hi):
        return jnp.dot(hb, wa_ref[:, lo:hi], preferred_element_type=F32)

    cq, sq, ci, si = cq_ref[...], sq_ref[...], ci_ref[...], si_ref[...]
    low_half = (lax.broadcasted_iota(jnp.int32, cq.shape, 1) & (IDX_DIM - 1)) < (IDX_DIM // 2)

    zq = proj(_PQ, _PK)
    for hd in range(N_HEADS):
        sl = slice(hd * HEAD_DIM, (hd + 1) * HEAD_DIM)
        q_ref[:, sl] = _rope128(zq[:, sl], cq, sq).astype(BF16)
    zk = proj(_PK, _PV)
    zv = proj(_PV, _PIQ)
    vb_ref[...] = zv.astype(BF16)
    for hd in range(N_KV_HEADS):
        sl = slice(hd * HEAD_DIM, (hd + 1) * HEAD_DIM)
        head_rows = pl.ds(hd, tm, stride=N_KV_HEADS)
        r = _rope128(zk[:, sl], cq, sq)
        k_ref[head_rows, :] = r
        kb_ref[:, sl] = r.astype(BF16)
        v_ref[head_rows, :] = zv[:, sl]
    ziq = proj(_PIQ, _PEND)
    for p in range(IDX_WIDTH // LANES):
        sl = slice(p * LANES, (p + 1) * LANES)
        iq_ref[:, sl] = _rope64(ziq[:, sl], ci, si, low_half).astype(BF16)
    z2 = jnp.dot(hb, wi_ref[...], preferred_element_type=F32)
    ik_ref[...] = _rope64(z2[:, :LANES], ci, si, low_half)[:, :IDX_DIM]
    iw_ref[...] = z2[:, LANES:] * IW_SCALE
    u_ref[...] = jnp.dot(hb, wu_ref[...], preferred_element_type=F32)


def _proj_call(x2d, g_mix, w_att, w_idx, w_u, tabs, tm):
    m = x2d.shape[0]
    n_tiles = m // tm
    tab_tiles = tabs[0].shape[0] // tm
    row = lambda w: pl.BlockSpec((tm, w), lambda i: (i, 0))
    tab_spec = pl.BlockSpec((tm, LANES), lambda i: (i % tab_tiles, 0))
    kv_cache = jax.ShapeDtypeStruct((m * N_KV_HEADS, HEAD_DIM), F32)
    kv_cache_spec = pl.BlockSpec((tm * N_KV_HEADS, HEAD_DIM), lambda i: (i, 0))
    out_shape = (
        jax.ShapeDtypeStruct((m, D_MODEL), BF16),
        jax.ShapeDtypeStruct((m, ATT_WIDTH), BF16),
        kv_cache,
        kv_cache,
        jax.ShapeDtypeStruct((m, KV_WIDTH), BF16),
        jax.ShapeDtypeStruct((m, KV_WIDTH), BF16),
        jax.ShapeDtypeStruct((m, IDX_WIDTH), BF16),
        jax.ShapeDtypeStruct((m, IDX_DIM), F32),
        jax.ShapeDtypeStruct((m, LANES), F32),
        jax.ShapeDtypeStruct((m, S5_WIDTH), F32),
    )
    out_specs = (row(D_MODEL), row(ATT_WIDTH), kv_cache_spec, kv_cache_spec, row(KV_WIDTH), row(KV_WIDTH),
                 row(IDX_WIDTH), row(IDX_DIM), row(LANES), row(S5_WIDTH))
    return pl.pallas_call(
        _proj_body,
        grid=(n_tiles,),
        in_specs=[row(D_MODEL), _const_spec((1, D_MODEL)),
                  _const_spec(w_att.shape), _const_spec(w_idx.shape), _const_spec(w_u.shape),
                  tab_spec, tab_spec, tab_spec, tab_spec],
        out_specs=out_specs,
        out_shape=out_shape,
        compiler_params=_cparams("parallel"),
        name="proj",
    )(x2d, g_mix, w_att, w_idx, w_u, *tabs)


def _gates_body(h_ref, w_ref, o_ref):
    o_ref[...] = jax.nn.sigmoid(jnp.dot(h_ref[...], w_ref[...], preferred_element_type=F32))


def _gates_call(h, w_gl, tm):
    m, n = h.shape[0], w_gl.shape[1]
    return pl.pallas_call(
        _gates_body,
        grid=(m // tm,),
        in_specs=[pl.BlockSpec((tm, D_MODEL), lambda i: (i, 0)), _const_spec(w_gl.shape)],
        out_specs=pl.BlockSpec((tm, n), lambda i: (i, 0)),
        out_shape=jax.ShapeDtypeStruct((m, n), F32),
        compiler_params=_cparams("parallel"),
        name="gates",
    )(h, w_gl)


LOG2_E = math.log2(math.e)
NEG_INF_KEY = INT_MIN + 0x7FFFFF
F32_MAX = float(np.finfo(np.float32).max)


def _key_to_f32(key):
    return lax.bitcast_convert_type(key ^ ((key >> 31) & jnp.int32(0x7FFFFFFF)), F32)


def _row_count(mask):
    return jnp.sum(jnp.where(mask, 1.0, 0.0), axis=1, keepdims=True)


def _resolve_threshold_ties(score, t_lo, t_next, excess, kf, s_keys, bias_ref, cand_ref, pick_ref, rem_ref):
    tied_row = excess > 0.0
    ge = score >= t_lo
    above = score >= t_next
    cand_ref[...] = jnp.where(ge & jnp.logical_not(above) & tied_row, 1.0, 0.0)
    pick_ref[...] = jnp.zeros_like(score)
    rem0 = jnp.where(tied_row, kf - _row_count(above), 0.0)
    rem_ref[...] = jnp.broadcast_to(rem0, rem_ref.shape)
    col = lax.broadcasted_iota(jnp.int32, score.shape, 1)

    def take_next_value(_):
        cand = cand_ref[...] > 0.5
        rem = rem_ref[:, :1]
        top = jnp.max(jnp.where(cand, score, -jnp.inf), axis=1, keepdims=True)
        eq = cand & (score == top)
        last = jnp.zeros((score.shape[0], 1), jnp.int32)
        for b in range(int(s_keys).bit_length() - 1, -1, -1):
            nxt = last + jnp.int32(1 << b)
            last = jnp.where(_row_count(eq & (col < nxt)) < rem, nxt, last)
        take = eq & (col <= last) & (rem > 0.0)
        pick_ref[...] = jnp.where(take, 1.0, pick_ref[...])
        cand_ref[...] = jnp.where(eq, 0.0, cand_ref[...])
        rem = jnp.where(top > -jnp.inf, rem - _row_count(take), 0.0)
        rem_ref[...] = jnp.broadcast_to(rem, rem_ref.shape)
        return jnp.max(rem) > 0.0

    lax.while_loop(lambda go: go, take_next_value, jnp.max(rem0) > 0.0)
    chosen = above | (pick_ref[...] > 0.5)
    bias_ref[...] = jnp.where(tied_row, jnp.where(chosen, 0.0, -jnp.inf), bias_ref[...])


def _dsa_core(q_ref, iq_ref, iw_ref, get_k, get_v, get_ikbd, n_keysets,
              o_ref, score_ref, bias_ref, cand_ref, pick_ref, rem_ref, *,
              s_keys, s_chunk, blk0, fixed_valid, topk):
    qrows = q_ref.shape[0]
    halves = [(slice(hf * CHUNK, (hf + 1) * CHUNK), hf % n_keysets) for hf in range(qrows // CHUNK)]

    pairs = IDX_WIDTH // LANES
    for hf, (rows, ks) in enumerate(halves):
        if fixed_valid is None:
            valid = (blk0 + len(halves) * pl.program_id(1) + hf + 1) * CHUNK
        else:
            valid = fixed_valid
        iq = iq_ref[rows, :]
        lhs = jnp.concatenate([iq[:, p * LANES:(p + 1) * LANES] for p in range(pairs)], axis=0)
        iw = iw_ref[rows, :]
        for c in range(s_keys // s_chunk):
            logits = lax.dot_general(lhs, get_ikbd(ks, c), (((1,), (1,)), ((), ())), preferred_element_type=F32)
            acc = jnp.zeros((CHUNK, s_chunk), F32)
            for p in range(pairs):
                lp = logits[p * CHUNK:(p + 1) * CHUNK]
                acc = acc + jnp.maximum(lp[:, :s_chunk], 0.0) * iw[:, 2 * p:2 * p + 1]
                acc = acc + jnp.maximum(lp[:, s_chunk:], 0.0) * iw[:, 2 * p + 1:2 * p + 2]
            col = c * s_chunk + lax.broadcasted_iota(jnp.int32, acc.shape, 1)
            score_ref[rows, c * s_chunk:(c + 1) * s_chunk] = jnp.where(col < valid, acc, -jnp.inf)

    score = score_ref[...]
    if s_keys <= topk:
        bias_ref[...] = jnp.where(score > -jnp.inf, 0.0, -jnp.inf)
    else:
        kf = float(topk)
        score_t = score.T

        def count_ge(t):
            ind = jnp.where(score_t >= t, 1.0, 0.0).reshape(s_keys // SEARCH_PART, SEARCH_PART, qrows)
            return jnp.sum(jnp.sum(ind, axis=0), axis=0, keepdims=True)

        thr = jnp.full((1, qrows), INT_MIN, jnp.int32)
        for b in range(31, -1, -1):
            cand = thr + jnp.int32(INT_MIN if b == 31 else 1 << b)
            thr = jnp.where(count_ge(_key_to_f32(cand)) >= kf, cand, thr)
        thr = jnp.maximum(thr, jnp.int32(NEG_INF_KEY))
        t_lo = jnp.maximum(_key_to_f32(thr), -F32_MAX)
        excess = count_ge(t_lo) - kf
        stats = jnp.concatenate([t_lo, _key_to_f32(thr + 1), excess, jnp.zeros((qrows - 3, qrows), F32)], axis=0).T
        t_lo_col = stats[:, 0:1]
        bias_ref[...] = jnp.where(score >= t_lo_col, 0.0, -jnp.inf)

        @pl.when(jnp.max(excess) > 0.0)
        def _():
            _resolve_threshold_ties(score, t_lo_col, stats[:, 1:2], stats[:, 2:3], kf, s_keys,
                                    bias_ref, cand_ref, pick_ref, rem_ref)

    scale = HEAD_DIM ** -0.5
    for rows, ks in halves:
        bias = bias_ref[rows, :]
        q = q_ref[rows, :]
        for c in range(N_KV_HEADS):
            kc = get_k(ks, c)
            vc = get_v(ks, c)
            qc = jnp.concatenate(
                [q[:, (c * GROUP + g) * HEAD_DIM:(c * GROUP + g + 1) * HEAD_DIM] for g in range(GROUP)], axis=0)
            logits = lax.dot_general(qc, kc, (((1,), (1,)), ((), ())), preferred_element_type=F32)
            es, inv = [], []
            for g in range(GROUP):
                lg = logits[g * CHUNK:(g + 1) * CHUNK] + bias
                e = jnp.exp2((lg - jnp.max(lg, axis=1, keepdims=True)) * (scale * LOG2_E))
                inv.append(1.0 / jnp.sum(e, axis=1, keepdims=True))
                es.append(e.astype(BF16))
            oc = jnp.dot(jnp.concatenate(es, axis=0), vc, preferred_element_type=F32)
            for g in range(GROUP):
                hd = c * GROUP + g
                o_ref[rows, hd * HEAD_DIM:(hd + 1) * HEAD_DIM] = (
                    oc[g * CHUNK:(g + 1) * CHUNK] * inv[g]).astype(BF16)


def _dsa_body(q_ref, iq_ref, iw_ref, kb_ref, vb_ref, ikbd_ref, *rest, **static):
    head = lambda ref: (lambda ks, c: ref[ks, :, c * HEAD_DIM:(c + 1) * HEAD_DIM])
    _dsa_core(q_ref, iq_ref, iw_ref, head(kb_ref), head(vb_ref), lambda ks, c: ikbd_ref[ks, c],
              kb_ref.shape[0], *rest[-6:], **static)


def _dsa_cached_body(q_ref, iq_ref, iw_ref, kn_ref, vn_ref, ikn_ref, pk_ref, pv_ref, pik_ref,
                     o_ref, k_all, v_all, ikbd_all, *scratch, past_len, **static):
    s_keys = static['s_keys']
    new_end = past_len + CHUNK
    for ks in range(2):
        new = slice(ks * CHUNK, (ks + 1) * CHUNK)
        for c in range(N_KV_HEADS):
            for dst, past, fresh in ((k_all, pk_ref, kn_ref), (v_all, pv_ref, vn_ref)):
                dst[ks, c, :past_len, :] = past[ks, pl.ds(c, past_len, stride=N_KV_HEADS), :].astype(BF16)
                dst[ks, c, past_len:new_end, :] = fresh[new, c * HEAD_DIM:(c + 1) * HEAD_DIM]
                dst[ks, c, new_end:, :] = jnp.zeros((s_keys - new_end, HEAD_DIM), BF16)
        ik = jnp.concatenate([pik_ref[ks].astype(BF16), ikn_ref[new, :].astype(BF16),
                              jnp.zeros((s_keys - new_end, IDX_DIM), BF16)], axis=0)
        z = jnp.zeros_like(ik)
        ikbd_all[ks, :s_keys, :] = jnp.concatenate([ik, z], axis=1)
        ikbd_all[ks, s_keys:, :] = jnp.concatenate([z, ik], axis=1)
    _dsa_core(q_ref, iq_ref, iw_ref, lambda ks, c: k_all[ks, c], lambda ks, c: v_all[ks, c],
              lambda ks, c: ikbd_all[ks], 2, o_ref, *scratch, **static)


def _dsa_cached_call(q, iq, iw, kb, vb, ik, pk, pv, pik, topk):
    nb, past_len = pik.shape[0], pik.shape[1]
    l_keys = past_len + CHUNK
    s_keys = -(-l_keys // LANES) * LANES
    body = functools.partial(_dsa_cached_body, past_len=past_len, s_keys=s_keys, s_chunk=s_keys, blk0=0,
                             fixed_valid=l_keys, topk=topk)
    qrow = lambda w: pl.BlockSpec((QROWS, w), lambda n: (n, 0))
    pair = lambda a: pl.BlockSpec((2,) + a.shape[1:], lambda n: (n, 0, 0))
    mask_buf = pltpu.VMEM((QROWS, s_keys), F32)
    heads = pltpu.VMEM((2, N_KV_HEADS, s_keys, HEAD_DIM), BF16)
    return pl.pallas_call(
        body,
        grid=(nb // 2,),
        in_specs=[qrow(ATT_WIDTH), qrow(IDX_WIDTH), qrow(LANES), qrow(KV_WIDTH), qrow(KV_WIDTH), qrow(IDX_DIM),
                  pair(pk), pair(pv), pair(pik)],
        out_specs=qrow(ATT_WIDTH),
        out_shape=jax.ShapeDtypeStruct((nb * CHUNK, ATT_WIDTH), BF16),
        scratch_shapes=[heads, heads, pltpu.VMEM((2, 2 * s_keys, LANES), BF16),
                        mask_buf, mask_buf, mask_buf, mask_buf, pltpu.VMEM((QROWS, LANES), F32)],
        compiler_params=_cparams("parallel"),
        name="dsa_cached",
    )(q, iq, iw, kb, vb, ik, pk, pv, pik)


def _dsa_call(q, iq, iw, kb, vb, ikbd, o_prev, qrows, blk0, n_blk, blks_per_seq, s_keys, s_chunk, topk):
    nb = kb.shape[0]
    blocks_per_step = qrows // CHUNK
    steps_per_seq = blks_per_seq // blocks_per_step
    qrow = lambda w: pl.BlockSpec(
        (qrows, w), lambda n, j: (n * steps_per_seq + blk0 // blocks_per_step + j, 0))
    n_chunks = s_keys // s_chunk
    body = functools.partial(_dsa_body, s_keys=s_keys, s_chunk=s_chunk, blk0=blk0, fixed_valid=None, topk=topk)
    mask_buf = pltpu.VMEM((qrows, s_keys), F32)
    return pl.pallas_call(
        body,
        grid=(nb, n_blk // blocks_per_step),
        in_specs=[qrow(ATT_WIDTH), qrow(IDX_WIDTH), qrow(LANES),
                  pl.BlockSpec((1, s_keys, KV_WIDTH), lambda n, j: (n, 0, 0)),
                  pl.BlockSpec((1, s_keys, KV_WIDTH), lambda n, j: (n, 0, 0)),
                  pl.BlockSpec((1, n_chunks, 2 * s_chunk, LANES), lambda n, j: (n, 0, 0, 0)),
                  pl.BlockSpec(memory_space=pl.ANY)],
        out_specs=qrow(ATT_WIDTH),
        out_shape=jax.ShapeDtypeStruct((nb * blks_per_seq * CHUNK, ATT_WIDTH), BF16),
        scratch_shapes=[mask_buf, mask_buf, mask_buf, mask_buf, pltpu.VMEM((qrows, LANES), F32)],
        input_output_aliases={6: 0},
        compiler_params=_cparams("parallel", "arbitrary"),
        name="dsa",
    )(q, iq, iw, kb, vb, ikbd, o_prev)


def _indexer_key_blocks(ik, s_chunk):
    nb, s, _ = ik.shape
    ikb = ik.astype(BF16).reshape(nb, s // s_chunk, s_chunk, IDX_DIM)
    z = jnp.zeros_like(ikb)
    return jnp.concatenate([jnp.concatenate([ikb, z], axis=-1), jnp.concatenate([z, ikb], axis=-1)], axis=-2)


def _s5prep_body(are_ref, aim_ref, ldt_ref, bre_ref, bim_ref, abre_ref, abim_ref, bbre_ref, bbim_ref):
    a_re, a_im = are_ref[...], aim_ref[...]
    dt = jnp.exp(ldt_ref[...])
    mag = jnp.exp(dt * a_re)
    ab_re = mag * jnp.cos(dt * a_im)
    ab_im = mag * jnp.sin(dt * a_im)
    den = a_re * a_re + a_im * a_im
    f_re = ((ab_re - 1.0) * a_re + ab_im * a_im) / den
    f_im = (ab_im * a_re - (ab_re - 1.0) * a_im) / den
    abre_ref[...] = ab_re
    abim_ref[...] = ab_im
    for c in range(S5_CH):
        b_re, b_im = bre_ref[c], bim_ref[c]
        bbre_ref[c] = f_re * b_re - f_im * b_im
        bbim_ref[c] = f_re * b_im + f_im * b_re


def _s5prep_call(a_re, a_im, log_dt, b_re, b_im):
    rows = S5_LANES // LANES
    flat = lambda a: a.reshape(rows, LANES)
    ldt = jnp.broadcast_to(log_dt[:, None], (S5_GROUPS, S5_STATE))
    chan_major = lambda b: jnp.transpose(b, (2, 0, 1)).reshape(S5_CH, rows, LANES)
    small = jax.ShapeDtypeStruct((rows, LANES), F32)
    big = jax.ShapeDtypeStruct((S5_CH, rows, LANES), F32)
    return pl.pallas_call(
        _s5prep_body,
        out_shape=(small, small, big, big),
        name="s5prep",
    )(flat(a_re), flat(a_im), flat(ldt), chan_major(b_re), chan_major(b_im))


def _gelu_tanh(x):
    return 0.5 * x * (1.0 + jnp.tanh(math.sqrt(2.0 / math.pi) * (x + 0.044715 * (x * x * x))))


def _s5_body(u_ref, x0re_ref, x0im_ref, abre_ref, abim_ref, bre_ref, bim_ref, cre_ref, cim_ref,
             dskip_ref, wglu_ref, bglu_ref, ob_ref, sre_ref, sim_ref, utm, otm, xre, xim, st_re, st_im,
             *, tc, nb):
    step = pl.program_id(0)

    @pl.when(step == 0)
    def _():
        st_re[...] = x0re_ref[...]
        st_im[...] = x0im_ref[...]

    for n in range(nb):
        for k in range(S5_SLABS):
            utm[k, pl.ds(n, tc, stride=nb), :] = u_ref[n, :, k * LANES:(k + 1) * LANES]

    def project_in(k):
        us = utm[k].astype(BF16)
        sl = slice(k * S5_SLAB_STATE, (k + 1) * S5_SLAB_STATE)
        xre[:, :, sl] = jnp.dot(us, bre_ref[k], preferred_element_type=F32).reshape(tc, nb, S5_SLAB_STATE)
        xim[:, :, sl] = jnp.dot(us, bim_ref[k], preferred_element_type=F32).reshape(tc, nb, S5_SLAB_STATE)

    def scan(k):
        sl = slice(k * S5_SLAB_STATE, (k + 1) * S5_SLAB_STATE)
        a_r = jnp.broadcast_to(abre_ref[:, sl], (nb, S5_SLAB_STATE))
        a_i = jnp.broadcast_to(abim_ref[:, sl], (nb, S5_SLAB_STATE))
        s_r, s_i = st_re[:, sl], st_im[:, sl]
        for t in range(tc):
            s_r, s_i = (a_r * s_r - a_i * s_i + xre[t, :, sl],
                        a_r * s_i + a_i * s_r + xim[t, :, sl])
            xre[t, :, sl] = s_r
            xim[t, :, sl] = s_i
        st_re[:, sl] = s_r
        st_im[:, sl] = s_i

    def project_out(k):
        sl = slice(k * S5_SLAB_STATE, (k + 1) * S5_SLAB_STATE)
        xr = xre[:, :, sl].reshape(tc * nb, S5_SLAB_STATE).astype(BF16)
        xi = xim[:, :, sl].reshape(tc * nb, S5_SLAB_STATE).astype(BF16)
        return (jnp.dot(xr, cre_ref[k], preferred_element_type=F32)
                - jnp.dot(xi, cim_ref[k], preferred_element_type=F32)
                + dskip_ref[:, k * LANES:(k + 1) * LANES] * utm[k])

    project_in(0)
    ys = []
    for k in range(S5_SLABS):
        if k + 1 < S5_SLABS:
            project_in(k + 1)
        scan(k)
        ys.append(project_out(k))
    yb = _gelu_tanh(jnp.concatenate(ys, axis=1))
    gate = jax.nn.sigmoid(jnp.dot(yb.astype(BF16), wglu_ref[...], preferred_element_type=F32) + bglu_ref[...])
    o = yb * gate
    for k in range(S5_SLABS):
        otm[k] = o[:, k * LANES:(k + 1) * LANES]
    for n in range(nb):
        for k in range(S5_SLABS):
            ob_ref[n, :, k * LANES:(k + 1) * LANES] = otm[k, pl.ds(n, tc, stride=nb), :].astype(BF16)

    @pl.when(step == pl.num_programs(0) - 1)
    def _():
        sre_ref[...] = st_re[...]
        sim_ref[...] = st_im[...]


def _s5_call(u, x0_re, x0_im, ab_re, ab_im, b_re_bd, b_im_bd, c_re_bd, c_im_bd, d_skip, w_glu, b_glu, tc):
    nb, t_len, _ = u.shape
    body = functools.partial(_s5_body, tc=tc, nb=nb)
    state = jax.ShapeDtypeStruct((nb, S5_LANES), F32)
    seq_spec = pl.BlockSpec((nb, tc, S5_WIDTH), lambda i: (0, i, 0))
    slabs = pltpu.VMEM((S5_SLABS, tc * nb, LANES), F32)
    return pl.pallas_call(
        body,
        grid=(t_len // tc,),
        in_specs=[seq_spec,
                  _const_spec((nb, S5_LANES)), _const_spec((nb, S5_LANES)),
                  _const_spec((1, S5_LANES)), _const_spec((1, S5_LANES)),
                  _const_spec(b_re_bd.shape), _const_spec(b_im_bd.shape),
                  _const_spec(c_re_bd.shape), _const_spec(c_im_bd.shape),
                  _const_spec((1, S5_WIDTH)), _const_spec((S5_WIDTH, S5_WIDTH)), _const_spec((1, S5_WIDTH))],
        out_specs=(seq_spec, _const_spec((nb, S5_LANES)), _const_spec((nb, S5_LANES))),
        out_shape=(jax.ShapeDtypeStruct((nb, t_len, S5_WIDTH), BF16), state, state),
        scratch_shapes=[slabs, slabs,
                        pltpu.VMEM((tc, nb, S5_LANES), F32), pltpu.VMEM((tc, nb, S5_LANES), F32),
                        pltpu.VMEM((nb, S5_LANES), F32), pltpu.VMEM((nb, S5_LANES), F32)],
        compiler_params=_cparams("arbitrary"),
        name="s5",
    )(u, x0_re, x0_im, ab_re, ab_im, b_re_bd, b_im_bd, c_re_bd, c_im_bd, d_skip, w_glu, b_glu)


def _block_diag_slabs(w):
    g, r, c = w.shape
    w = w.reshape(S5_SLABS, S5_SLAB_GROUPS, r, c)
    eye = jnp.eye(S5_SLAB_GROUPS, dtype=w.dtype)
    bd = w[:, :, :, None, :] * eye[None, :, None, :, None]
    return bd.reshape(S5_SLABS, S5_SLAB_GROUPS * r, S5_SLAB_GROUPS * c)


def _merge_body(oa_ref, ob_ref, ga_ref, gb_ref, x_ref, wa_ref, wb_ref, wo_ref, g_ref, x1_ref, hf_ref):
    pa = jnp.dot(oa_ref[...], wa_ref[...], preferred_element_type=F32)
    pb = jnp.dot(ob_ref[...], wb_ref[...], preferred_element_type=F32)
    merged = ga_ref[...] * pa + gb_ref[...] * pb
    x1 = x_ref[...] + jnp.dot(merged.astype(BF16), wo_ref[...], preferred_element_type=F32)
    x1_ref[...] = x1
    hf = (x1 * lax.rsqrt(jnp.mean(x1 * x1, axis=-1, keepdims=True) + EPS)) * g_ref[...]
    hf_ref[...] = hf.astype(BF16)


def _merge_call(oa, ob, gates, x2d, w_a, w_b, w_o, g_ffn, tm):
    m = x2d.shape[0]
    row = lambda w: pl.BlockSpec((tm, w), lambda i: (i, 0))
    return pl.pallas_call(
        _merge_body,
        grid=(m // tm,),
        in_specs=[row(ATT_WIDTH), row(S5_WIDTH),
                  pl.BlockSpec((tm, D_MODEL), lambda i: (i, 0)), pl.BlockSpec((tm, D_MODEL), lambda i: (i, 1)),
                  row(D_MODEL), _const_spec(w_a.shape), _const_spec(w_b.shape), _const_spec(w_o.shape),
                  _const_spec((1, D_MODEL))],
        out_specs=(row(D_MODEL), row(D_MODEL)),
        out_shape=(jax.ShapeDtypeStruct((m, D_MODEL), F32), jax.ShapeDtypeStruct((m, D_MODEL), BF16)),
        compiler_params=_cparams("parallel"),
        name="merge",
    )(oa, ob, gates, gates, x2d, w_a, w_b, w_o, g_ffn)


FFN_OUT_CHUNK = 512


def _ffn_body(hf_ref, x1_hbm, wg_ref, wu_ref, wd_ref, g_ref, y_ref, x1_sem, *, final_norm, tm):
    i, f = pl.program_id(0), pl.program_id(1)
    x1_copy = pltpu.make_async_copy(x1_hbm.at[pl.ds(i * tm, tm), :], y_ref, x1_sem)

    @pl.when(f == 0)
    def _():
        x1_copy.start()

    hf = hf_ref[...]
    a = jax.nn.silu(jnp.dot(hf, wg_ref[...], preferred_element_type=F32)) * jnp.dot(
        hf, wu_ref[...], preferred_element_type=F32)
    ab = a.astype(BF16)

    @pl.when(f == 0)
    def _():
        x1_copy.wait()

    for c0 in range(0, D_MODEL, FFN_OUT_CHUNK):
        cols = slice(c0, c0 + FFN_OUT_CHUNK)
        y_ref[:, cols] += jnp.dot(ab, wd_ref[:, cols], preferred_element_type=F32)

    if final_norm:
        @pl.when(f == pl.num_programs(1) - 1)
        def _():
            x2 = y_ref[...]
            y_ref[...] = (x2 * lax.rsqrt(jnp.mean(x2 * x2, axis=-1, keepdims=True) + EPS)) * g_ref[...]


def _ffn_call(hf, x1, w_gate, w_up, w_down, g_final, tm, tf, final_norm):
    m = hf.shape[0]
    return pl.pallas_call(
        functools.partial(_ffn_body, final_norm=final_norm, tm=tm),
        grid=(m // tm, D_FF // tf),
        in_specs=[pl.BlockSpec((tm, D_MODEL), lambda i, f: (i, 0)),
                  pl.BlockSpec(memory_space=pl.ANY),
                  pl.BlockSpec((D_MODEL, tf), lambda i, f: (0, f)),
                  pl.BlockSpec((D_MODEL, tf), lambda i, f: (0, f)),
                  pl.BlockSpec((tf, D_MODEL), lambda i, f: (f, 0)),
                  _const_spec((1, D_MODEL))],
        out_specs=pl.BlockSpec((tm, D_MODEL), lambda i, f: (i, 0)),
        out_shape=jax.ShapeDtypeStruct((m, D_MODEL), F32),
        scratch_shapes=[pltpu.SemaphoreType.DMA(())],
        compiler_params=_cparams("parallel", "arbitrary"),
        name="ffn",
    )(hf, x1, w_gate, w_up, w_down, g_final)


def _rope_tables(pos, dim):
    half = dim // 2
    inv = 1.0 / (ROPE_THETA ** (jnp.arange(half, dtype=F32) * (2.0 / dim)))
    ang = pos[:, None] * inv[None, :]
    cos, sin = jnp.cos(ang), jnp.sin(ang)
    reps = LANES // dim
    return (jnp.tile(jnp.concatenate([cos, cos], axis=-1), (1, reps)),
            jnp.tile(jnp.concatenate([-sin, sin], axis=-1), (1, reps)))


def _split_w_in_body(w_ref, att_ref, idx_ref, u_ref, gl_ref):
    o = IN_OFFS
    rows = w_ref.shape[0]
    seg = lambda i, j: w_ref[:, o[i]:o[j]].astype(BF16)
    pad = lambda w: jnp.concatenate([w, jnp.zeros((rows, LANES - w.shape[1]), BF16)], axis=1)
    att_ref[...] = seg(0, 4)
    idx_ref[...] = jnp.concatenate([pad(seg(4, 5)), pad(seg(5, 6))], axis=1)
    u_ref[...] = seg(6, 7)
    gl_ref[...] = seg(7, 8)


def _split_w_in(w_in, layer):
    _, d, width = w_in.shape
    o = IN_OFFS
    widths = (o[4] - o[0], 2 * LANES, o[7] - o[6], o[8] - o[7])
    return pl.pallas_call(
        _split_w_in_body,
        grid=(d // W_IN_SPLIT_ROWS,),
        in_specs=[pl.BlockSpec((None, W_IN_SPLIT_ROWS, width), lambda i: (layer, i, 0))],
        out_specs=tuple(pl.BlockSpec((W_IN_SPLIT_ROWS, w), lambda i: (i, 0)) for w in widths),
        out_shape=tuple(jax.ShapeDtypeStruct((d, w), BF16) for w in widths),
        compiler_params=_cparams("parallel"),
        name="split_w_in",
    )(w_in)


def _layer(x, pos, past, ssm0, lw):
    nb, t_len, _ = x.shape
    m = nb * t_len
    x2d = x.reshape(m, D_MODEL)

    cq, sq = _rope_tables(pos, HEAD_DIM)
    ci, si = _rope_tables(pos, IDX_DIM)
    tabs = (cq, sq, ci, si)
    if t_len < PROJ_ROWS:
        tabs = tuple(jnp.tile(t, (PROJ_ROWS // t_len, 1)) for t in tabs)

    h, q, k, v, kb, vb, iq, ik, iw, u = _proj_call(
        x2d, lw['g_mix'], lw['w_att'], lw['w_idx'], lw['w_u'], tabs, PROJ_ROWS)
    gates = _gates_call(h, lw['w_gl'], PROJ_ROWS)

    if past is None:
        s_chunk = DSA_GROUP_BLOCKS * CHUNK
        kb_all = kb.reshape(nb, t_len, KV_WIDTH)
        vb_all = vb.reshape(nb, t_len, KV_WIDTH)
        ikbd = _indexer_key_blocks(ik.reshape(nb, t_len, IDX_DIM), s_chunk)
        n_blk = t_len // CHUNK
        o_a = jnp.zeros((m, ATT_WIDTH), BF16)
        for blk0 in range(0, n_blk, DSA_GROUP_BLOCKS):
            s_keys = (blk0 + DSA_GROUP_BLOCKS) * CHUNK
            qrows = DSA_GROUP_BLOCKS * CHUNK if s_keys <= DSA_WIDE_STEP_MAX_KEYS else QROWS
            o_a = _dsa_call(q, iq, iw, kb_all, vb_all, ikbd, o_a, qrows, blk0, DSA_GROUP_BLOCKS,
                            n_blk, s_keys, s_chunk, min(TOPK_MAX, t_len // 4))
    else:
        pk, pv, pik = past
        assert t_len == CHUNK, "the cached stream is one query block per sequence"
        l_keys = pk.shape[1] + t_len
        o_a = _dsa_cached_call(q, iq, iw, kb, vb, ik, pk.reshape(nb, -1, HEAD_DIM), pv.reshape(nb, -1, HEAD_DIM),
                               pik, min(TOPK_MAX, l_keys // 4))

    x0_re, x0_im = ssm0
    o_b, s_re, s_im = _s5_call(
        u.reshape(nb, t_len, S5_WIDTH), x0_re.reshape(nb, S5_LANES), x0_im.reshape(nb, S5_LANES),
        lw['ab_re'], lw['ab_im'], lw['b_re_bd'], lw['b_im_bd'], lw['c_re_bd'], lw['c_im_bd'],
        lw['d_skip'], lw['w_glu'], lw['b_glu'], min(t_len, S5_STEP_ROWS // nb))

    x1, hf = _merge_call(o_a, o_b.reshape(m, S5_WIDTH), gates, x2d, lw['w_proj_a'], lw['w_proj_b'],
                         lw['w_out'], lw['g_ffn'], MERGE_ROWS)
    caches = (k.reshape(nb, t_len, N_KV_HEADS, HEAD_DIM), v.reshape(nb, t_len, N_KV_HEADS, HEAD_DIM),
              ik.reshape(nb, t_len, IDX_DIM), s_re.reshape(nb, S5_GROUPS, S5_STATE),
              s_im.reshape(nb, S5_GROUPS, S5_STATE))
    return x1, hf, caches


def kernel(x_prompt, x_sample, cache_k, cache_v, cache_idx_k, state_ssm_re, state_ssm_im,
           g_mix, w_in, a_re, a_im, log_dt, b_re, b_im, c_re, c_im, d_skip, w_glu, b_glu,
           w_proj_a, w_proj_b, w_out, g_ffn, w_gate, w_up, w_down, g_final):
    depth = w_in.shape[0]
    t_p, t_s = x_prompt.shape[1], x_sample.shape[1]
    past_len = cache_k.shape[2]
    pos_p = jnp.arange(t_p, dtype=F32)
    pos_s = past_len + jnp.arange(t_s, dtype=F32)
    nb_p, nb_s = x_prompt.shape[0], x_sample.shape[0]
    g_fin = g_final.reshape(1, D_MODEL)

    hp, hs = x_prompt, x_sample
    outs_p, outs_s = [], []
    for l in range(depth):
        w_att, w_idx, w_u, w_gl = _split_w_in(w_in, l)
        ab_re, ab_im, bb_re, bb_im = _s5prep_call(a_re[l], a_im[l], log_dt[l], b_re[l], b_im[l])
        per_group = lambda bb: jnp.transpose(bb.reshape(S5_CH, S5_GROUPS, S5_STATE), (1, 0, 2))
        lw = {
            'g_mix': g_mix[l].reshape(1, D_MODEL), 'w_att': w_att, 'w_idx': w_idx, 'w_u': w_u, 'w_gl': w_gl,
            'ab_re': ab_re.reshape(1, S5_LANES), 'ab_im': ab_im.reshape(1, S5_LANES),
            'b_re_bd': _block_diag_slabs(per_group(bb_re)).astype(BF16),
            'b_im_bd': _block_diag_slabs(per_group(bb_im)).astype(BF16),
            'c_re_bd': _block_diag_slabs(jnp.transpose(c_re[l], (0, 2, 1))).astype(BF16),
            'c_im_bd': _block_diag_slabs(jnp.transpose(c_im[l], (0, 2, 1))).astype(BF16),
            'd_skip': d_skip[l].reshape(1, S5_WIDTH), 'w_glu': w_glu[l].astype(BF16),
            'b_glu': b_glu[l].reshape(1, S5_WIDTH),
            'w_proj_a': w_proj_a[l].astype(BF16), 'w_proj_b': w_proj_b[l].astype(BF16),
            'w_out': w_out[l].astype(BF16), 'g_ffn': g_ffn[l].reshape(1, D_MODEL),
        }
        wg, wu, wd = w_gate[l].astype(BF16), w_up[l].astype(BF16), w_down[l].astype(BF16)
        last = l == depth - 1

        zeros = jnp.zeros((nb_p, S5_GROUPS, S5_STATE), F32)
        x1p, hfp, cp = _layer(hp, pos_p, None, (zeros, zeros), lw)
        x1s, hfs, cs = _layer(hs, pos_s, (cache_k[l], cache_v[l], cache_idx_k[l]),
                              (state_ssm_re[l], state_ssm_im[l]), lw)
        yp = _ffn_call(hfp, x1p, wg, wu, wd, g_fin, FFN_ROWS, FFN_FF_TILE, last)
        ys = _ffn_call(hfs, x1s, wg, wu, wd, g_fin, FFN_ROWS, FFN_FF_TILE, last)
        hp = yp.reshape(nb_p, t_p, D_MODEL)
        hs = ys.reshape(nb_s, t_s, D_MODEL)
        outs_p.append(cp)
        outs_s.append(cs)

    stack = lambda outs, i: jnp.stack([o[i] for o in outs])
    return (hp, hs,
            stack(outs_p, 0), stack(outs_p, 1), stack(outs_p, 2), stack(outs_p, 3), stack(outs_p, 4),
            stack(outs_s, 0), stack(outs_s, 1), stack(outs_s, 2), stack(outs_s, 3), stack(outs_s, 4))
```

```python
import functools
import math

import jax
import jax.numpy as jnp
import numpy as np
from jax import lax
from jax.experimental import pallas as pl
from jax.experimental.pallas import tpu as pltpu

F32 = jnp.float32
BF16 = jnp.bfloat16

D_MODEL = 2048
CHUNK = 64
HEAD_DIM = 128
N_HEADS = 8
N_KV_HEADS = 2
GROUP = N_HEADS // N_KV_HEADS
ATT_WIDTH = N_HEADS * HEAD_DIM
KV_WIDTH = N_KV_HEADS * HEAD_DIM
IDX_HEADS = 16
IDX_DIM = 64
IDX_WIDTH = IDX_HEADS * IDX_DIM
TOPK_MAX = 256
S5_CH = 16
S5_WIDTH = D_MODEL // 2
S5_GROUPS = S5_WIDTH // S5_CH
S5_STATE = 64
S5_LANES = S5_GROUPS * S5_STATE
D_FF = 5632
ROPE_THETA = 10000.0
EPS = 1e-6
IN_SIZES = (ATT_WIDTH, KV_WIDTH, KV_WIDTH, IDX_WIDTH, IDX_DIM, IDX_HEADS, S5_WIDTH, 2 * D_MODEL)
IN_OFFS = tuple(int(s) for s in np.cumsum((0,) + IN_SIZES))

LANES = 128
S5_SLAB_GROUPS = LANES // S5_CH
S5_SLABS = S5_GROUPS // S5_SLAB_GROUPS
S5_SLAB_STATE = S5_SLAB_GROUPS * S5_STATE
INT_MIN = -2 ** 31
DSA_GROUP_BLOCKS = 4
QROWS = 2 * CHUNK
DSA_WIDE_STEP_MAX_KEYS = 1280
S5_STEP_ROWS = 512
SEARCH_PART = 128

VMEM_LIMIT = 56 * 2 ** 20
PROJ_ROWS = 512
MERGE_ROWS = 256
FFN_ROWS = 1024
FFN_FF_TILE = 512
W_IN_SPLIT_ROWS = 256


def _cparams(*sem):
    return pltpu.CompilerParams(dimension_semantics=sem, vmem_limit_bytes=VMEM_LIMIT)


def _const_spec(shape):
    nd = len(shape)
    return pl.BlockSpec(shape, lambda *_: (0,) * nd, pipeline_mode=pl.Buffered(1))


_PQ = 0
_PK = _PQ + ATT_WIDTH
_PV = _PK + KV_WIDTH
_PIQ = _PV + KV_WIDTH
_PEND = _PIQ + IDX_WIDTH
IW_SCALE = (IDX_DIM ** -0.5) * (IDX_HEADS ** -0.5)


def _rope128(z, cos, sin):
    return z * cos + pltpu.roll(z, HEAD_DIM // 2, 1) * sin


def _rope64(z, cos, sin, low_half):
    partner = jnp.where(low_half, pltpu.roll(z, LANES - IDX_DIM // 2, 1), pltpu.roll(z, IDX_DIM // 2, 1))
    return z * cos + partner * sin


def _proj_body(x_ref, g_ref, wa_ref, wi_ref, wu_ref, cq_ref, sq_ref, ci_ref, si_ref,
               h_ref, q_ref, k_ref, v_ref, kb_ref, vb_ref, iq_ref, ik_ref, iw_ref, u_ref):
    x = x_ref[...]
    tm = x.shape[0]
    h = (x * lax.rsqrt(jnp.mean(x * x, axis=-1, keepdims=True) + EPS)) * g_ref[...]
    hb = h.astype(BF16)
    h_ref[...] = hb

    def proj(lo, hi):
        return jnp.dot(hb, wa_ref[:, lo:hi], preferred_element_type=F32)

    cq, sq, ci, si = cq_ref[...], sq_ref[...], ci_ref[...], si_ref[...]
    low_half = (lax.broadcasted_iota(jnp.int32, cq.shape, 1) & (IDX_DIM - 1)) < (IDX_DIM // 2)

    zq = proj(_PQ, _PK)
    for hd in range(N_HEADS):
        sl = slice(hd * HEAD_DIM, (hd + 1) * HEAD_DIM)
        q_ref[:, sl] = _rope128(zq[:, sl], cq, sq).astype(BF16)
    zk = proj(_PK, _PV)
    zv = proj(_PV, _PIQ)
    vb_ref[...] = zv.astype(BF16)
    for hd in range(N_KV_HEADS):
        sl = slice(hd * HEAD_DIM, (hd + 1) * HEAD_DIM)
        head_rows = pl.ds(hd, tm, stride=N_KV_HEADS)
        r = _rope128(zk[:, sl], cq, sq)
        k_ref[head_rows, :] = r
        kb_ref[:, sl] = r.astype(BF16)
        v_ref[head_rows, :] = zv[:, sl]
    ziq = proj(_PIQ, _PEND)
    for p in range(IDX_WIDTH // LANES):
        sl = slice(p * LANES, (p + 1) * LANES)
        iq_ref[:, sl] = _rope64(ziq[:, sl], ci, si, low_half).astype(BF16)
    z2 = jnp.dot(hb, wi_ref[...], preferred_element_type=F32)
    ik_ref[...] = _rope64(z2[:, :LANES], ci, si, low_half)[:, :IDX_DIM]
    iw_ref[...] = z2[:, LANES:] * IW_SCALE
    u_ref[...] = jnp.dot(hb, wu_ref[...], preferred_element_type=F32)


def _proj_call(x2d, g_mix, w_att, w_idx, w_u, tabs, tm):
    m = x2d.shape[0]
    n_tiles = m // tm
    tab_tiles = tabs[0].shape[0] // tm
    row = lambda w: pl.BlockSpec((tm, w), lambda i: (i, 0))
    tab_spec = pl.BlockSpec((tm, LANES), lambda i: (i % tab_tiles, 0))
    kv_cache = jax.ShapeDtypeStruct((m * N_KV_HEADS, HEAD_DIM), F32)
    kv_cache_spec = pl.BlockSpec((tm * N_KV_HEADS, HEAD_DIM), lambda i: (i, 0))
    out_shape = (
        jax.ShapeDtypeStruct((m, D_MODEL), BF16),
        jax.ShapeDtypeStruct((m, ATT_WIDTH), BF16),
        kv_cache,
        kv_cache,
        jax.ShapeDtypeStruct((m, KV_WIDTH), BF16),
        jax.ShapeDtypeStruct((m, KV_WIDTH), BF16),
        jax.ShapeDtypeStruct((m, IDX_WIDTH), BF16),
        jax.ShapeDtypeStruct((m, IDX_DIM), F32),
        jax.ShapeDtypeStruct((m, LANES), F32),
        jax.ShapeDtypeStruct((m, S5_WIDTH), F32),
    )
    out_specs = (row(D_MODEL), row(ATT_WIDTH), kv_cache_spec, kv_cache_spec, row(KV_WIDTH), row(KV_WIDTH),
                 row(IDX_WIDTH), row(IDX_DIM), row(LANES), row(S5_WIDTH))
    return pl.pallas_call(
        _proj_body,
        grid=(n_tiles,),
        in_specs=[row(D_MODEL), _const_spec((1, D_MODEL)),
                  _const_spec(w_att.shape), _const_spec(w_idx.shape), _const_spec(w_u.shape),
                  tab_spec, tab_spec, tab_spec, tab_spec],
        out_specs=out_specs,
        out_shape=out_shape,
        compiler_params=_cparams("parallel"),
        name="proj",
    )(x2d, g_mix, w_att, w_idx, w_u, *tabs)


def _gates_body(h_ref, w_ref, o_ref):
    o_ref[...] = jax.nn.sigmoid(jnp.dot(h_ref[...], w_ref[...], preferred_element_type=F32))


def _gates_call(h, w_gl, tm):
    m, n = h.shape[0], w_gl.shape[1]
    return pl.pallas_call(
        _gates_body,
        grid=(m // tm,),
        in_specs=[pl.BlockSpec((tm, D_MODEL), lambda i: (i, 0)), _const_spec(w_gl.shape)],
        out_specs=pl.BlockSpec((tm, n), lambda i: (i, 0)),
        out_shape=jax.ShapeDtypeStruct((m, n), F32),
        compiler_params=_cparams("parallel"),
        name="gates",
    )(h, w_gl)


LOG2_E = math.log2(math.e)
NEG_INF_KEY = INT_MIN + 0x7FFFFF
F32_MAX = float(np.finfo(np.float32).max)


def _key_to_f32(key):
    return lax.bitcast_convert_type(key ^ ((key >> 31) & jnp.int32(0x7FFFFFFF)), F32)


def _row_count(mask):
    return jnp.sum(jnp.where(mask, 1.0, 0.0), axis=1, keepdims=True)


def _resolve_threshold_ties(score, t_lo, t_next, excess, kf, s_keys, bias_ref, cand_ref, pick_ref, rem_ref):
    tied_row = excess > 0.0
    ge = score >= t_lo
    above = score >= t_next
    cand_ref[...] = jnp.where(ge & jnp.logical_not(above) & tied_row, 1.0, 0.0)
    pick_ref[...] = jnp.zeros_like(score)
    rem0 = jnp.where(tied_row, kf - _row_count(above), 0.0)
    rem_ref[...] = jnp.broadcast_to(rem0, rem_ref.shape)
    col = lax.broadcasted_iota(jnp.int32, score.shape, 1)

    def take_next_value(_):
        cand = cand_ref[...] > 0.5
        rem = rem_ref[:, :1]
        top = jnp.max(jnp.where(cand, score, -jnp.inf), axis=1, keepdims=True)
        eq = cand & (score == top)
        last = jnp.zeros((score.shape[0], 1), jnp.int32)
        for b in range(int(s_keys).bit_length() - 1, -1, -1):
            nxt = last + jnp.int32(1 << b)
            last = jnp.where(_row_count(eq & (col < nxt)) < rem, nxt, last)
        take = eq & (col <= last) & (rem > 0.0)
        pick_ref[...] = jnp.where(take, 1.0, pick_ref[...])
        cand_ref[...] = jnp.where(eq, 0.0, cand_ref[...])
        rem = jnp.where(top > -jnp.inf, rem - _row_count(take), 0.0)
        rem_ref[...] = jnp.broadcast_to(rem, rem_ref.shape)
        return jnp.max(rem) > 0.0

    lax.while_loop(lambda go: go, take_next_value, jnp.max(rem0) > 0.0)
    chosen = above | (pick_ref[...] > 0.5)
    bias_ref[...] = jnp.where(tied_row, jnp.where(chosen, 0.0, -jnp.inf), bias_ref[...])


def _dsa_core(q_ref, iq_ref, iw_ref, get_k, get_v, get_ikbd, n_keysets,
              o_ref, score_ref, bias_ref, cand_ref, pick_ref, rem_ref, *,
              s_keys, s_chunk, blk0, fixed_valid, topk):
    qrows = q_ref.shape[0]
    halves = [(slice(hf * CHUNK, (hf + 1) * CHUNK), hf % n_keysets) for hf in range(qrows // CHUNK)]

    pairs = IDX_WIDTH // LANES
    for hf, (rows, ks) in enumerate(halves):
        if fixed_valid is None:
            valid = (blk0 + len(halves) * pl.program_id(1) + hf + 1) * CHUNK
        else:
            valid = fixed_valid
        iq = iq_ref[rows, :]
        lhs = jnp.concatenate([iq[:, p * LANES:(p + 1) * LANES] for p in range(pairs)], axis=0)
        iw = iw_ref[rows, :]
        for c in range(s_keys // s_chunk):
            logits = lax.dot_general(lhs, get_ikbd(ks, c), (((1,), (1,)), ((), ())), preferred_element_type=F32)
            acc = jnp.zeros((CHUNK, s_chunk), F32)
            for p in range(pairs):
                lp = logits[p * CHUNK:(p + 1) * CHUNK]
                acc = acc + jnp.maximum(lp[:, :s_chunk], 0.0) * iw[:, 2 * p:2 * p + 1]
                acc = acc + jnp.maximum(lp[:, s_chunk:], 0.0) * iw[:, 2 * p + 1:2 * p + 2]
            col = c * s_chunk + lax.broadcasted_iota(jnp.int32, acc.shape, 1)
            score_ref[rows, c * s_chunk:(c + 1) * s_chunk] = jnp.where(col < valid, acc, -jnp.inf)

    score = score_ref[...]
    if s_keys <= topk:
        bias_ref[...] = jnp.where(score > -jnp.inf, 0.0, -jnp.inf)
    else:
        kf = float(topk)
        score_t = score.T

        def count_ge(t):
            ind = jnp.where(score_t >= t, 1.0, 0.0).reshape(s_keys // SEARCH_PART, SEARCH_PART, qrows)
            return jnp.sum(jnp.sum(ind, axis=0), axis=0, keepdims=True)

        thr = jnp.full((1, qrows), INT_MIN, jnp.int32)
        for b in range(31, -1, -1):
            cand = thr + jnp.int32(INT_MIN if b == 31 else 1 << b)
            thr = jnp.where(count_ge(_key_to_f32(cand)) >= kf, cand, thr)
        thr = jnp.maximum(thr, jnp.int32(NEG_INF_KEY))
        t_lo = jnp.maximum(_key_to_f32(thr), -F32_MAX)
        excess = count_ge(t_lo) - kf
        stats = jnp.concatenate([t_lo, _key_to_f32(thr + 1), excess, jnp.zeros((qrows - 3, qrows), F32)], axis=0).T
        t_lo_col = stats[:, 0:1]
        bias_ref[...] = jnp.where(score >= t_lo_col, 0.0, -jnp.inf)

        @pl.when(jnp.max(excess) > 0.0)
        def _():
            _resolve_threshold_ties(score, t_lo_col, stats[:, 1:2], stats[:, 2:3], kf, s_keys,
                                    bias_ref, cand_ref, pick_ref, rem_ref)

    scale = HEAD_DIM ** -0.5
    for rows, ks in halves:
        bias = bias_ref[rows, :]
        q = q_ref[rows, :]
        for c in range(N_KV_HEADS):
            kc = get_k(ks, c)
            vc = get_v(ks, c)
            qc = jnp.concatenate(
                [q[:, (c * GROUP + g) * HEAD_DIM:(c * GROUP + g + 1) * HEAD_DIM] for g in range(GROUP)], axis=0)
            logits = lax.dot_general(qc, kc, (((1,), (1,)), ((), ())), preferred_element_type=F32)
            es, inv = [], []
            for g in range(GROUP):
                lg = logits[g * CHUNK:(g + 1) * CHUNK] + bias
                e = jnp.exp2((lg - jnp.max(lg, axis=1, keepdims=True)) * (scale * LOG2_E))
                inv.append(1.0 / jnp.sum(e, axis=1, keepdims=True))
                es.append(e.astype(BF16))
            oc = jnp.dot(jnp.concatenate(es, axis=0), vc, preferred_element_type=F32)
            for g in range(GROUP):
                hd = c * GROUP + g
                o_ref[rows, hd * HEAD_DIM:(hd + 1) * HEAD_DIM] = (
                    oc[g * CHUNK:(g + 1) * CHUNK] * inv[g]).astype(BF16)


def _dsa_body(q_ref, iq_ref, iw_ref, kb_ref, vb_ref, ikbd_ref, *rest, **static):
    head = lambda ref: (lambda ks, c: ref[ks, :, c * HEAD_DIM:(c + 1) * HEAD_DIM])
    _dsa_core(q_ref, iq_ref, iw_ref, head(kb_ref), head(vb_ref), lambda ks, c: ikbd_ref[ks, c],
              kb_ref.shape[0], *rest[-6:], **static)


def _dsa_cached_body(q_ref, iq_ref, iw_ref, kn_ref, vn_ref, ikn_ref, pk_ref, pv_ref, pik_ref,
                     o_ref, k_all, v_all, ikbd_all, *scratch, past_len, **static):
    s_keys = static['s_keys']
    new_end = past_len + CHUNK
    for ks in range(2):
        new = slice(ks * CHUNK, (ks + 1) * CHUNK)
        for c in range(N_KV_HEADS):
            for dst, past, fresh in ((k_all, pk_ref, kn_ref), (v_all, pv_ref, vn_ref)):
                dst[ks, c, :past_len, :] = past[ks, pl.ds(c, past_len, stride=N_KV_HEADS), :].astype(BF16)
                dst[ks, c, past_len:new_end, :] = fresh[new, c * HEAD_DIM:(c + 1) * HEAD_DIM]
                dst[ks, c, new_end:, :] = jnp.zeros((s_keys - new_end, HEAD_DIM), BF16)
        ik = jnp.concatenate([pik_ref[ks].astype(BF16), ikn_ref[new, :].astype(BF16),
                              jnp.zeros((s_keys - new_end, IDX_DIM), BF16)], axis=0)
        z = jnp.zeros_like(ik)
        ikbd_all[ks, :s_keys, :] = jnp.concatenate([ik, z], axis=1)
        ikbd_all[ks, s_keys:, :] = jnp.concatenate([z, ik], axis=1)
    _dsa_core(q_ref, iq_ref, iw_ref, lambda ks, c: k_all[ks, c], lambda ks, c: v_all[ks, c],
              lambda ks, c: ikbd_all[ks], 2, o_ref, *scratch, **static)


def _dsa_cached_call(q, iq, iw, kb, vb, ik, pk, pv, pik, topk):
    nb, past_len = pik.shape[0], pik.shape[1]
    l_keys = past_len + CHUNK
    s_keys = -(-l_keys // LANES) * LANES
    body = functools.partial(_dsa_cached_body, past_len=past_len, s_keys=s_keys, s_chunk=s_keys, blk0=0,
                             fixed_valid=l_keys, topk=topk)
    qrow = lambda w: pl.BlockSpec((QROWS, w), lambda n: (n, 0))
    pair = lambda a: pl.BlockSpec((2,) + a.shape[1:], lambda n: (n, 0, 0))
    mask_buf = pltpu.VMEM((QROWS, s_keys), F32)
    heads = pltpu.VMEM((2, N_KV_HEADS, s_keys, HEAD_DIM), BF16)
    return pl.pallas_call(
        body,
        grid=(nb // 2,),
        in_specs=[qrow(ATT_WIDTH), qrow(IDX_WIDTH), qrow(LANES), qrow(KV_WIDTH), qrow(KV_WIDTH), qrow(IDX_DIM),
                  pair(pk), pair(pv), pair(pik)],
        out_specs=qrow(ATT_WIDTH),
        out_shape=jax.ShapeDtypeStruct((nb * CHUNK, ATT_WIDTH), BF16),
        scratch_shapes=[heads, heads, pltpu.VMEM((2, 2 * s_keys, LANES), BF16),
                        mask_buf, mask_buf, mask_buf, mask_buf, pltpu.VMEM((QROWS, LANES), F32)],
        compiler_params=_cparams("parallel"),
        name="dsa_cached",
    )(q, iq, iw, kb, vb, ik, pk, pv, pik)


def _dsa_call(q, iq, iw, kb, vb, ikbd, o_prev, qrows, blk0, n_blk, blks_per_seq, s_keys, s_chunk, topk):
    nb = kb.shape[0]
    blocks_per_step = qrows // CHUNK
    steps_per_seq = blks_per_seq // blocks_per_step
    qrow = lambda w: pl.BlockSpec(
        (qrows, w), lambda n, j: (n * steps_per_seq + blk0 // blocks_per_step + j, 0))
    n_chunks = s_keys // s_chunk
    body = functools.partial(_dsa_body, s_keys=s_keys, s_chunk=s_chunk, blk0=blk0, fixed_valid=None, topk=topk)
    mask_buf = pltpu.VMEM((qrows, s_keys), F32)
    return pl.pallas_call(
        body,
        grid=(nb, n_blk // blocks_per_step),
        in_specs=[qrow(ATT_WIDTH), qrow(IDX_WIDTH), qrow(LANES),
                  pl.BlockSpec((1, s_keys, KV_WIDTH), lambda n, j: (n, 0, 0)),
                  pl.BlockSpec((1, s_keys, KV_WIDTH), lambda n, j: (n, 0, 0)),
                  pl.BlockSpec((1, n_chunks, 2 * s_chunk, LANES), lambda n, j: (n, 0, 0, 0)),
                  pl.BlockSpec(memory_space=pl.ANY)],
        out_specs=qrow(ATT_WIDTH),
        out_shape=jax.ShapeDtypeStruct((nb * blks_per_seq * CHUNK, ATT_WIDTH), BF16),
        scratch_shapes=[mask_buf, mask_buf, mask_buf, mask_buf, pltpu.VMEM((qrows, LANES), F32)],
        input_output_aliases={6: 0},
        compiler_params=_cparams("parallel", "arbitrary"),
        name="dsa",
    )(q, iq, iw, kb, vb, ikbd, o_prev)


def _indexer_key_blocks(ik, s_chunk):
    nb, s, _ = ik.shape
    ikb = ik.astype(BF16).reshape(nb, s // s_chunk, s_chunk, IDX_DIM)
    z = jnp.zeros_like(ikb)
    return jnp.concatenate([jnp.concatenate([ikb, z], axis=-1), jnp.concatenate([z, ikb], axis=-1)], axis=-2)


def _s5prep_body(are_ref, aim_ref, ldt_ref, bre_ref, bim_ref, abre_ref, abim_ref, bbre_ref, bbim_ref):
    a_re, a_im = are_ref[...], aim_ref[...]
    dt = jnp.exp(ldt_ref[...])
    mag = jnp.exp(dt * a_re)
    ab_re = mag * jnp.cos(dt * a_im)
    ab_im = mag * jnp.sin(dt * a_im)
    den = a_re * a_re + a_im * a_im
    f_re = ((ab_re - 1.0) * a_re + ab_im * a_im) / den
    f_im = (ab_im * a_re - (ab_re - 1.0) * a_im) / den
    abre_ref[...] = ab_re
    abim_ref[...] = ab_im
    for c in range(S5_CH):
        b_re, b_im = bre_ref[c], bim_ref[c]
        bbre_ref[c] = f_re * b_re - f_im * b_im
        bbim_ref[c] = f_re * b_im + f_im * b_re


def _s5prep_call(a_re, a_im, log_dt, b_re, b_im):
    rows = S5_LANES // LANES
    flat = lambda a: a.reshape(rows, LANES)
    ldt = jnp.broadcast_to(log_dt[:, None], (S5_GROUPS, S5_STATE))
    chan_major = lambda b: jnp.transpose(b, (2, 0, 1)).reshape(S5_CH, rows, LANES)
    small = jax.ShapeDtypeStruct((rows, LANES), F32)
    big = jax.ShapeDtypeStruct((S5_CH, rows, LANES), F32)
    return pl.pallas_call(
        _s5prep_body,
        out_shape=(small, small, big, big),
        name="s5prep",
    )(flat(a_re), flat(a_im), flat(ldt), chan_major(b_re), chan_major(b_im))


def _gelu_tanh(x):
    return 0.5 * x * (1.0 + jnp.tanh(math.sqrt(2.0 / math.pi) * (x + 0.044715 * (x * x * x))))


def _s5_body(u_ref, x0re_ref, x0im_ref, abre_ref, abim_ref, bre_ref, bim_ref, cre_ref, cim_ref,
             dskip_ref, wglu_ref, bglu_ref, ob_ref, sre_ref, sim_ref, utm, otm, xre, xim, st_re, st_im,
             *, tc, nb):
    step = pl.program_id(0)

    @pl.when(step == 0)
    def _():
        st_re[...] = x0re_ref[...]
        st_im[...] = x0im_ref[...]

    for n in range(nb):
        for k in range(S5_SLABS):
            utm[k, pl.ds(n, tc, stride=nb), :] = u_ref[n, :, k * LANES:(k + 1) * LANES]

    def project_in(k):
        us = utm[k].astype(BF16)
        sl = slice(k * S5_SLAB_STATE, (k + 1) * S5_SLAB_STATE)
        xre[:, :, sl] = jnp.dot(us, bre_ref[k], preferred_element_type=F32).reshape(tc, nb, S5_SLAB_STATE)
        xim[:, :, sl] = jnp.dot(us, bim_ref[k], preferred_element_type=F32).reshape(tc, nb, S5_SLAB_STATE)

    def scan(k):
        sl = slice(k * S5_SLAB_STATE, (k + 1) * S5_SLAB_STATE)
        a_r = jnp.broadcast_to(abre_ref[:, sl], (nb, S5_SLAB_STATE))
        a_i = jnp.broadcast_to(abim_ref[:, sl], (nb, S5_SLAB_STATE))
        s_r, s_i = st_re[:, sl], st_im[:, sl]
        for t in range(tc):
            s_r, s_i = (a_r * s_r - a_i * s_i + xre[t, :, sl],
                        a_r * s_i + a_i * s_r + xim[t, :, sl])
            xre[t, :, sl] = s_r
            xim[t, :, sl] = s_i
        st_re[:, sl] = s_r
        st_im[:, sl] = s_i

    def project_out(k):
        sl = slice(k * S5_SLAB_STATE, (k + 1) * S5_SLAB_STATE)
        xr = xre[:, :, sl].reshape(tc * nb, S5_SLAB_STATE).astype(BF16)
        xi = xim[:, :, sl].reshape(tc * nb, S5_SLAB_STATE).astype(BF16)
        return (jnp.dot(xr, cre_ref[k], preferred_element_type=F32)
                - jnp.dot(xi, cim_ref[k], preferred_element_type=F32)
                + dskip_ref[:, k * LANES:(k + 1) * LANES] * utm[k])

    project_in(0)
    ys = []
    for k in range(S5_SLABS):
        if k + 1 < S5_SLABS:
            project_in(k + 1)
        scan(k)
        ys.append(project_out(k))
    yb = _gelu_tanh(jnp.concatenate(ys, axis=1))
    gate = jax.nn.sigmoid(jnp.dot(yb.astype(BF16), wglu_ref[...], preferred_element_type=F32) + bglu_ref[...])
    o = yb * gate
    for k in range(S5_SLABS):
        otm[k] = o[:, k * LANES:(k + 1) * LANES]
    for n in range(nb):
        for k in range(S5_SLABS):
            ob_ref[n, :, k * LANES:(k + 1) * LANES] = otm[k, pl.ds(n, tc, stride=nb), :].astype(BF16)

    @pl.when(step == pl.num_programs(0) - 1)
    def _():
        sre_ref[...] = st_re[...]
        sim_ref[...] = st_im[...]


def _s5_call(u, x0_re, x0_im, ab_re, ab_im, b_re_bd, b_im_bd, c_re_bd, c_im_bd, d_skip, w_glu, b_glu, tc):
    nb, t_len, _ = u.shape
    body = functools.partial(_s5_body, tc=tc, nb=nb)
    state = jax.ShapeDtypeStruct((nb, S5_LANES), F32)
    seq_spec = pl.BlockSpec((nb, tc, S5_WIDTH), lambda i: (0, i, 0))
    slabs = pltpu.VMEM((S5_SLABS, tc * nb, LANES), F32)
    return pl.pallas_call(
        body,
        grid=(t_len // tc,),
        in_specs=[seq_spec,
                  _const_spec((nb, S5_LANES)), _const_spec((nb, S5_LANES)),
                  _const_spec((1, S5_LANES)), _const_spec((1, S5_LANES)),
                  _const_spec(b_re_bd.shape), _const_spec(b_im_bd.shape),
                  _const_spec(c_re_bd.shape), _const_spec(c_im_bd.shape),
                  _const_spec((1, S5_WIDTH)), _const_spec((S5_WIDTH, S5_WIDTH)), _const_spec((1, S5_WIDTH))],
        out_specs=(seq_spec, _const_spec((nb, S5_LANES)), _const_spec((nb, S5_LANES))),
        out_shape=(jax.ShapeDtypeStruct((nb, t_len, S5_WIDTH), BF16), state, state),
        scratch_shapes=[slabs, slabs,
                        pltpu.VMEM((tc, nb, S5_LANES), F32), pltpu.VMEM((tc, nb, S5_LANES), F32),
                        pltpu.VMEM((nb, S5_LANES), F32), pltpu.VMEM((nb, S5_LANES), F32)],
        compiler_params=_cparams("arbitrary"),
        name="s5",
    )(u, x0_re, x0_im, ab_re, ab_im, b_re_bd, b_im_bd, c_re_bd, c_im_bd, d_skip, w_glu, b_glu)


def _block_diag_slabs(w):
    g, r, c = w.shape
    w = w.reshape(S5_SLABS, S5_SLAB_GROUPS, r, c)
    eye = jnp.eye(S5_SLAB_GROUPS, dtype=w.dtype)
    bd = w[:, :, :, None, :] * eye[None, :, None, :, None]
    return bd.reshape(S5_SLABS, S5_SLAB_GROUPS * r, S5_SLAB_GROUPS * c)


def _merge_body(oa_ref, ob_ref, ga_ref, gb_ref, x_ref, wa_ref, wb_ref, wo_ref, g_ref, x1_ref, hf_ref):
    pa = jnp.dot(oa_ref[...], wa_ref[...], preferred_element_type=F32)
    pb = jnp.dot(ob_ref[...], wb_ref[...], preferred_element_type=F32)
    merged = ga_ref[...] * pa + gb_ref[...] * pb
    x1 = x_ref[...] + jnp.dot(merged.astype(BF16), wo_ref[...], preferred_element_type=F32)
    x1_ref[...] = x1
    hf = (x1 * lax.rsqrt(jnp.mean(x1 * x1, axis=-1, keepdims=True) + EPS)) * g_ref[...]
    hf_ref[...] = hf.astype(BF16)


def _merge_call(oa, ob, gates, x2d, w_a, w_b, w_o, g_ffn, tm):
    m = x2d.shape[0]
    row = lambda w: pl.BlockSpec((tm, w), lambda i: (i, 0))
    return pl.pallas_call(
        _merge_body,
        grid=(m // tm,),
        in_specs=[row(ATT_WIDTH), row(S5_WIDTH),
                  pl.BlockSpec((tm, D_MODEL), lambda i: (i, 0)), pl.BlockSpec((tm, D_MODEL), lambda i: (i, 1)),
                  row(D_MODEL), _const_spec(w_a.shape), _const_spec(w_b.shape), _const_spec(w_o.shape),
                  _const_spec((1, D_MODEL))],
        out_specs=(row(D_MODEL), row(D_MODEL)),
        out_shape=(jax.ShapeDtypeStruct((m, D_MODEL), F32), jax.ShapeDtypeStruct((m, D_MODEL), BF16)),
        compiler_params=_cparams("parallel"),
        name="merge",
    )(oa, ob, gates, gates, x2d, w_a, w_b, w_o, g_ffn)


FFN_OUT_CHUNK = 512


def _ffn_body(hf_ref, x1_hbm, wg_ref, wu_ref, wd_ref, g_ref, y_ref, x1_sem, *, final_norm, tm):
    i, f = pl.program_id(0), pl.program_id(1)
    x1_copy = pltpu.make_async_copy(x1_hbm.at[pl.ds(i * tm, tm), :], y_ref, x1_sem)

    @pl.when(f == 0)
    def _():
        x1_copy.start()

    hf = hf_ref[...]
    a = jax.nn.silu(jnp.dot(hf, wg_ref[...], preferred_element_type=F32)) * jnp.dot(
        hf, wu_ref[...], preferred_element_type=F32)
    ab = a.astype(BF16)

    @pl.when(f == 0)
    def _():
        x1_copy.wait()

    for c0 in range(0, D_MODEL, FFN_OUT_CHUNK):
        cols = slice(c0, c0 + FFN_OUT_CHUNK)
        y_ref[:, cols] += jnp.dot(ab, wd_ref[:, cols], preferred_element_type=F32)

    if final_norm:
        @pl.when(f == pl.num_programs(1) - 1)
        def _():
            x2 = y_ref[...]
            y_ref[...] = (x2 * lax.rsqrt(jnp.mean(x2 * x2, axis=-1, keepdims=True) + EPS)) * g_ref[...]


def _ffn_call(hf, x1, w_gate, w_up, w_down, g_final, tm, tf, final_norm):
    m = hf.shape[0]
    return pl.pallas_call(
        functools.partial(_ffn_body, final_norm=final_norm, tm=tm),
        grid=(m // tm, D_FF // tf),
        in_specs=[pl.BlockSpec((tm, D_MODEL), lambda i, f: (i, 0)),
                  pl.BlockSpec(memory_space=pl.ANY),
                  pl.BlockSpec((D_MODEL, tf), lambda i, f: (0, f)),
                  pl.BlockSpec((D_MODEL, tf), lambda i, f: (0, f)),
                  pl.BlockSpec((tf, D_MODEL), lambda i, f: (f, 0)),
                  _const_spec((1, D_MODEL))],
        out_specs=pl.BlockSpec((tm, D_MODEL), lambda i, f: (i, 0)),
        out_shape=jax.ShapeDtypeStruct((m, D_MODEL), F32),
        scratch_shapes=[pltpu.SemaphoreType.DMA(())],
        compiler_params=_cparams("parallel", "arbitrary"),
        name="ffn",
    )(hf, x1, w_gate, w_up, w_down, g_final)


def _rope_tables(pos, dim):
    half = dim // 2
    inv = 1.0 / (ROPE_THETA ** (jnp.arange(half, dtype=F32) * (2.0 / dim)))
    ang = pos[:, None] * inv[None, :]
    cos, sin = jnp.cos(ang), jnp.sin(ang)
    reps = LANES // dim
    return (jnp.tile(jnp.concatenate([cos, cos], axis=-1), (1, reps)),
            jnp.tile(jnp.concatenate([-sin, sin], axis=-1), (1, reps)))


def _split_w_in_body(w_ref, att_ref, idx_ref, u_ref, gl_ref):
    o = IN_OFFS
    rows = w_ref.shape[0]
    seg = lambda i, j: w_ref[:, o[i]:o[j]].astype(BF16)
    pad = lambda w: jnp.concatenate([w, jnp.zeros((rows, LANES - w.shape[1]), BF16)], axis=1)
    att_ref[...] = seg(0, 4)
    idx_ref[...] = jnp.concatenate([pad(seg(4, 5)), pad(seg(5, 6))], axis=1)
    u_ref[...] = seg(6, 7)
    gl_ref[...] = seg(7, 8)


def _split_w_in(w_in, layer):
    _, d, width = w_in.shape
    o = IN_OFFS
    widths = (o[4] - o[0], 2 * LANES, o[7] - o[6], o[8] - o[7])
    return pl.pallas_call(
        _split_w_in_body,
        grid=(d // W_IN_SPLIT_ROWS,),
        in_specs=[pl.BlockSpec((None, W_IN_SPLIT_ROWS, width), lambda i: (layer, i, 0))],
        out_specs=tuple(pl.BlockSpec((W_IN_SPLIT_ROWS, w), lambda i: (i, 0)) for w in widths),
        out_shape=tuple(jax.ShapeDtypeStruct((d, w), BF16) for w in widths),
        compiler_params=_cparams("parallel"),
        name="split_w_in",
    )(w_in.astype(BF16))


def _layer(x, pos, past, ssm0, lw):
    nb, t_len, _ = x.shape
    m = nb * t_len
    x2d = x.reshape(m, D_MODEL)

    cq, sq = _rope_tables(pos, HEAD_DIM)
    ci, si = _rope_tables(pos, IDX_DIM)
    tabs = (cq, sq, ci, si)
    if t_len < PROJ_ROWS:
        tabs = tuple(jnp.tile(t, (PROJ_ROWS // t_len, 1)) for t in tabs)

    h, q, k, v, kb, vb, iq, ik, iw, u = _proj_call(
        x2d, lw['g_mix'], lw['w_att'], lw['w_idx'], lw['w_u'], tabs, PROJ_ROWS)
    gates = _gates_call(h, lw['w_gl'], PROJ_ROWS)

    if past is None:
        s_chunk = DSA_GROUP_BLOCKS * CHUNK
        kb_all = kb.reshape(nb, t_len, KV_WIDTH)
        vb_all = vb.reshape(nb, t_len, KV_WIDTH)
        ikbd = _indexer_key_blocks(ik.reshape(nb, t_len, IDX_DIM), s_chunk)
        n_blk = t_len // CHUNK
        o_a = jnp.zeros((m, ATT_WIDTH), BF16)
        for blk0 in range(0, n_blk, DSA_GROUP_BLOCKS):
            s_keys = (blk0 + DSA_GROUP_BLOCKS) * CHUNK
            qrows = DSA_GROUP_BLOCKS * CHUNK if s_keys <= DSA_WIDE_STEP_MAX_KEYS else QROWS
            o_a = _dsa_call(q, iq, iw, kb_all, vb_all, ikbd, o_a, qrows, blk0, DSA_GROUP_BLOCKS,
                            n_blk, s_keys, s_chunk, min(TOPK_MAX, t_len // 4))
    else:
        pk, pv, pik = past
        assert t_len == CHUNK, "the cached stream is one query block per sequence"
        l_keys = pk.shape[1] + t_len
        o_a = _dsa_cached_call(q, iq, iw, kb, vb, ik, pk.reshape(nb, -1, HEAD_DIM), pv.reshape(nb, -1, HEAD_DIM),
                               pik, min(TOPK_MAX, l_keys // 4))

    x0_re, x0_im = ssm0
    o_b, s_re, s_im = _s5_call(
        u.reshape(nb, t_len, S5_WIDTH), x0_re.reshape(nb, S5_LANES), x0_im.reshape(nb, S5_LANES),
        lw['ab_re'], lw['ab_im'], lw['b_re_bd'], lw['b_im_bd'], lw['c_re_bd'], lw['c_im_bd'],
        lw['d_skip'], lw['w_glu'], lw['b_glu'], min(t_len, S5_STEP_ROWS // nb))

    x1, hf = _merge_call(o_a, o_b.reshape(m, S5_WIDTH), gates, x2d, lw['w_proj_a'], lw['w_proj_b'],
                         lw['w_out'], lw['g_ffn'], MERGE_ROWS)
    caches = (k.reshape(nb, t_len, N_KV_HEADS, HEAD_DIM), v.reshape(nb, t_len, N_KV_HEADS, HEAD_DIM),
              ik.reshape(nb, t_len, IDX_DIM), s_re.reshape(nb, S5_GROUPS, S5_STATE),
              s_im.reshape(nb, S5_GROUPS, S5_STATE))
    return x1, hf, caches


def kernel(x_prompt, x_sample, cache_k, cache_v, cache_idx_k, state_ssm_re, state_ssm_im,
           g_mix, w_in, a_re, a_im, log_dt, b_re, b_im, c_re, c_im, d_skip, w_glu, b_glu,
           w_proj_a, w_proj_b, w_out, g_ffn, w_gate, w_up, w_down, g_final):
    depth = w_in.shape[0]
    t_p, t_s = x_prompt.shape[1], x_sample.shape[1]
    past_len = cache_k.shape[2]
    pos_p = jnp.arange(t_p, dtype=F32)
    pos_s = past_len + jnp.arange(t_s, dtype=F32)
    nb_p, nb_s = x_prompt.shape[0], x_sample.shape[0]
    g_fin = g_final.reshape(1, D_MODEL)

    hp, hs = x_prompt, x_sample
    outs_p, outs_s = [], []
    for l in range(depth):
        w_att, w_idx, w_u, w_gl = _split_w_in(w_in, l)
        ab_re, ab_im, bb_re, bb_im = _s5prep_call(a_re[l], a_im[l], log_dt[l], b_re[l], b_im[l])
        per_group = lambda bb: jnp.transpose(bb.reshape(S5_CH, S5_GROUPS, S5_STATE), (1, 0, 2))
        lw = {
            'g_mix': g_mix[l].reshape(1, D_MODEL), 'w_att': w_att, 'w_idx': w_idx, 'w_u': w_u, 'w_gl': w_gl,
            'ab_re': ab_re.reshape(1, S5_LANES), 'ab_im': ab_im.reshape(1, S5_LANES),
            'b_re_bd': _block_diag_slabs(per_group(bb_re)).astype(BF16),
            'b_im_bd': _block_diag_slabs(per_group(bb_im)).astype(BF16),
            'c_re_bd': _block_diag_slabs(jnp.transpose(c_re[l], (0, 2, 1))).astype(BF16),
            'c_im_bd': _block_diag_slabs(jnp.transpose(c_im[l], (0, 2, 1))).astype(BF16),
            'd_skip': d_skip[l].reshape(1, S5_WIDTH), 'w_glu': w_glu[l].astype(BF16),
            'b_glu': b_glu[l].reshape(1, S5_WIDTH),
            'w_proj_a': w_proj_a[l].astype(BF16), 'w_proj_b': w_proj_b[l].astype(BF16),
            'w_out': w_out[l].astype(BF16), 'g_ffn': g_ffn[l].reshape(1, D_MODEL),
        }
        wg, wu, wd = w_gate[l].astype(BF16), w_up[l].astype(BF16), w_down[l].astype(BF16)
        last = l == depth - 1

        zeros = jnp.zeros((nb_p, S5_GROUPS, S5_STATE), F32)
        x1p, hfp, cp = _layer(hp, pos_p, None, (zeros, zeros), lw)
        x1s, hfs, cs = _layer(hs, pos_s, (cache_k[l], cache_v[l], cache_idx_k[l]),
                              (state_ssm_re[l], state_ssm_im[l]), lw)
        yp = _ffn_call(hfp, x1p, wg, wu, wd, g_fin, FFN_ROWS, FFN_FF_TILE, last)
        ys = _ffn_call(hfs, x1s, wg, wu, wd, g_fin, FFN_ROWS, FFN_FF_TILE, last)
        hp = yp.reshape(nb_p, t_p, D_MODEL)
        hs = ys.reshape(nb_s, t_s, D_MODEL)
        outs_p.append(cp)
        outs_s.append(cs)

    stack = lambda outs, i: jnp.stack([o[i] for o in outs])
    return (hp, hs,
            stack(outs_p, 0), stack(outs_p, 1), stack(outs_p, 2), stack(outs_p, 3), stack(outs_p, 4),
            stack(outs_s, 0), stack(outs_s, 1), stack(outs_s, 2), stack(outs_s, 3), stack(outs_s, 4))
```

```python
import functools
import math

import jax
import jax.numpy as jnp
import numpy as np
from jax import lax
from jax.experimental import pallas as pl
from jax.experimental.pallas import tpu as pltpu

F32 = jnp.float32
BF16 = jnp.bfloat16

D_MODEL = 2048
CHUNK = 64
HEAD_DIM = 128
N_HEADS = 8
N_KV_HEADS = 2
GROUP = N_HEADS // N_KV_HEADS
ATT_WIDTH = N_HEADS * HEAD_DIM
KV_WIDTH = N_KV_HEADS * HEAD_DIM
IDX_HEADS = 16
IDX_DIM = 64
IDX_WIDTH = IDX_HEADS * IDX_DIM
TOPK_MAX = 256
S5_CH = 16
S5_WIDTH = D_MODEL // 2
S5_GROUPS = S5_WIDTH // S5_CH
S5_STATE = 64
S5_LANES = S5_GROUPS * S5_STATE
D_FF = 5632
ROPE_THETA = 10000.0
EPS = 1e-6
IN_SIZES = (ATT_WIDTH, KV_WIDTH, KV_WIDTH, IDX_WIDTH, IDX_DIM, IDX_HEADS, S5_WIDTH, 2 * D_MODEL)
IN_OFFS = tuple(int(s) for s in np.cumsum((0,) + IN_SIZES))

LANES = 128
S5_SLAB_GROUPS = LANES // S5_CH
S5_SLABS = S5_GROUPS // S5_SLAB_GROUPS
S5_SLAB_STATE = S5_SLAB_GROUPS * S5_STATE
INT_MIN = -2 ** 31
DSA_GROUP_BLOCKS = 4
QROWS = 2 * CHUNK
DSA_WIDE_STEP_MAX_KEYS = 1280
S5_STEP_ROWS = 512
SEARCH_PART = 128

VMEM_LIMIT = 56 * 2 ** 20
PROJ_ROWS = 512
MERGE_ROWS = 256
FFN_ROWS = 1024
FFN_FF_TILE = 512
W_IN_SPLIT_ROWS = 256


def _cparams(*sem):
    return pltpu.CompilerParams(dimension_semantics=sem, vmem_limit_bytes=VMEM_LIMIT)


def _const_spec(shape):
    nd = len(shape)
    return pl.BlockSpec(shape, lambda *_: (0,) * nd, pipeline_mode=pl.Buffered(1))


_PQ = 0
_PK = _PQ + ATT_WIDTH
_PV = _PK + KV_WIDTH
_PIQ = _PV + KV_WIDTH
_PEND = _PIQ + IDX_WIDTH
IW_SCALE = (IDX_DIM ** -0.5) * (IDX_HEADS ** -0.5)


def _rope128(z, cos, sin):
    return z * cos + pltpu.roll(z, HEAD_DIM // 2, 1) * sin


def _rope64(z, cos, sin, low_half):
    partner = jnp.where(low_half, pltpu.roll(z, LANES - IDX_DIM // 2, 1), pltpu.roll(z, IDX_DIM // 2, 1))
    return z * cos + partner * sin


def _proj_body(x_ref, g_ref, wa_ref, wi_ref, wu_ref, cq_ref, sq_ref, ci_ref, si_ref,
               h_ref, q_ref, k_ref, v_ref, kb_ref, vb_ref, iq_ref, ik_ref, iw_ref, u_ref):
    x = x_ref[...]
    tm = x.shape[0]
    h = (x * lax.rsqrt(jnp.mean(x * x, axis=-1, keepdims=True) + EPS)) * g_ref[...]
    hb = h.astype(BF16)
    h_ref[...] = hb

    def proj(lo, hi):
        return jnp.dot(hb, wa_ref[:, lo:hi], preferred_element_type=F32)

    cq, sq, ci, si = cq_ref[...], sq_ref[...], ci_ref[...], si_ref[...]
    low_half = (lax.broadcasted_iota(jnp.int32, cq.shape, 1) & (IDX_DIM - 1)) < (IDX_DIM // 2)

    zq = proj(_PQ, _PK)
    for hd in range(N_HEADS):
        sl = slice(hd * HEAD_DIM, (hd + 1) * HEAD_DIM)
        q_ref[:, sl] = _rope128(zq[:, sl], cq, sq).astype(BF16)
    zk = proj(_PK, _PV)
    zv = proj(_PV, _PIQ)
    vb_ref[...] = zv.astype(BF16)
    for hd in range(N_KV_HEADS):
        sl = slice(hd * HEAD_DIM, (hd + 1) * HEAD_DIM)
        head_rows = pl.ds(hd, tm, stride=N_KV_HEADS)
        r = _rope128(zk[:, sl], cq, sq)
        k_ref[head_rows, :] = r
        kb_ref[:, sl] = r.astype(BF16)
        v_ref[head_rows, :] = zv[:, sl]
    ziq = proj(_PIQ, _PEND)
    for p in range(IDX_WIDTH // LANES):
        sl = slice(p * LANES, (p + 1) * LANES)
        iq_ref[:, sl] = _rope64(ziq[:, sl], ci, si, low_half).astype(BF16)
    z2 = jnp.dot(hb, wi_ref[...], preferred_element_type=F32)
    ik_ref[...] = _rope64(z2[:, :LANES], ci, si, low_half)[:, :IDX_DIM]
    iw_ref[...] = z2[:, LANES:] * IW_SCALE
    u_ref[...] = jnp.dot(hb, wu_ref[...], preferred_element_type=F32)


def _proj_call(x2d, g_mix, w_att, w_idx, w_u, tabs, tm):
    m = x2d.shape[0]
    n_tiles = m // tm
    tab_tiles = tabs[0].shape[0] // tm
    row = lambda w: pl.BlockSpec((tm, w), lambda i: (i, 0))
    tab_spec = pl.BlockSpec((tm, LANES), lambda i: (i % tab_tiles, 0))
    kv_cache = jax.ShapeDtypeStruct((m * N_KV_HEADS, HEAD_DIM), F32)
    kv_cache_spec = pl.BlockSpec((tm * N_KV_HEADS, HEAD_DIM), lambda i: (i, 0))
    out_shape = (
        jax.ShapeDtypeStruct((m, D_MODEL), BF16),
        jax.ShapeDtypeStruct((m, ATT_WIDTH), BF16),
        kv_cache,
        kv_cache,
        jax.ShapeDtypeStruct((m, KV_WIDTH), BF16),
        jax.ShapeDtypeStruct((m, KV_WIDTH), BF16),
        jax.ShapeDtypeStruct((m, IDX_WIDTH), BF16),
        jax.ShapeDtypeStruct((m, IDX_DIM), F32),
        jax.ShapeDtypeStruct((m, LANES), F32),
        jax.ShapeDtypeStruct((m, S5_WIDTH), F32),
    )
    out_specs = (row(D_MODEL), row(ATT_WIDTH), kv_cache_spec, kv_cache_spec, row(KV_WIDTH), row(KV_WIDTH),
                 row(IDX_WIDTH), row(IDX_DIM), row(LANES), row(S5_WIDTH))
    return pl.pallas_call(
        _proj_body,
        grid=(n_tiles,),
        in_specs=[row(D_MODEL), _const_spec((1, D_MODEL)),
                  _const_spec(w_att.shape), _const_spec(w_idx.shape), _const_spec(w_u.shape),
                  tab_spec, tab_spec, tab_spec, tab_spec],
        out_specs=out_specs,
        out_shape=out_shape,
        compiler_params=_cparams("parallel"),
        name="proj",
    )(x2d, g_mix, w_att, w_idx, w_u, *tabs)


LOG2_E = math.log2(math.e)
NEG_INF_KEY = INT_MIN + 0x7FFFFF
F32_MAX = float(np.finfo(np.float32).max)


def _key_to_f32(key):
    return lax.bitcast_convert_type(key ^ ((key >> 31) & jnp.int32(0x7FFFFFFF)), F32)


def _row_count(mask):
    return jnp.sum(jnp.where(mask, 1.0, 0.0), axis=1, keepdims=True)


def _resolve_threshold_ties(score, t_lo, t_next, excess, kf, s_keys, bias_ref, cand_ref, pick_ref, rem_ref):
    tied_row = excess > 0.0
    ge = score >= t_lo
    above = score >= t_next
    cand_ref[...] = jnp.where(ge & jnp.logical_not(above) & tied_row, 1.0, 0.0)
    pick_ref[...] = jnp.zeros_like(score)
    rem0 = jnp.where(tied_row, kf - _row_count(above), 0.0)
    rem_ref[...] = jnp.broadcast_to(rem0, rem_ref.shape)
    col = lax.broadcasted_iota(jnp.int32, score.shape, 1)

    def take_next_value(_):
        cand = cand_ref[...] > 0.5
        rem = rem_ref[:, :1]
        top = jnp.max(jnp.where(cand, score, -jnp.inf), axis=1, keepdims=True)
        eq = cand & (score == top)
        last = jnp.zeros((score.shape[0], 1), jnp.int32)
        for b in range(int(s_keys).bit_length() - 1, -1, -1):
            nxt = last + jnp.int32(1 << b)
            last = jnp.where(_row_count(eq & (col < nxt)) < rem, nxt, last)
        take = eq & (col <= last) & (rem > 0.0)
        pick_ref[...] = jnp.where(take, 1.0, pick_ref[...])
        cand_ref[...] = jnp.where(eq, 0.0, cand_ref[...])
        rem = jnp.where(top > -jnp.inf, rem - _row_count(take), 0.0)
        rem_ref[...] = jnp.broadcast_to(rem, rem_ref.shape)
        return jnp.max(rem) > 0.0

    lax.while_loop(lambda go: go, take_next_value, jnp.max(rem0) > 0.0)
    chosen = above | (pick_ref[...] > 0.5)
    bias_ref[...] = jnp.where(tied_row, jnp.where(chosen, 0.0, -jnp.inf), bias_ref[...])


def _dsa_core(q_ref, iq_ref, iw_ref, get_k, get_v, get_ikbd, n_keysets,
              o_ref, score_ref, bias_ref, cand_ref, pick_ref, rem_ref, *,
              s_keys, s_chunk, blk0, fixed_valid, topk):
    qrows = q_ref.shape[0]
    halves = [(slice(hf * CHUNK, (hf + 1) * CHUNK), hf % n_keysets) for hf in range(qrows // CHUNK)]

    pairs = IDX_WIDTH // LANES
    for hf, (rows, ks) in enumerate(halves):
        if fixed_valid is None:
            valid = (blk0 + len(halves) * pl.program_id(1) + hf + 1) * CHUNK
        else:
            valid = fixed_valid
        iq = iq_ref[rows, :]
        lhs = jnp.concatenate([iq[:, p * LANES:(p + 1) * LANES] for p in range(pairs)], axis=0)
        iw = iw_ref[rows, :]
        for c in range(s_keys // s_chunk):
            logits = lax.dot_general(lhs, get_ikbd(ks, c), (((1,), (1,)), ((), ())), preferred_element_type=F32)
            acc = jnp.zeros((CHUNK, s_chunk), F32)
            for p in range(pairs):
                lp = logits[p * CHUNK:(p + 1) * CHUNK]
                acc = acc + jnp.maximum(lp[:, :s_chunk], 0.0) * iw[:, 2 * p:2 * p + 1]
                acc = acc + jnp.maximum(lp[:, s_chunk:], 0.0) * iw[:, 2 * p + 1:2 * p + 2]
            col = c * s_chunk + lax.broadcasted_iota(jnp.int32, acc.shape, 1)
            score_ref[rows, c * s_chunk:(c + 1) * s_chunk] = jnp.where(col < valid, acc, -jnp.inf)

    score = score_ref[...]
    if s_keys <= topk:
        bias_ref[...] = jnp.where(score > -jnp.inf, 0.0, -jnp.inf)
    else:
        kf = float(topk)
        score_t = score.T

        def count_ge(t):
            ind = jnp.where(score_t >= t, 1.0, 0.0).reshape(s_keys // SEARCH_PART, SEARCH_PART, qrows)
            return jnp.sum(jnp.sum(ind, axis=0), axis=0, keepdims=True)

        thr = jnp.full((1, qrows), INT_MIN, jnp.int32)
        for b in range(31, -1, -1):
            cand = thr + jnp.int32(INT_MIN if b == 31 else 1 << b)
            thr = jnp.where(count_ge(_key_to_f32(cand)) >= kf, cand, thr)
        thr = jnp.maximum(thr, jnp.int32(NEG_INF_KEY))
        t_lo = jnp.maximum(_key_to_f32(thr), -F32_MAX)
        excess = count_ge(t_lo) - kf
        stats = jnp.concatenate([t_lo, _key_to_f32(thr + 1), excess, jnp.zeros((qrows - 3, qrows), F32)], axis=0).T
        t_lo_col = stats[:, 0:1]
        bias_ref[...] = jnp.where(score >= t_lo_col, 0.0, -jnp.inf)

        @pl.when(jnp.max(excess) > 0.0)
        def _():
            _resolve_threshold_ties(score, t_lo_col, stats[:, 1:2], stats[:, 2:3], kf, s_keys,
                                    bias_ref, cand_ref, pick_ref, rem_ref)

    scale = HEAD_DIM ** -0.5
    for rows, ks in halves:
        bias = bias_ref[rows, :]
        q = q_ref[rows, :]
        for c in range(N_KV_HEADS):
            kc = get_k(ks, c)
            vc = get_v(ks, c)
            qc = jnp.concatenate(
                [q[:, (c * GROUP + g) * HEAD_DIM:(c * GROUP + g + 1) * HEAD_DIM] for g in range(GROUP)], axis=0)
            logits = lax.dot_general(qc, kc, (((1,), (1,)), ((), ())), preferred_element_type=F32)
            es, inv = [], []
            for g in range(GROUP):
                lg = logits[g * CHUNK:(g + 1) * CHUNK] + bias
                e = jnp.exp2((lg - jnp.max(lg, axis=1, keepdims=True)) * (scale * LOG2_E))
                inv.append(1.0 / jnp.sum(e, axis=1, keepdims=True))
                es.append(e.astype(BF16))
            oc = jnp.dot(jnp.concatenate(es, axis=0), vc, preferred_element_type=F32)
            for g in range(GROUP):
                hd = c * GROUP + g
                o_ref[rows, hd * HEAD_DIM:(hd + 1) * HEAD_DIM] = (
                    oc[g * CHUNK:(g + 1) * CHUNK] * inv[g]).astype(BF16)


def _dsa_body(q_ref, iq_ref, iw_ref, kb_ref, vb_ref, ikbd_ref, *rest, **static):
    head = lambda ref: (lambda ks, c: ref[ks, :, c * HEAD_DIM:(c + 1) * HEAD_DIM])
    _dsa_core(q_ref, iq_ref, iw_ref, head(kb_ref), head(vb_ref), lambda ks, c: ikbd_ref[ks, c],
              kb_ref.shape[0], *rest[-6:], **static)


def _dsa_cached_body(q_ref, iq_ref, iw_ref, kn_ref, vn_ref, ikn_ref, pk_ref, pv_ref, pik_ref,
                     o_ref, k_all, v_all, ikbd_all, *scratch, past_len, **static):
    s_keys = static['s_keys']
    new_end = past_len + CHUNK
    for ks in range(2):
        new = slice(ks * CHUNK, (ks + 1) * CHUNK)
        for c in range(N_KV_HEADS):
            for dst, past, fresh in ((k_all, pk_ref, kn_ref), (v_all, pv_ref, vn_ref)):
                dst[ks, c, :past_len, :] = past[ks, pl.ds(c, past_len, stride=N_KV_HEADS), :].astype(BF16)
                dst[ks, c, past_len:new_end, :] = fresh[new, c * HEAD_DIM:(c + 1) * HEAD_DIM]
                dst[ks, c, new_end:, :] = jnp.zeros((s_keys - new_end, HEAD_DIM), BF16)
        ik = jnp.concatenate([pik_ref[ks].astype(BF16), ikn_ref[new, :].astype(BF16),
                              jnp.zeros((s_keys - new_end, IDX_DIM), BF16)], axis=0)
        z = jnp.zeros_like(ik)
        ikbd_all[ks, :s_keys, :] = jnp.concatenate([ik, z], axis=1)
        ikbd_all[ks, s_keys:, :] = jnp.concatenate([z, ik], axis=1)
    _dsa_core(q_ref, iq_ref, iw_ref, lambda ks, c: k_all[ks, c], lambda ks, c: v_all[ks, c],
              lambda ks, c: ikbd_all[ks], 2, o_ref, *scratch, **static)


def _dsa_cached_call(q, iq, iw, kb, vb, ik, pk, pv, pik, topk):
    nb, past_len = pik.shape[0], pik.shape[1]
    l_keys = past_len + CHUNK
    s_keys = -(-l_keys // LANES) * LANES
    body = functools.partial(_dsa_cached_body, past_len=past_len, s_keys=s_keys, s_chunk=s_keys, blk0=0,
                             fixed_valid=l_keys, topk=topk)
    qrow = lambda w: pl.BlockSpec((QROWS, w), lambda n: (n, 0))
    pair = lambda a: pl.BlockSpec((2,) + a.shape[1:], lambda n: (n, 0, 0))
    mask_buf = pltpu.VMEM((QROWS, s_keys), F32)
    heads = pltpu.VMEM((2, N_KV_HEADS, s_keys, HEAD_DIM), BF16)
    return pl.pallas_call(
        body,
        grid=(nb // 2,),
        in_specs=[qrow(ATT_WIDTH), qrow(IDX_WIDTH), qrow(LANES), qrow(KV_WIDTH), qrow(KV_WIDTH), qrow(IDX_DIM),
                  pair(pk), pair(pv), pair(pik)],
        out_specs=qrow(ATT_WIDTH),
        out_shape=jax.ShapeDtypeStruct((nb * CHUNK, ATT_WIDTH), BF16),
        scratch_shapes=[heads, heads, pltpu.VMEM((2, 2 * s_keys, LANES), BF16),
                        mask_buf, mask_buf, mask_buf, mask_buf, pltpu.VMEM((QROWS, LANES), F32)],
        compiler_params=_cparams("parallel"),
        name="dsa_cached",
    )(q, iq, iw, kb, vb, ik, pk, pv, pik)


def _dsa_call(q, iq, iw, kb, vb, ikbd, o_prev, qrows, blk0, n_blk, blks_per_seq, s_keys, s_chunk, topk):
    nb = kb.shape[0]
    blocks_per_step = qrows // CHUNK
    steps_per_seq = blks_per_seq // blocks_per_step
    qrow = lambda w: pl.BlockSpec(
        (qrows, w), lambda n, j: (n * steps_per_seq + blk0 // blocks_per_step + j, 0))
    n_chunks = s_keys // s_chunk
    body = functools.partial(_dsa_body, s_keys=s_keys, s_chunk=s_chunk, blk0=blk0, fixed_valid=None, topk=topk)
    mask_buf = pltpu.VMEM((qrows, s_keys), F32)
    return pl.pallas_call(
        body,
        grid=(nb, n_blk // blocks_per_step),
        in_specs=[qrow(ATT_WIDTH), qrow(IDX_WIDTH), qrow(LANES),
                  pl.BlockSpec((1, s_keys, KV_WIDTH), lambda n, j: (n, 0, 0)),
                  pl.BlockSpec((1, s_keys, KV_WIDTH), lambda n, j: (n, 0, 0)),
                  pl.BlockSpec((1, n_chunks, 2 * s_chunk, LANES), lambda n, j: (n, 0, 0, 0)),
                  pl.BlockSpec(memory_space=pl.ANY)],
        out_specs=qrow(ATT_WIDTH),
        out_shape=jax.ShapeDtypeStruct((nb * blks_per_seq * CHUNK, ATT_WIDTH), BF16),
        scratch_shapes=[mask_buf, mask_buf, mask_buf, mask_buf, pltpu.VMEM((qrows, LANES), F32)],
        input_output_aliases={6: 0},
        compiler_params=_cparams("parallel", "arbitrary"),
        name="dsa",
    )(q, iq, iw, kb, vb, ikbd, o_prev)


def _indexer_key_blocks(ik, s_chunk):
    nb, s, _ = ik.shape
    ikb = ik.astype(BF16).reshape(nb, s // s_chunk, s_chunk, IDX_DIM)
    z = jnp.zeros_like(ikb)
    return jnp.concatenate([jnp.concatenate([ikb, z], axis=-1), jnp.concatenate([z, ikb], axis=-1)], axis=-2)


def _s5prep_body(are_ref, aim_ref, ldt_ref, bre_ref, bim_ref, abre_ref, abim_ref, bbre_ref, bbim_ref):
    a_re, a_im = are_ref[...], aim_ref[...]
    dt = jnp.exp(ldt_ref[...])
    mag = jnp.exp(dt * a_re)
    ab_re = mag * jnp.cos(dt * a_im)
    ab_im = mag * jnp.sin(dt * a_im)
    den = a_re * a_re + a_im * a_im
    f_re = ((ab_re - 1.0) * a_re + ab_im * a_im) / den
    f_im = (ab_im * a_re - (ab_re - 1.0) * a_im) / den
    abre_ref[...] = ab_re
    abim_ref[...] = ab_im
    for c in range(S5_CH):
        b_re, b_im = bre_ref[c], bim_ref[c]
        bbre_ref[c] = f_re * b_re - f_im * b_im
        bbim_ref[c] = f_re * b_im + f_im * b_re


def _s5prep_call(a_re, a_im, log_dt, b_re, b_im):
    rows = S5_LANES // LANES
    flat = lambda a: a.reshape(rows, LANES)
    ldt = jnp.broadcast_to(log_dt[:, None], (S5_GROUPS, S5_STATE))
    chan_major = lambda b: jnp.transpose(b, (2, 0, 1)).reshape(S5_CH, rows, LANES)
    small = jax.ShapeDtypeStruct((rows, LANES), F32)
    big = jax.ShapeDtypeStruct((S5_CH, rows, LANES), F32)
    return pl.pallas_call(
        _s5prep_body,
        out_shape=(small, small, big, big),
        name="s5prep",
    )(flat(a_re), flat(a_im), flat(ldt), chan_major(b_re), chan_major(b_im))


def _gelu_tanh(x):
    return 0.5 * x * (1.0 + jnp.tanh(math.sqrt(2.0 / math.pi) * (x + 0.044715 * (x * x * x))))


def _s5_body(u_ref, x0re_ref, x0im_ref, abre_ref, abim_ref, bre_ref, bim_ref, cre_ref, cim_ref,
             dskip_ref, wglu_ref, bglu_ref, ob_ref, sre_ref, sim_ref, utm, otm, xre, xim, st_re, st_im,
             *, tc, nb):
    step = pl.program_id(0)

    @pl.when(step == 0)
    def _():
        st_re[...] = x0re_ref[...]
        st_im[...] = x0im_ref[...]

    for n in range(nb):
        for k in range(S5_SLABS):
            utm[k, pl.ds(n, tc, stride=nb), :] = u_ref[n, :, k * LANES:(k + 1) * LANES]

    def project_in(k):
        us = utm[k].astype(BF16)
        sl = slice(k * S5_SLAB_STATE, (k + 1) * S5_SLAB_STATE)
        xre[:, :, sl] = jnp.dot(us, bre_ref[k], preferred_element_type=F32).reshape(tc, nb, S5_SLAB_STATE)
        xim[:, :, sl] = jnp.dot(us, bim_ref[k], preferred_element_type=F32).reshape(tc, nb, S5_SLAB_STATE)

    def scan(k):
        sl = slice(k * S5_SLAB_STATE, (k + 1) * S5_SLAB_STATE)
        a_r = jnp.broadcast_to(abre_ref[:, sl], (nb, S5_SLAB_STATE))
        a_i = jnp.broadcast_to(abim_ref[:, sl], (nb, S5_SLAB_STATE))
        s_r, s_i = st_re[:, sl], st_im[:, sl]
        for t in range(tc):
            s_r, s_i = (a_r * s_r - a_i * s_i + xre[t, :, sl],
                        a_r * s_i + a_i * s_r + xim[t, :, sl])
            xre[t, :, sl] = s_r
            xim[t, :, sl] = s_i
        st_re[:, sl] = s_r
        st_im[:, sl] = s_i

    def project_out(k):
        sl = slice(k * S5_SLAB_STATE, (k + 1) * S5_SLAB_STATE)
        xr = xre[:, :, sl].reshape(tc * nb, S5_SLAB_STATE).astype(BF16)
        xi = xim[:, :, sl].reshape(tc * nb, S5_SLAB_STATE).astype(BF16)
        return (jnp.dot(xr, cre_ref[k], preferred_element_type=F32)
                - jnp.dot(xi, cim_ref[k], preferred_element_type=F32)
                + dskip_ref[:, k * LANES:(k + 1) * LANES] * utm[k])

    project_in(0)
    ys = []
    for k in range(S5_SLABS):
        if k + 1 < S5_SLABS:
            project_in(k + 1)
        scan(k)
        ys.append(project_out(k))
    yb = _gelu_tanh(jnp.concatenate(ys, axis=1))
    gate = jax.nn.sigmoid(jnp.dot(yb.astype(BF16), wglu_ref[...], preferred_element_type=F32) + bglu_ref[...])
    o = yb * gate
    for k in range(S5_SLABS):
        otm[k] = o[:, k * LANES:(k + 1) * LANES]
    for n in range(nb):
        for k in range(S5_SLABS):
            ob_ref[n, :, k * LANES:(k + 1) * LANES] = otm[k, pl.ds(n, tc, stride=nb), :].astype(BF16)

    @pl.when(step == pl.num_programs(0) - 1)
    def _():
        sre_ref[...] = st_re[...]
        sim_ref[...] = st_im[...]


def _s5_call(u, x0_re, x0_im, ab_re, ab_im, b_re_bd, b_im_bd, c_re_bd, c_im_bd, d_skip, w_glu, b_glu, tc):
    nb, t_len, _ = u.shape
    body = functools.partial(_s5_body, tc=tc, nb=nb)
    state = jax.ShapeDtypeStruct((nb, S5_LANES), F32)
    seq_spec = pl.BlockSpec((nb, tc, S5_WIDTH), lambda i: (0, i, 0))
    slabs = pltpu.VMEM((S5_SLABS, tc * nb, LANES), F32)
    return pl.pallas_call(
        body,
        grid=(t_len // tc,),
        in_specs=[seq_spec,
                  _const_spec((nb, S5_LANES)), _const_spec((nb, S5_LANES)),
                  _const_spec((1, S5_LANES)), _const_spec((1, S5_LANES)),
                  _const_spec(b_re_bd.shape), _const_spec(b_im_bd.shape),
                  _const_spec(c_re_bd.shape), _const_spec(c_im_bd.shape),
                  _const_spec((1, S5_WIDTH)), _const_spec((S5_WIDTH, S5_WIDTH)), _const_spec((1, S5_WIDTH))],
        out_specs=(seq_spec, _const_spec((nb, S5_LANES)), _const_spec((nb, S5_LANES))),
        out_shape=(jax.ShapeDtypeStruct((nb, t_len, S5_WIDTH), BF16), state, state),
        scratch_shapes=[slabs, slabs,
                        pltpu.VMEM((tc, nb, S5_LANES), F32), pltpu.VMEM((tc, nb, S5_LANES), F32),
                        pltpu.VMEM((nb, S5_LANES), F32), pltpu.VMEM((nb, S5_LANES), F32)],
        compiler_params=_cparams("arbitrary"),
        name="s5",
    )(u, x0_re, x0_im, ab_re, ab_im, b_re_bd, b_im_bd, c_re_bd, c_im_bd, d_skip, w_glu, b_glu)


def _block_diag_slabs(w):
    g, r, c = w.shape
    w = w.reshape(S5_SLABS, S5_SLAB_GROUPS, r, c)
    eye = jnp.eye(S5_SLAB_GROUPS, dtype=w.dtype)
    bd = w[:, :, :, None, :] * eye[None, :, None, :, None]
    return bd.reshape(S5_SLABS, S5_SLAB_GROUPS * r, S5_SLAB_GROUPS * c)


def _merge_body(oa_ref, ob_ref, h_ref, x_ref, wgl_ref, wa_ref, wb_ref, wo_ref, g_ref, x1_ref, hf_ref):
    h = h_ref[...]
    gate = lambda lo: jax.nn.sigmoid(jnp.dot(h, wgl_ref[:, lo:lo + D_MODEL], preferred_element_type=F32))
    merged = gate(0) * jnp.dot(oa_ref[...], wa_ref[...], preferred_element_type=F32)
    merged = merged + gate(D_MODEL) * jnp.dot(ob_ref[...], wb_ref[...], preferred_element_type=F32)
    x1 = x_ref[...] + jnp.dot(merged.astype(BF16), wo_ref[...], preferred_element_type=F32)
    x1_ref[...] = x1
    hf = (x1 * lax.rsqrt(jnp.mean(x1 * x1, axis=-1, keepdims=True) + EPS)) * g_ref[...]
    hf_ref[...] = hf.astype(BF16)


def _merge_call(oa, ob, h, x2d, w_gl, w_a, w_b, w_o, g_ffn, tm):
    m = x2d.shape[0]
    row = lambda w: pl.BlockSpec((tm, w), lambda i: (i, 0))
    return pl.pallas_call(
        _merge_body,
        grid=(m // tm,),
        in_specs=[row(ATT_WIDTH), row(S5_WIDTH), row(D_MODEL), row(D_MODEL),
                  _const_spec(w_gl.shape), _const_spec(w_a.shape), _const_spec(w_b.shape), _const_spec(w_o.shape),
                  _const_spec((1, D_MODEL))],
        out_specs=(row(D_MODEL), row(D_MODEL)),
        out_shape=(jax.ShapeDtypeStruct((m, D_MODEL), F32), jax.ShapeDtypeStruct((m, D_MODEL), BF16)),
        compiler_params=_cparams("parallel"),
        name="merge",
    )(oa, ob, h, x2d, w_gl, w_a, w_b, w_o, g_ffn)


FFN_OUT_CHUNK = 512


def _ffn_body(hf_ref, x1_hbm, wg_ref, wu_ref, wd_ref, g_ref, y_ref, x1_sem, *, final_norm, tm):
    i, f = pl.program_id(0), pl.program_id(1)
    x1_copy = pltpu.make_async_copy(x1_hbm.at[pl.ds(i * tm, tm), :], y_ref, x1_sem)

    @pl.when(f == 0)
    def _():
        x1_copy.start()

    hf = hf_ref[...]
    a = jax.nn.silu(jnp.dot(hf, wg_ref[...], preferred_element_type=F32)) * jnp.dot(
        hf, wu_ref[...], preferred_element_type=F32)
    ab = a.astype(BF16)

    @pl.when(f == 0)
    def _():
        x1_copy.wait()

    for c0 in range(0, D_MODEL, FFN_OUT_CHUNK):
        cols = slice(c0, c0 + FFN_OUT_CHUNK)
        y_ref[:, cols] += jnp.dot(ab, wd_ref[:, cols], preferred_element_type=F32)

    if final_norm:
        @pl.when(f == pl.num_programs(1) - 1)
        def _():
            x2 = y_ref[...]
            y_ref[...] = (x2 * lax.rsqrt(jnp.mean(x2 * x2, axis=-1, keepdims=True) + EPS)) * g_ref[...]


def _ffn_call(hf, x1, w_gate, w_up, w_down, g_final, tm, tf, final_norm):
    m = hf.shape[0]
    return pl.pallas_call(
        functools.partial(_ffn_body, final_norm=final_norm, tm=tm),
        grid=(m // tm, D_FF // tf),
        in_specs=[pl.BlockSpec((tm, D_MODEL), lambda i, f: (i, 0)),
                  pl.BlockSpec(memory_space=pl.ANY),
                  pl.BlockSpec((D_MODEL, tf), lambda i, f: (0, f)),
                  pl.BlockSpec((D_MODEL, tf), lambda i, f: (0, f)),
                  pl.BlockSpec((tf, D_MODEL), lambda i, f: (f, 0)),
                  _const_spec((1, D_MODEL))],
        out_specs=pl.BlockSpec((tm, D_MODEL), lambda i, f: (i, 0)),
        out_shape=jax.ShapeDtypeStruct((m, D_MODEL), F32),
        scratch_shapes=[pltpu.SemaphoreType.DMA(())],
        compiler_params=_cparams("parallel", "arbitrary"),
        name="ffn",
    )(hf, x1, w_gate, w_up, w_down, g_final)


def _rope_tables(pos, dim):
    half = dim // 2
    inv = 1.0 / (ROPE_THETA ** (jnp.arange(half, dtype=F32) * (2.0 / dim)))
    ang = pos[:, None] * inv[None, :]
    cos, sin = jnp.cos(ang), jnp.sin(ang)
    reps = LANES // dim
    return (jnp.tile(jnp.concatenate([cos, cos], axis=-1), (1, reps)),
            jnp.tile(jnp.concatenate([-sin, sin], axis=-1), (1, reps)))


def _split_w_in_body(w_ref, att_ref, idx_ref, u_ref, gl_ref):
    o = IN_OFFS
    rows = w_ref.shape[0]
    seg = lambda i, j: w_ref[:, o[i]:o[j]].astype(BF16)
    pad = lambda w: jnp.concatenate([w, jnp.zeros((rows, LANES - w.shape[1]), BF16)], axis=1)
    att_ref[...] = seg(0, 4)
    idx_ref[...] = jnp.concatenate([pad(seg(4, 5)), pad(seg(5, 6))], axis=1)
    u_ref[...] = seg(6, 7)
    gl_ref[...] = seg(7, 8)


def _split_w_in(w_in, layer):
    _, d, width = w_in.shape
    o = IN_OFFS
    widths = (o[4] - o[0], 2 * LANES, o[7] - o[6], o[8] - o[7])
    return pl.pallas_call(
        _split_w_in_body,
        grid=(d // W_IN_SPLIT_ROWS,),
        in_specs=[pl.BlockSpec((None, W_IN_SPLIT_ROWS, width), lambda i: (layer, i, 0))],
        out_specs=tuple(pl.BlockSpec((W_IN_SPLIT_ROWS, w), lambda i: (i, 0)) for w in widths),
        out_shape=tuple(jax.ShapeDtypeStruct((d, w), BF16) for w in widths),
        compiler_params=_cparams("parallel"),
        name="split_w_in",
    )(w_in)


def _layer(x, pos, past, ssm0, lw):
    nb, t_len, _ = x.shape
    m = nb * t_len
    x2d = x.reshape(m, D_MODEL)

    cq, sq = _rope_tables(pos, HEAD_DIM)
    ci, si = _rope_tables(pos, IDX_DIM)
    tabs = (cq, sq, ci, si)
    if t_len < PROJ_ROWS:
        tabs = tuple(jnp.tile(t, (PROJ_ROWS // t_len, 1)) for t in tabs)

    h, q, k, v, kb, vb, iq, ik, iw, u = _proj_call(
        x2d, lw['g_mix'], lw['w_att'], lw['w_idx'], lw['w_u'], tabs, PROJ_ROWS)

    if past is None:
        s_chunk = DSA_GROUP_BLOCKS * CHUNK
        kb_all = kb.reshape(nb, t_len, KV_WIDTH)
        vb_all = vb.reshape(nb, t_len, KV_WIDTH)
        ikbd = _indexer_key_blocks(ik.reshape(nb, t_len, IDX_DIM), s_chunk)
        n_blk = t_len // CHUNK
        o_a = jnp.zeros((m, ATT_WIDTH), BF16)
        for blk0 in range(0, n_blk, DSA_GROUP_BLOCKS):
            s_keys = (blk0 + DSA_GROUP_BLOCKS) * CHUNK
            qrows = DSA_GROUP_BLOCKS * CHUNK if s_keys <= DSA_WIDE_STEP_MAX_KEYS else QROWS
            o_a = _dsa_call(q, iq, iw, kb_all, vb_all, ikbd, o_a, qrows, blk0, DSA_GROUP_BLOCKS,
                            n_blk, s_keys, s_chunk, min(TOPK_MAX, t_len // 4))
    else:
        pk, pv, pik = past
        assert t_len == CHUNK, "the cached stream is one query block per sequence"
        l_keys = pk.shape[1] + t_len
        o_a = _dsa_cached_call(q, iq, iw, kb, vb, ik, pk.reshape(nb, -1, HEAD_DIM), pv.reshape(nb, -1, HEAD_DIM),
                               pik, min(TOPK_MAX, l_keys // 4))

    x0_re, x0_im = ssm0
    o_b, s_re, s_im = _s5_call(
        u.reshape(nb, t_len, S5_WIDTH), x0_re.reshape(nb, S5_LANES), x0_im.reshape(nb, S5_LANES),
        lw['ab_re'], lw['ab_im'], lw['b_re_bd'], lw['b_im_bd'], lw['c_re_bd'], lw['c_im_bd'],
        lw['d_skip'], lw['w_glu'], lw['b_glu'], min(t_len, S5_STEP_ROWS // nb))

    x1, hf = _merge_call(o_a, o_b.reshape(m, S5_WIDTH), h, x2d, lw['w_gl'], lw['w_proj_a'], lw['w_proj_b'],
                         lw['w_out'], lw['g_ffn'], MERGE_ROWS)
    caches = (k.reshape(nb, t_len, N_KV_HEADS, HEAD_DIM), v.reshape(nb, t_len, N_KV_HEADS, HEAD_DIM),
              ik.reshape(nb, t_len, IDX_DIM), s_re.reshape(nb, S5_GROUPS, S5_STATE),
              s_im.reshape(nb, S5_GROUPS, S5_STATE))
    return x1, hf, caches


def kernel(x_prompt, x_sample, cache_k, cache_v, cache_idx_k, state_ssm_re, state_ssm_im,
           g_mix, w_in, a_re, a_im, log_dt, b_re, b_im, c_re, c_im, d_skip, w_glu, b_glu,
           w_proj_a, w_proj_b, w_out, g_ffn, w_gate, w_up, w_down, g_final):
    depth = w_in.shape[0]
    t_p, t_s = x_prompt.shape[1], x_sample.shape[1]
    past_len = cache_k.shape[2]
    pos_p = jnp.arange(t_p, dtype=F32)
    pos_s = past_len + jnp.arange(t_s, dtype=F32)
    nb_p, nb_s = x_prompt.shape[0], x_sample.shape[0]
    g_fin = g_final.reshape(1, D_MODEL)

    hp, hs = x_prompt, x_sample
    outs_p, outs_s = [], []
    for l in range(depth):
        w_att, w_idx, w_u, w_gl = _split_w_in(w_in, l)
        ab_re, ab_im, bb_re, bb_im = _s5prep_call(a_re[l], a_im[l], log_dt[l], b_re[l], b_im[l])
        per_group = lambda bb: jnp.transpose(bb.reshape(S5_CH, S5_GROUPS, S5_STATE), (1, 0, 2))
        lw = {
            'g_mix': g_mix[l].reshape(1, D_MODEL), 'w_att': w_att, 'w_idx': w_idx, 'w_u': w_u, 'w_gl': w_gl,
            'ab_re': ab_re.reshape(1, S5_LANES), 'ab_im': ab_im.reshape(1, S5_LANES),
            'b_re_bd': _block_diag_slabs(per_group(bb_re)).astype(BF16),
            'b_im_bd': _block_diag_slabs(per_group(bb_im)).astype(BF16),
            'c_re_bd': _block_diag_slabs(jnp.transpose(c_re[l], (0, 2, 1))).astype(BF16),
            'c_im_bd': _block_diag_slabs(jnp.transpose(c_im[l], (0, 2, 1))).astype(BF16),
            'd_skip': d_skip[l].reshape(1, S5_WIDTH), 'w_glu': w_glu[l].astype(BF16),
            'b_glu': b_glu[l].reshape(1, S5_WIDTH),
            'w_proj_a': w_proj_a[l].astype(BF16), 'w_proj_b': w_proj_b[l].astype(BF16),
            'w_out': w_out[l].astype(BF16), 'g_ffn': g_ffn[l].reshape(1, D_MODEL),
        }
        wg, wu, wd = w_gate[l].astype(BF16), w_up[l].astype(BF16), w_down[l].astype(BF16)
        last = l == depth - 1

        zeros = jnp.zeros((nb_p, S5_GROUPS, S5_STATE), F32)
        x1p, hfp, cp = _layer(hp, pos_p, None, (zeros, zeros), lw)
        x1s, hfs, cs = _layer(hs, pos_s, (cache_k[l], cache_v[l], cache_idx_k[l]),
                              (state_ssm_re[l], state_ssm_im[l]), lw)
        yp = _ffn_call(hfp, x1p, wg, wu, wd, g_fin, FFN_ROWS, FFN_FF_TILE, last)
        ys = _ffn_call(hfs, x1s, wg, wu, wd, g_fin, FFN_ROWS, FFN_FF_TILE, last)
        hp = yp.reshape(nb_p, t_p, D_MODEL)
        hs = ys.reshape(nb_s, t_s, D_MODEL)
        outs_p.append(cp)
        outs_s.append(cs)

    stack = lambda outs, i: jnp.stack([o[i] for o in outs])
    return (hp, hs,
            stack(outs_p, 0), stack(outs_p, 1), stack(outs_p, 2), stack(outs_p, 3), stack(outs_p, 4),
            stack(outs_s, 0), stack(outs_s, 1), stack(outs_s, 2), stack(outs_s, 3), stack(outs_s, 4))
```
